```python
import math
import jax, jax.numpy as jnp
from jax import lax
import numpy as np

D_MODEL = 1024
BATCH = 8
SEQ = 4096
DEPTH = 1

NSA_HEADS = 8
NSA_HEAD_DIM = 64
NSA_KV_GROUPS = 2
NSA_WIDTH = NSA_HEADS * NSA_HEAD_DIM
NSA_KV_WIDTH = NSA_KV_GROUPS * NSA_HEAD_DIM
CMP_LEN = 32
CMP_STRIDE = 16
CMP_HIDDEN = 256
SLC_BLOCK = 64
SLC_TOPK = 16
WINDOW = 512
QUERY_BLOCK = 64
BIG = 1e9

SSD_HEADS = 8
SSD_HEAD_DIM = 64
SSD_WIDTH = SSD_HEADS * SSD_HEAD_DIM
SSD_GROUPS = 2
SSD_STATE = 128
SSD_CONV = 4
SSD_CHUNK = 128
DT_MIN = 0.001
DT_MAX = 0.1

MEM_LEN = 256
XA_HEADS = 4
XA_HEAD_DIM = 128
XA_WIDTH = XA_HEADS * XA_HEAD_DIM

MIX_WIDTH = NSA_WIDTH + SSD_WIDTH + XA_WIDTH
ROPE_THETA = 500000.0
ROPE_DIM = NSA_HEAD_DIM // 4
POS_OFFSET_MAX = 1024
EPS = 1e-6

kernel_name = "hymba_nsa_ssd_memory_layer"


def _in_proj_sizes():
    return [NSA_WIDTH,
            NSA_KV_WIDTH, NSA_KV_WIDTH,
            NSA_KV_WIDTH, NSA_KV_WIDTH,
            NSA_KV_WIDTH, NSA_KV_WIDTH,
            NSA_HEADS * 3,
            NSA_WIDTH,
            SSD_WIDTH,
            SSD_WIDTH + 2 * SSD_GROUPS * SSD_STATE,
            SSD_HEADS,
            XA_WIDTH,
            XA_WIDTH]


def _rmsnorm(x, g):
    xf = x.astype(jnp.float32)
    y = xf * lax.rsqrt(jnp.mean(xf * xf, axis=-1, keepdims=True) + EPS)
    return (y * g.astype(jnp.float32)).astype(x.dtype)


def _masked_softmax(s, mask):
    s = jnp.where(mask, s.astype(jnp.float32), -1e30)
    s = s - jnp.max(s, axis=-1, keepdims=True)
    p = jnp.exp(s) * mask
    return p / jnp.maximum(jnp.sum(p, axis=-1, keepdims=True), 1e-30)


def _rope_tables(pos):
    inv = ROPE_THETA ** (-jnp.arange(0, ROPE_DIM, 2, dtype=jnp.float32) / ROPE_DIM)
    ang = pos.astype(jnp.float32)[..., None] * inv
    return jnp.cos(ang), jnp.sin(ang)


def _apply_rope(x, cos, sin):
    half = ROPE_DIM // 2
    shp = cos.shape[:2] + (1,) * (x.ndim - 3) + (half,)
    c = cos.reshape(shp).astype(x.dtype)
    s = sin.reshape(shp).astype(x.dtype)
    x1, x2, rest = x[..., :half], x[..., half:ROPE_DIM], x[..., ROPE_DIM:]
    return jnp.concatenate([x1 * c - x2 * s, x2 * c + x1 * s, rest], axis=-1)


def _nsa_mixer(q, k_c, v_c, k_s, v_s, k_w, v_w, gate_logits, positions,
               pos_k, w1_k, w2_k, pos_v, w1_v, w2_v):
    B, S = q.shape[0], q.shape[1]
    G, E, Dh = NSA_KV_GROUPS, NSA_HEADS // NSA_KV_GROUPS, NSA_HEAD_DIM
    scale = Dh ** -0.5
    q = q.reshape(B, S, G, E, Dh)
    k_c, v_c, k_s, v_s, k_w, v_w = [a.reshape(B, S, G, Dh) for a in (k_c, v_c, k_s, v_s, k_w, v_w)]
    cos, sin = _rope_tables(positions)
    q = _apply_rope(q, cos, sin)
    k_s = _apply_rope(k_s, cos, sin)
    k_w = _apply_rope(k_w, cos, sin)
    t = jnp.arange(S)

    R = CMP_LEN // CMP_STRIDE
    n_cmp = S // CMP_STRIDE - (R - 1)

    def compress(a, pos_emb, w1, w2):
        chunks = a.reshape(B, S // CMP_STRIDE, CMP_STRIDE, G, Dh)
        blocks = jnp.concatenate([chunks[:, r:r + n_cmp] for r in range(R)], axis=2)
        blocks = blocks + pos_emb[:, None, :]
        flat = blocks.transpose(0, 1, 3, 2, 4).reshape(B, n_cmp, G, CMP_LEN * Dh)
        return jax.nn.silu(flat @ w1) @ w2

    kc = compress(k_c, pos_k, w1_k, w2_k)
    vc = compress(v_c, pos_v, w1_v, w2_v)
    cmp_end = jnp.arange(n_cmp) * CMP_STRIDE + CMP_LEN - 1
    cos_c, sin_c = _rope_tables(positions[:, cmp_end])
    kc = _apply_rope(kc, cos_c, sin_c)
    s_c = jnp.einsum('bsged,bngd->bgesn', q, kc) * scale
    p_c = _masked_softmax(s_c, cmp_end[None, :] <= t[:, None])
    o_c = jnp.einsum('bgesn,bngd->bsged', p_c.astype(vc.dtype), vc)

    n_slc = S // SLC_BLOCK
    topk = min(SLC_TOPK, n_slc)
    cmp_start = cmp_end - (CMP_LEN - 1)
    slc_start = jnp.arange(n_slc) * SLC_BLOCK
    overlap = ((cmp_start[:, None] < slc_start[None, :] + SLC_BLOCK)
               & (cmp_start[:, None] + CMP_LEN > slc_start[None, :])).astype(jnp.float32)
    imp = jnp.einsum('bgesn,nj->bgsj', p_c, overlap)
    cur = t // SLC_BLOCK
    j = jnp.arange(n_slc)
    forced = (j[None, :] == 0) | (j[None, :] == cur[:, None]) | (j[None, :] == cur[:, None] - 1)
    valid = j[None, :] <= cur[:, None]
    imp = jnp.where(forced, BIG, jnp.where(valid, imp, -BIG))
    _, sel = lax.top_k(imp, topk)

    k_blk = k_s.reshape(B, n_slc, SLC_BLOCK, G, Dh).transpose(0, 3, 1, 2, 4)
    v_blk = v_s.reshape(B, n_slc, SLC_BLOCK, G, Dh).transpose(0, 3, 1, 2, 4)
    k_pad = jnp.pad(k_w, ((0, 0), (WINDOW, 0), (0, 0), (0, 0)))
    v_pad = jnp.pad(v_w, ((0, 0), (WINDOW, 0), (0, 0), (0, 0)))
    b_ix = jnp.arange(B)[:, None, None, None]
    g_ix = jnp.arange(G)[None, :, None, None]
    in_blk = jnp.arange(SLC_BLOCK)
    span = WINDOW + QUERY_BLOCK
    w_off = jnp.arange(span)
    n_sel_keys = topk * SLC_BLOCK

    def query_block(i):
        s0 = i * QUERY_BLOCK
        tq = s0 + jnp.arange(QUERY_BLOCK)
        qb = lax.dynamic_slice_in_dim(q, s0, QUERY_BLOCK, axis=1)
        sb = lax.dynamic_slice_in_dim(sel, s0, QUERY_BLOCK, axis=2)
        ks = k_blk[b_ix, g_ix, sb].reshape(B, G, QUERY_BLOCK, n_sel_keys, Dh)
        vs = v_blk[b_ix, g_ix, sb].reshape(B, G, QUERY_BLOCK, n_sel_keys, Dh)
        kpos = (sb[..., None] * SLC_BLOCK + in_blk).reshape(B, G, QUERY_BLOCK, n_sel_keys)
        ps = _masked_softmax(jnp.einsum('bqged,bgqkd->bgeqk', qb, ks) * scale,
                             (kpos <= tq[:, None])[:, :, None])
        o_s = jnp.einsum('bgeqk,bgqkd->bqged', ps.astype(vs.dtype), vs)
        kw = lax.dynamic_slice_in_dim(k_pad, s0, span, axis=1)
        vw = lax.dynamic_slice_in_dim(v_pad, s0, span, axis=1)
        kpos_w = s0 - WINDOW + w_off
        mw = ((kpos_w[None, :] <= tq[:, None]) & (kpos_w[None, :] > tq[:, None] - WINDOW)
              & (kpos_w[None, :] >= 0))
        pw = _masked_softmax(jnp.einsum('bqged,bkgd->bgeqk', qb, kw) * scale, mw)
        o_w = jnp.einsum('bgeqk,bkgd->bqged', pw.astype(vw.dtype), vw)
        return o_s, o_w

    o_s, o_w = lax.map(query_block, jnp.arange(S // QUERY_BLOCK))
    o_s = jnp.moveaxis(o_s, 0, 1).reshape(B, S, G, E, Dh)
    o_w = jnp.moveaxis(o_w, 0, 1).reshape(B, S, G, E, Dh)

    gates = jax.nn.sigmoid(gate_logits.astype(jnp.float32)).astype(q.dtype).reshape(B, S, G, E, 3)
    o = gates[..., 0:1] * o_c + gates[..., 1:2] * o_s + gates[..., 2:3] * o_w
    return o.reshape(B, S, NSA_WIDTH)


def _ssd_mixer(z, xbc, dt, conv_w, conv_b, dt_bias, a_log, d_skip, g_norm):
    B, S = xbc.shape[0], xbc.shape[1]
    H, P, G, N, L = SSD_HEADS, SSD_HEAD_DIM, SSD_GROUPS, SSD_STATE, SSD_CHUNK
    E = H // G
    nc = S // L
    f32 = jnp.float32
    xp = jnp.pad(xbc, ((0, 0), (SSD_CONV - 1, 0), (0, 0)))
    acc = conv_b
    for k in range(SSD_CONV):
        acc = acc + xp[:, k:k + S] * conv_w[k]
    xbc = jax.nn.silu(acc)
    xs, bm, cm = jnp.split(xbc, [SSD_WIDTH, SSD_WIDTH + G * N], axis=-1)
    xs = xs.astype(f32).reshape(B, nc, L, G, E, P)
    bm = bm.astype(f32).reshape(B, nc, L, G, N)
    cm = cm.astype(f32).reshape(B, nc, L, G, N)
    dt = jax.nn.softplus(dt.astype(f32) + dt_bias.astype(f32)).reshape(B, nc, L, G, E)
    a = -jnp.exp(a_log.astype(f32)).reshape(G, E)
    a_cs = jnp.cumsum((dt * a).transpose(0, 1, 3, 4, 2), axis=-1)
    causal = jnp.tril(jnp.ones((L, L), dtype=bool))
    decay_in = jnp.exp(jnp.where(causal, a_cs[..., :, None] - a_cs[..., None, :], -jnp.inf))
    xdt = xs * dt[..., None]
    cb = jnp.einsum('bclgn,bcsgn->bcgls', cm, bm)
    y_diag = jnp.einsum('bcgels,bcsgep->bclgep', cb[:, :, :, None] * decay_in, xdt)
    decay_out = jnp.exp(a_cs[..., -1:] - a_cs)
    states = jnp.einsum('bclgn,bcgel,bclgep->bcgepn', bm, decay_out, xdt)
    chunk_decay = jnp.exp(a_cs[..., -1])

    def step(h, inp):
        st, dec = inp
        return h * dec[..., None, None] + st, h

    h0 = jnp.zeros((B, G, E, P, N), f32)
    _, prev = lax.scan(step, h0, (jnp.moveaxis(states, 1, 0), jnp.moveaxis(chunk_decay, 1, 0)))
    prev = jnp.moveaxis(prev, 0, 1)
    y_off = jnp.einsum('bclgn,bcgepn,bcgel->bclgep', cm, prev, jnp.exp(a_cs))
    y = y_diag + y_off + xs * d_skip.astype(f32).reshape(G, E, 1)
    y = y.reshape(B, S, SSD_WIDTH)
    return _rmsnorm(y * jax.nn.silu(z.astype(f32)), g_norm).astype(z.dtype)


def _memory_attention(q, mem, g_mem, w_mem_kv):
    B, S = q.shape[0], q.shape[1]
    M = mem.shape[1]
    q = q.reshape(B, S, XA_HEADS, XA_HEAD_DIM)
    kv = _rmsnorm(mem, g_mem) @ w_mem_kv
    k, v = jnp.split(kv, 2, axis=-1)
    k = k.reshape(B, M, XA_HEADS, XA_HEAD_DIM)
    v = v.reshape(B, M, XA_HEADS, XA_HEAD_DIM)
    s = jnp.einsum('bshd,bmhd->bhsm', q, k).astype(jnp.float32) * (XA_HEAD_DIM ** -0.5)
    p = jax.nn.softmax(s, axis=-1).astype(v.dtype)
    return jnp.einsum('bhsm,bmhd->bshd', p, v).reshape(B, S, XA_WIDTH)


def setup_inputs(seed: int = 0) -> dict:
    key = jax.random.key(seed)
    ks = jax.random.split(key, 24)
    f32 = jnp.float32

    def nrm(k, shape, scale):
        return jax.random.normal(k, shape, f32) * scale

    n_in = sum(_in_proj_sizes())
    conv_dim = SSD_WIDTH + 2 * SSD_GROUPS * SSD_STATE
    dt0 = jnp.exp(jax.random.uniform(ks[14], (DEPTH, SSD_HEADS), f32)
                  * (math.log(DT_MAX) - math.log(DT_MIN)) + math.log(DT_MIN))
    return {
        "x": nrm(ks[0], (BATCH, SEQ, D_MODEL), 1.0),
        "mem": nrm(ks[1], (BATCH, MEM_LEN, D_MODEL), 1.0),
        "positions": (jnp.arange(SEQ, dtype=jnp.int32)[None, :]
                      + jax.random.randint(ks[2], (BATCH, 1), 0, POS_OFFSET_MAX, dtype=jnp.int32)),
        "g_in": 1.0 + nrm(ks[3], (DEPTH, D_MODEL), 0.02),
        "w_in": nrm(ks[4], (DEPTH, D_MODEL, n_in), D_MODEL ** -0.5),
        "cmp_pos_k": nrm(ks[5], (DEPTH, CMP_LEN, NSA_HEAD_DIM), 0.02),
        "w_cmp1_k": nrm(ks[6], (DEPTH, CMP_LEN * NSA_HEAD_DIM, CMP_HIDDEN), (CMP_LEN * NSA_HEAD_DIM) ** -0.5),
        "w_cmp2_k": nrm(ks[7], (DEPTH, CMP_HIDDEN, NSA_HEAD_DIM), CMP_HIDDEN ** -0.5),
        "cmp_pos_v": nrm(ks[8], (DEPTH, CMP_LEN, NSA_HEAD_DIM), 0.02),
        "w_cmp1_v": nrm(ks[9], (DEPTH, CMP_LEN * NSA_HEAD_DIM, CMP_HIDDEN), (CMP_LEN * NSA_HEAD_DIM) ** -0.5),
        "w_cmp2_v": nrm(ks[10], (DEPTH, CMP_HIDDEN, NSA_HEAD_DIM), CMP_HIDDEN ** -0.5),
        "conv_w": nrm(ks[11], (DEPTH, SSD_CONV, conv_dim), SSD_CONV ** -0.5),
        "conv_b": nrm(ks[12], (DEPTH, conv_dim), 0.02),
        "dt_bias": dt0 + jnp.log(-jnp.expm1(-dt0)),
        "a_log": jnp.log(jax.random.uniform(ks[15], (DEPTH, SSD_HEADS), f32, 1.0, 16.0)),
        "d_skip": 1.0 + nrm(ks[16], (DEPTH, SSD_HEADS), 0.02),
        "g_ssd_norm": 1.0 + nrm(ks[17], (DEPTH, SSD_WIDTH), 0.02),
        "g_mem": 1.0 + nrm(ks[18], (DEPTH, D_MODEL), 0.02),
        "w_mem_kv": nrm(ks[19], (DEPTH, D_MODEL, 2 * XA_WIDTH), D_MODEL ** -0.5),
        "w_out": nrm(ks[20], (DEPTH, MIX_WIDTH, D_MODEL), MIX_WIDTH ** -0.5),
        "g_final": 1.0 + nrm(ks[21], (D_MODEL,), 0.02),
    }


def reference(x, mem, positions, g_in, w_in, cmp_pos_k, w_cmp1_k, w_cmp2_k, cmp_pos_v, w_cmp1_v,
              w_cmp2_v, conv_w, conv_b, dt_bias, a_log, d_skip, g_ssd_norm, g_mem, w_mem_kv, w_out,
              g_final):
    split_at = [int(v) for v in np.cumsum(_in_proj_sizes())[:-1]]
    h = x
    for l in range(DEPTH):
        xn = _rmsnorm(h, g_in[l])
        proj = xn @ w_in[l]
        (q_a, k_c, v_c, k_s, v_s, k_w, v_w, gate_logits, gate_a,
         z, xbc, dt, q_x, gate_x) = jnp.split(proj, split_at, axis=-1)
        o_a = _nsa_mixer(q_a, k_c, v_c, k_s, v_s, k_w, v_w, gate_logits, positions,
                         cmp_pos_k[l], w_cmp1_k[l], w_cmp2_k[l], cmp_pos_v[l], w_cmp1_v[l], w_cmp2_v[l])
        o_a = o_a * jax.nn.silu(gate_a)
        o_b = _ssd_mixer(z, xbc, dt, conv_w[l], conv_b[l], dt_bias[l], a_log[l], d_skip[l], g_ssd_norm[l])
        o_c = _memory_attention(q_x, mem, g_mem[l], w_mem_kv[l]) * jax.nn.silu(gate_x)
        h = h + jnp.concatenate([o_a, o_b, o_c], axis=-1) @ w_out[l]
    return _rmsnorm(h, g_final)
```

```python
import functools

import jax
import jax.numpy as jnp
from jax import lax
from jax.experimental import pallas as pl
from jax.experimental.pallas import tpu as pltpu

F32 = jnp.float32
BF16 = jnp.bfloat16
I32 = jnp.int32

D_MODEL = 1024
NSA_HEADS = 8
HEAD_DIM = 64
KV_GROUPS = 2
HEADS_PER_GROUP = NSA_HEADS // KV_GROUPS
NSA_WIDTH = NSA_HEADS * HEAD_DIM
KV_WIDTH = KV_GROUPS * HEAD_DIM
CMP_LEN = 32
CMP_STRIDE = 16
CMP_HIDDEN = 256
SLC_BLOCK = 64
SLC_TOPK = 16
SLC_SLOTS = 64
WINDOW = 512
BIG = 1e9
NEG = -1e30

SSD_HEADS = 8
SSD_HEAD_DIM = 64
SSD_WIDTH = SSD_HEADS * SSD_HEAD_DIM
SSD_GROUPS = 2
SSD_STATE = 128
SSD_CONV = 4
SSD_CHUNK = 128
CONV_DIM = SSD_WIDTH + 2 * SSD_GROUPS * SSD_STATE

XA_HEADS = 4
XA_HEAD_DIM = 128
XA_WIDTH = XA_HEADS * XA_HEAD_DIM
MIX_WIDTH = NSA_WIDTH + SSD_WIDTH + XA_WIDTH

ROPE_THETA = 500000.0
ROPE_DIM = HEAD_DIM // 4
ROPE_HALF = ROPE_DIM // 2
EPS = 1e-6

LANES = 128
GATE_LANE0 = 0
DT_LANE0 = 24

OFF_Q = 0
OFF_KV = OFF_Q + NSA_WIDTH
OFF_GA = OFF_KV + 6 * KV_WIDTH
OFF_Z = OFF_GA + NSA_WIDTH
OFF_XBC = OFF_Z + SSD_WIDTH
OFF_QX = OFF_XBC + CONV_DIM
OFF_GX = OFF_QX + XA_WIDTH
OFF_SM = OFF_GX + XA_WIDTH
N_PROJ = OFF_SM + LANES

ROW_TILE = 512
NSA_TILE = 256
SSD_TILE = 512
VMEM_LIMIT = 56 * 1024 * 1024


def _nt(a, b):
    return lax.dot_general(a, b, (((1,), (1,)), ((), ())), preferred_element_type=F32)


def _dot(a, b):
    return jnp.dot(a, b, preferred_element_type=F32)


def _split3(x):
    hi = x.astype(BF16)
    r1 = x - hi.astype(F32)
    mid = r1.astype(BF16)
    lo = (r1 - mid.astype(F32)).astype(BF16)
    return hi, mid, lo


def _dot3_l(x, w):
    hi, mid, lo = _split3(x)
    return _dot(hi, w) + _dot(mid, w) + _dot(lo, w)


def _dot3_r(w, x):
    hi, mid, lo = _split3(x)
    return _dot(w, hi) + _dot(w, mid) + _dot(w, lo)


def _silu(x):
    return x * jax.nn.sigmoid(x)


def _rope(a, cs, sn, first):
    r = jnp.where(first, pltpu.roll(a, LANES - ROPE_HALF, 1), pltpu.roll(a, ROPE_HALF, 1))
    return a * cs + r * sn


def _rope_tables(pos_f32, invl, sgn):
    ang = pos_f32 * invl
    return jnp.cos(ang), jnp.sin(ang) * sgn


def _inproj_body(x_ref, pos_ref, g_ref, w_ref, invl_ref, sgn_ref,
                 q_ref, kc_ref, vc_ref, ksa_ref, vs_ref, kw_ref, vw_ref,
                 ga_ref, z_ref, xbc_ref, qx_ref, gx_ref, sm_ref, *, tm, tiles_per_seq):
    x = x_ref[...]
    ms = jnp.mean(x * x, axis=-1, keepdims=True)
    xn = (x * lax.rsqrt(ms + EPS) * g_ref[...]).astype(BF16)

    cs, sn = _rope_tables(pos_ref[...].astype(F32), invl_ref[...], sgn_ref[...])
    lane = lax.broadcasted_iota(I32, (tm, LANES), 1)
    first = (lane % HEAD_DIM) < ROPE_HALF

    def mm(lo, n):
        return _dot(xn, w_ref[:, lo:lo + n])

    qf = mm(OFF_Q, NSA_WIDTH)
    for c in range(NSA_WIDTH // LANES):
        sl = slice(c * LANES, (c + 1) * LANES)
        q_ref[:, sl] = (_rope(qf[:, sl], cs, sn, first) * (HEAD_DIM ** -0.5)).astype(BF16)

    kv = mm(OFF_KV, 6 * KV_WIDTH)
    k_c, v_c, k_s, v_s, k_w, v_w = [kv[:, n * KV_WIDTH:(n + 1) * KV_WIDTH] for n in range(6)]
    k_s = _rope(k_s, cs, sn, first)
    k_w = _rope(k_w, cs, sn, first)
    s_base = (pl.program_id(0) % tiles_per_seq) * tm
    blk = (s_base + lax.broadcasted_iota(I32, (tm, LANES), 0)) // SLC_BLOCK
    onehot = jnp.where(lane - HEAD_DIM == blk, 1.0, 0.0)
    low = lane < HEAD_DIM
    for g in range(KV_GROUPS):
        gs = slice(g * HEAD_DIM, (g + 1) * HEAD_DIM)
        kc_ref[0, g] = k_c[:, gs].astype(BF16)
        vc_ref[0, g] = v_c[:, gs].astype(BF16)
        vs_ref[0, g] = v_s[:, gs].astype(BF16)
        kw_ref[0, g] = k_w[:, gs].astype(BF16)
        vw_ref[0, g] = v_w[:, gs].astype(BF16)
        ks_g = k_s if g == 0 else pltpu.roll(k_s, HEAD_DIM, 1)
        ksa_ref[0, g] = jnp.where(low, ks_g, onehot).astype(BF16)

    ga_ref[...] = mm(OFF_GA, NSA_WIDTH)
    z_ref[...] = mm(OFF_Z, SSD_WIDTH)
    xbc_ref[...] = mm(OFF_XBC, CONV_DIM)
    qx_ref[...] = (mm(OFF_QX, XA_WIDTH) * (XA_HEAD_DIM ** -0.5)).astype(BF16)
    gx_ref[...] = mm(OFF_GX, XA_WIDTH)
    sm_ref[...] = mm(OFF_SM, LANES)


def _in_proj(x2, pos2, g_in, w_p, invl, sgn, *, batch, seq):
    rows = batch * seq
    tm = ROW_TILE
    tps = seq // tm
    row = lambda n: pl.BlockSpec((tm, n), lambda r: (r, 0))
    full = lambda a: pl.BlockSpec(a.shape, lambda r: (0,) * a.ndim)
    grp = lambda n: pl.BlockSpec((1, KV_GROUPS, tm, n), lambda r: (r // tps, 0, r % tps, 0))
    grp_shape = lambda n: jax.ShapeDtypeStruct((batch, KV_GROUPS, seq, n), BF16)
    flat = lambda n, dt: jax.ShapeDtypeStruct((rows, n), dt)
    return pl.pallas_call(
        functools.partial(_inproj_body, tm=tm, tiles_per_seq=tps),
        grid=(rows // tm,),
        in_specs=[row(D_MODEL), row(1), full(g_in), full(w_p), full(invl), full(sgn)],
        out_specs=[row(NSA_WIDTH), grp(HEAD_DIM), grp(HEAD_DIM), grp(LANES), grp(HEAD_DIM),
                   grp(HEAD_DIM), grp(HEAD_DIM), row(NSA_WIDTH), row(SSD_WIDTH), row(CONV_DIM),
                   row(XA_WIDTH), row(XA_WIDTH), row(LANES)],
        out_shape=[flat(NSA_WIDTH, BF16), grp_shape(HEAD_DIM), grp_shape(HEAD_DIM),
                   grp_shape(LANES), grp_shape(HEAD_DIM), grp_shape(HEAD_DIM), grp_shape(HEAD_DIM),
                   flat(NSA_WIDTH, F32), flat(SSD_WIDTH, F32), flat(CONV_DIM, F32),
                   flat(XA_WIDTH, BF16), flat(XA_WIDTH, F32), flat(LANES, F32)],
        compiler_params=pltpu.CompilerParams(dimension_semantics=("parallel",),
                                             vmem_limit_bytes=VMEM_LIMIT),
        name="in_proj",
    )(x2, pos2, g_in, w_p, invl, sgn)


def _compress_body(ck_ref, cv_ref, posc_ref, w1k_ref, w2k_ref, pk_ref, w1v_ref, w2v_ref, pv_ref,
                   invl_ref, sgn_ref, kc_ref, vc_ref, *, ncp):
    half = CMP_STRIDE * HEAD_DIM

    def mlp(c_ref, w1_ref, w2_ref, p_ref):
        c = c_ref[0, 0]
        a = _dot(c, w1_ref[0:half, :])
        b = _dot(c, w1_ref[half:2 * half, :])
        bias = _dot(p_ref[...], w1_ref[...])[0:1, :]
        h = a + pltpu.roll(b, ncp - 1, 0) + bias
        return _dot(_silu(h).astype(BF16), w2_ref[...])

    kc = mlp(ck_ref, w1k_ref, w2k_ref, pk_ref)
    vc = mlp(cv_ref, w1v_ref, w2v_ref, pv_ref)
    cs, sn = _rope_tables(posc_ref[0].astype(F32), invl_ref[...], sgn_ref[...])
    lane = lax.broadcasted_iota(I32, (ncp, LANES), 1)
    kc = _rope(kc, cs, sn, (lane % HEAD_DIM) < ROPE_HALF)
    kc_ref[0, 0] = kc[:, 0:HEAD_DIM].astype(BF16)
    vc_ref[0, 0] = vc[:, 0:HEAD_DIM].astype(BF16)


def _compress(ck, cv, posc, w1k, w2k, pk, w1v, w2v, pv, invl, sgn, *, batch, ncp):
    chunk = pl.BlockSpec((1, 1, ncp, CMP_STRIDE * HEAD_DIM), lambda b, g: (b, g, 0, 0))
    full = lambda a: pl.BlockSpec(a.shape, lambda b, g: (0,) * a.ndim)
    out = pl.BlockSpec((1, 1, ncp, HEAD_DIM), lambda b, g: (b, g, 0, 0))
    shp = jax.ShapeDtypeStruct((batch, KV_GROUPS, ncp, HEAD_DIM), BF16)
    return pl.pallas_call(
        functools.partial(_compress_body, ncp=ncp),
        grid=(batch, KV_GROUPS),
        in_specs=[chunk, chunk, pl.BlockSpec((1, ncp, 1), lambda b, g: (b, 0, 0)),
                  full(w1k), full(w2k), full(pk), full(w1v), full(w2v), full(pv),
                  full(invl), full(sgn)],
        out_specs=[out, out],
        out_shape=[shp, shp],
        compiler_params=pltpu.CompilerParams(dimension_semantics=("parallel", "parallel"),
                                             vmem_limit_bytes=VMEM_LIMIT),
        name="compress",
    )(ck, cv, posc, w1k, w2k, pk, w1v, w2v, pv, invl, sgn)


def _nsa_body(q_ref, kc_ref, vct_ref, ksa_ref, vs_ref, kw_ref, vw_ref, ga_ref, sm_ref, ovl_ref,
              o_ref, qa_ref, m_ref, l_ref, acc_ref, oc_ref, os_ref, *, tq, ncp, topk):
    g = pl.program_id(1)
    i = pl.program_id(2)
    s0 = i * tq
    nh = HEADS_PER_GROUP
    rows = nh * tq

    q4 = q_ref[...]
    kc = kc_ref[0, 0]
    n_io = lax.broadcasted_iota(I32, (ncp, tq), 0)
    t_io = s0 + lax.broadcasted_iota(I32, (ncp, tq), 1)
    cmask = (n_io * CMP_STRIDE + (CMP_LEN - 1)) <= t_io
    cmask_f = jnp.where(cmask, 1.0, 0.0)
    p_sum = jnp.zeros((ncp, tq), F32)
    for e in range(nh):
        qe = q4[:, e * HEAD_DIM:(e + 1) * HEAD_DIM]
        qa_ref[e * tq:(e + 1) * tq, 0:HEAD_DIM] = qe
        st = jnp.where(cmask, _nt(kc, qe), NEG)
        st = st - jnp.max(st, axis=0, keepdims=True)
        p = jnp.exp(st) * cmask_f
        p = p / jnp.maximum(jnp.sum(p, axis=0, keepdims=True), 1e-30)
        p_sum = p_sum + p
        oct_e = _dot(vct_ref[0, 0], p.astype(BF16))
        oc_ref[e] = oct_e.T

    imp = _dot3_r(ovl_ref[...], p_sum)
    j_io = lax.broadcasted_iota(I32, (SLC_SLOTS, tq), 0)
    cur = (s0 + lax.broadcasted_iota(I32, (SLC_SLOTS, tq), 1)) // SLC_BLOCK
    forced = (j_io == 0) | (j_io == cur) | (j_io == cur - 1)
    valid = j_io <= cur
    v = jnp.where(forced, BIG, jnp.where(valid, imp, -BIG))
    sel = jnp.zeros((SLC_SLOTS, tq), F32)
    for _ in range(topk):
        mx = jnp.max(v, axis=0, keepdims=True)
        idx = jnp.min(jnp.where(v == mx, j_io, SLC_SLOTS), axis=0, keepdims=True)
        hit = j_io == idx
        sel = jnp.where(hit, 1.0, sel)
        v = jnp.where(hit, -3e38, v)
    selb = jnp.where((sel > 0.5) & valid, 0.0, NEG)
    selb_t = jnp.concatenate([jnp.zeros((SLC_SLOTS, tq), F32), selb], axis=0).T
    selb_t = selb_t.astype(BF16)
    for e in range(nh):
        qa_ref[e * tq:(e + 1) * tq, HEAD_DIM:LANES] = selb_t[:, HEAD_DIM:LANES]

    r_io = lax.broadcasted_iota(I32, (tq, tq), 0)
    c_io = lax.broadcasted_iota(I32, (tq, tq), 1)

    def reset():
        m_ref[...] = jnp.full((rows, 1), NEG, F32)
        l_ref[...] = jnp.zeros((rows, 1), F32)
        acc_ref[...] = jnp.zeros((rows, HEAD_DIM), F32)

    def flash_tile(k_tile, v_tile, width, mask):
        for e in range(nh):
            rs = slice(e * tq, (e + 1) * tq)
            s = _nt(qa_ref[rs, 0:width], k_tile)
            if mask is not None:
                s = jnp.where(mask, s, NEG)
            m_prev = m_ref[rs, :]
            m_new = jnp.maximum(m_prev, jnp.max(s, axis=1, keepdims=True))
            alpha = jnp.exp(m_prev - m_new)
            p = jnp.exp(s - m_new)
            l_ref[rs, :] = alpha * l_ref[rs, :] + jnp.sum(p, axis=1, keepdims=True)
            acc_ref[rs, :] = alpha * acc_ref[rs, :] + _dot(p.astype(BF16), v_tile)
            m_ref[rs, :] = m_new

    def sel_tile(j, mask):
        rows_j = pl.ds(pl.multiple_of(j * tq, tq), tq)
        flash_tile(ksa_ref[0, 0, rows_j, :], vs_ref[0, 0, rows_j, :], LANES, mask)

    def win_tile(j, mask):
        rows_j = pl.ds(pl.multiple_of(j * tq, tq), tq)
        flash_tile(kw_ref[0, 0, rows_j, :], vw_ref[0, 0, rows_j, :], HEAD_DIM, mask)

    reset()

    def sel_body(j, carry):
        sel_tile(j, None)
        return carry

    lax.fori_loop(0, i, sel_body, 0)
    sel_tile(i, c_io <= r_io)
    os_ref[...] = acc_ref[...] / l_ref[...]

    reset()

    @pl.when(i >= 2)
    def _():
        win_tile(i - 2, c_io > r_io)

    @pl.when(i >= 1)
    def _():
        win_tile(i - 1, None)

    win_tile(i, c_io <= r_io)
    ow = acc_ref[...] / l_ref[...]

    gates = jax.nn.sigmoid(sm_ref[...])
    outs = []
    for e in range(nh):
        rs = slice(e * tq, (e + 1) * tq)

        def gate(c):
            c0 = GATE_LANE0 + e * 3 + c
            c1 = c0 + nh * 3
            return jnp.where(g == 0, gates[:, c0:c0 + 1], gates[:, c1:c1 + 1])

        outs.append(gate(0) * oc_ref[e][:, 0:HEAD_DIM] + gate(1) * os_ref[rs, :] + gate(2) * ow[rs, :])
    o = jnp.concatenate(outs, axis=1)
    o_ref[...] = (o * _silu(ga_ref[...])).astype(BF16)


def _nsa(q, kc, vct, ksa, vs, kw, vw, ga, sm, ovl, *, batch, seq, ncp, topk):
    tq = NSA_TILE
    assert WINDOW == 2 * tq
    nq = seq // tq
    gw = HEADS_PER_GROUP * HEAD_DIM
    rows = HEADS_PER_GROUP * tq
    per_bg = lambda r, c: pl.BlockSpec((1, 1, r, c), lambda b, g, i: (b, g, 0, 0))
    return pl.pallas_call(
        functools.partial(_nsa_body, tq=tq, ncp=ncp, topk=topk),
        grid=(batch, KV_GROUPS, nq),
        in_specs=[pl.BlockSpec((tq, gw), lambda b, g, i: (b * nq + i, g)),
                  per_bg(ncp, HEAD_DIM), per_bg(LANES, ncp),
                  per_bg(seq, LANES), per_bg(seq, HEAD_DIM), per_bg(seq, HEAD_DIM),
                  per_bg(seq, HEAD_DIM),
                  pl.BlockSpec((tq, gw), lambda b, g, i: (b * nq + i, g)),
                  pl.BlockSpec((tq, LANES), lambda b, g, i: (b * nq + i, 0)),
                  pl.BlockSpec(ovl.shape, lambda b, g, i: (0, 0))],
        out_specs=pl.BlockSpec((tq, gw), lambda b, g, i: (b * nq + i, g)),
        out_shape=jax.ShapeDtypeStruct((batch * seq, NSA_WIDTH), BF16),
        scratch_shapes=[pltpu.VMEM((rows, LANES), BF16),
                        pltpu.VMEM((rows, 1), F32), pltpu.VMEM((rows, 1), F32),
                        pltpu.VMEM((rows, HEAD_DIM), F32),
                        pltpu.VMEM((HEADS_PER_GROUP, tq, LANES), F32),
                        pltpu.VMEM((rows, HEAD_DIM), F32)],
        compiler_params=pltpu.CompilerParams(
            dimension_semantics=("parallel", "parallel", "arbitrary"),
            vmem_limit_bytes=VMEM_LIMIT),
        name="nsa",
    )(q, kc, vct, ksa, vs, kw, vw, ga, sm, ovl)


def _ssd_body(xbc_ref, z_ref, sm_ref, dtt_ref, cw_ref, cb_ref, dtb_ref, dtbt_ref, al_ref, alt_ref,
              dsk_ref, gn_ref, eh_ref, ehw_ref, tril_ref, triu_ref,
              o_ref, ext_ref, xc_ref, y_ref, h_ref, *, ts):
    L = SSD_CHUNK
    N = SSD_STATE
    P = SSD_HEAD_DIM
    E = SSD_HEADS // SSD_GROUPS
    gw = E * P
    pad = 8

    @pl.when(pl.program_id(1) == 0)
    def _():
        ext_ref[0:pad, :] = jnp.zeros((pad, CONV_DIM), F32)
        h_ref[...] = jnp.zeros(h_ref.shape, F32)

    ext_ref[pad:pad + ts, :] = xbc_ref[...]
    acc = cb_ref[...] + ext_ref[pl.ds(pad - (SSD_CONV - 1), ts), :] * cw_ref[0:1, :]
    for k in range(1, SSD_CONV):
        acc = acc + ext_ref[pl.ds(pad - (SSD_CONV - 1) + k, ts), :] * cw_ref[k:k + 1, :]
    xc_ref[...] = _silu(acc)
    ext_ref[0:pad, :] = ext_ref[ts:ts + pad, :]

    a_row = -jnp.exp(al_ref[...])
    a_col = -jnp.exp(alt_ref[...])
    causal = lax.broadcasted_iota(I32, (L, L), 1) <= lax.broadcasted_iota(I32, (L, L), 0)

    for c in range(ts // L):
        rs = slice(c * L, (c + 1) * L)
        xs = xc_ref[rs, 0:SSD_WIDTH]
        dt = jax.nn.softplus(sm_ref[rs, :] + dtb_ref[...])
        a_cs = _dot3_r(tril_ref[...], dt * a_row)
        dt_x = _dot3_l(dt, eh_ref[...])
        acs_x = _dot3_l(a_cs, eh_ref[...])
        acs_w = _dot3_l(a_cs, ehw_ref[...])
        dtt = jax.nn.softplus(dtt_ref[0, :, rs] + dtbt_ref[...])
        acs_t = _dot3_l(dtt * a_col, triu_ref[...])
        a_last = acs_x[L - 1:L, :]
        xdt = xs * dt_x
        xdo = (xdt * jnp.exp(a_last - acs_x)).astype(BF16)
        xdt_b = xdt.astype(BF16)
        pre = jnp.exp(acs_x)
        cdec = jnp.exp(a_last)
        for gi in range(SSD_GROUPS):
            gs = slice(gi * gw, (gi + 1) * gw)
            bm = xc_ref[rs, SSD_WIDTH + gi * N:SSD_WIDTH + (gi + 1) * N]
            cm = xc_ref[rs, SSD_WIDTH + SSD_GROUPS * N + gi * N:SSD_WIDTH + SSD_GROUPS * N + (gi + 1) * N]
            cm_b = cm.astype(BF16)
            cbm = _nt(cm_b, bm.astype(BF16))
            h_prev = h_ref[gi]
            y_off = _dot(cm_b, h_prev.astype(BF16)) * pre[:, gs]
            h_ref[gi] = h_prev * cdec[:, gs] + _dot(bm.T.astype(BF16), xdo[:, gs])
            y_ref[rs, gs] = y_off + xs[:, gs] * dsk_ref[:, gs]
            for e in range(E):
                h = gi * E + e
                hs = slice(h * P, (h + 1) * P)
                d = acs_w[:, h * LANES:(h + 1) * LANES] - acs_t[h:h + 1, :]
                dec = jnp.exp(jnp.where(causal, d, NEG))
                y_ref[rs, hs] += _dot((cbm * dec).astype(BF16), xdt_b[:, hs])

    y = y_ref[...] * _silu(z_ref[...])
    ms = jnp.mean(y * y, axis=-1, keepdims=True)
    o_ref[...] = (y * lax.rsqrt(ms + EPS) * gn_ref[...]).astype(BF16)


def _ssd(xbc, z, sm, dtt, cw, cb, dtb, dtbt, al, alt, dsk, gn, eh, ehw, tril, triu, *, batch, seq):
    ts = SSD_TILE
    nt = seq // ts
    row = lambda n: pl.BlockSpec((ts, n), lambda b, c: (b * nt + c, 0))
    full = lambda a: pl.BlockSpec(a.shape, lambda b, c: (0,) * a.ndim)
    consts = [cw, cb, dtb, dtbt, al, alt, dsk, gn, eh, ehw, tril, triu]
    return pl.pallas_call(
        functools.partial(_ssd_body, ts=ts),
        grid=(batch, nt),
        in_specs=[row(CONV_DIM), row(SSD_WIDTH), row(LANES),
                  pl.BlockSpec((1, SSD_HEADS, ts), lambda b, c: (b, 0, c))]
                 + [full(a) for a in consts],
        out_specs=row(SSD_WIDTH),
        out_shape=jax.ShapeDtypeStruct((batch * seq, SSD_WIDTH), BF16),
        scratch_shapes=[pltpu.VMEM((ts + 8, CONV_DIM), F32), pltpu.VMEM((ts, CONV_DIM), F32),
                        pltpu.VMEM((ts, SSD_WIDTH), F32),
                        pltpu.VMEM((SSD_GROUPS, SSD_STATE, SSD_WIDTH // SSD_GROUPS), F32)],
        compiler_params=pltpu.CompilerParams(dimension_semantics=("parallel", "arbitrary"),
                                             vmem_limit_bytes=VMEM_LIMIT),
        name="ssd",
    )(xbc, z, sm, dtt, *consts)


def _memkv_body(mem_ref, g_ref, w_ref, k_ref, v_ref):
    x = mem_ref[0]
    ms = jnp.mean(x * x, axis=-1, keepdims=True)
    xn = (x * lax.rsqrt(ms + EPS) * g_ref[...]).astype(BF16)
    k_ref[0] = _dot(xn, w_ref[:, 0:XA_WIDTH]).astype(BF16)
    v_ref[0] = _dot(xn, w_ref[:, XA_WIDTH:2 * XA_WIDTH]).astype(BF16)


def _mem_kv(mem, g_mem, w_kv):
    batch, mlen, _ = mem.shape
    full = lambda a: pl.BlockSpec(a.shape, lambda b: (0,) * a.ndim)
    out = pl.BlockSpec((1, mlen, XA_WIDTH), lambda b: (b, 0, 0))
    shp = jax.ShapeDtypeStruct((batch, mlen, XA_WIDTH), BF16)
    return pl.pallas_call(
        _memkv_body,
        grid=(batch,),
        in_specs=[pl.BlockSpec((1, mlen, D_MODEL), lambda b: (b, 0, 0)), full(g_mem), full(w_kv)],
        out_specs=[out, out],
        out_shape=[shp, shp],
        compiler_params=pltpu.CompilerParams(dimension_semantics=("parallel",),
                                             vmem_limit_bytes=VMEM_LIMIT),
        name="mem_kv",
    )(mem, g_mem, w_kv)


def _outproj_body(x_ref, oa_ref, ob_ref, qx_ref, gx_ref, k_ref, v_ref, w_ref, g_ref, o_ref):
    acc = x_ref[...] + _dot(oa_ref[...], w_ref[0:NSA_WIDTH, :])
    acc = acc + _dot(ob_ref[...], w_ref[NSA_WIDTH:NSA_WIDTH + SSD_WIDTH, :])
    for h in range(XA_HEADS):
        hs = slice(h * XA_HEAD_DIM, (h + 1) * XA_HEAD_DIM)
        s = _nt(qx_ref[:, hs], k_ref[0, :, hs])
        s = s - jnp.max(s, axis=-1, keepdims=True)
        p = jnp.exp(s)
        p = p / jnp.sum(p, axis=-1, keepdims=True)
        oc = _dot(p.astype(BF16), v_ref[0, :, hs]) * _silu(gx_ref[:, hs])
        off = NSA_WIDTH + SSD_WIDTH + h * XA_HEAD_DIM
        acc = acc + _dot(oc.astype(BF16), w_ref[off:off + XA_HEAD_DIM, :])
    ms = jnp.mean(acc * acc, axis=-1, keepdims=True)
    o_ref[...] = acc * lax.rsqrt(ms + EPS) * g_ref[...]


def _out_proj(x2, oa, ob, qx, gx, km, vm, w_out, g_final, *, batch, seq):
    rows = batch * seq
    tm = ROW_TILE
    tps = seq // tm
    mlen = km.shape[1]
    row = lambda n: pl.BlockSpec((tm, n), lambda r: (r, 0))
    full = lambda a: pl.BlockSpec(a.shape, lambda r: (0,) * a.ndim)
    mem = pl.BlockSpec((1, mlen, XA_WIDTH), lambda r: (r // tps, 0, 0))
    return pl.pallas_call(
        _outproj_body,
        grid=(rows // tm,),
        in_specs=[row(D_MODEL), row(NSA_WIDTH), row(SSD_WIDTH), row(XA_WIDTH), row(XA_WIDTH),
                  mem, mem, full(w_out), full(g_final)],
        out_specs=row(D_MODEL),
        out_shape=jax.ShapeDtypeStruct((rows, D_MODEL), F32),
        compiler_params=pltpu.CompilerParams(dimension_semantics=("parallel",),
                                             vmem_limit_bytes=VMEM_LIMIT),
        name="out_proj",
    )(x2, oa, ob, qx, gx, km, vm, w_out, g_final)


def _permute_w_in(w):
    sizes = [NSA_WIDTH] + [KV_WIDTH] * 6 + [NSA_HEADS * 3, NSA_WIDTH, SSD_WIDTH, CONV_DIM, SSD_HEADS,
                                          XA_WIDTH, XA_WIDTH]
    offs = [0]
    for s in sizes:
        offs.append(offs[-1] + s)
    sec = lambda n: w[:, offs[n]:offs[n + 1]]
    small = jnp.concatenate(
        [sec(7), sec(11), jnp.zeros((w.shape[0], LANES - NSA_HEADS * 3 - SSD_HEADS), w.dtype)], axis=1)
    cols = [sec(0)] + [sec(n) for n in range(1, 7)] + [sec(8), sec(9), sec(10), sec(12), sec(13), small]
    return jnp.concatenate(cols, axis=1).astype(BF16)


def _lane_row(vals, lane0):
    return jnp.zeros((1, LANES), F32).at[0, lane0:lane0 + vals.shape[0]].set(vals.astype(F32))


def _forward(x, mem, positions, g_in, w_in, cmp_pos_k, w_cmp1_k, w_cmp2_k, cmp_pos_v, w_cmp1_v,
             w_cmp2_v, conv_w, conv_b, dt_bias, a_log, d_skip, g_ssd_norm, g_mem, w_mem_kv, w_out,
             g_final):
    batch, seq, _ = x.shape
    ncp = seq // CMP_STRIDE
    n_slc = seq // SLC_BLOCK
    assert n_slc <= SLC_SLOTS and seq % ROW_TILE == 0 and seq % SSD_TILE == 0
    topk = min(SLC_TOPK, n_slc)
    rows = batch * seq

    inv = ROPE_THETA ** (-jnp.arange(0, ROPE_DIM, 2, dtype=F32) / ROPE_DIM)
    head_inv = jnp.concatenate([inv, inv, jnp.zeros((HEAD_DIM - ROPE_DIM,), F32)])
    invl = jnp.tile(head_inv, LANES // HEAD_DIM)[None, :]
    head_sgn = jnp.concatenate([-jnp.ones((ROPE_HALF,), F32), jnp.ones((ROPE_HALF,), F32),
                                jnp.zeros((HEAD_DIM - ROPE_DIM,), F32)])
    sgn = jnp.tile(head_sgn, LANES // HEAD_DIM)[None, :]

    x2 = x.reshape(rows, D_MODEL)
    pos2 = positions.reshape(rows, 1)
    h = x2
    for l in range(g_in.shape[0]):
        (q, kc_raw, vc_raw, ksa, vs, kw, vw, ga, z, xbc, qx, gx, sm) = _in_proj(
            h, pos2, g_in[l][None, :], _permute_w_in(w_in[l]), invl, sgn, batch=batch, seq=seq)

        chunks = lambda a: a.reshape(batch, KV_GROUPS, ncp, CMP_STRIDE * HEAD_DIM)
        cmp_end = jnp.minimum(jnp.arange(ncp) * CMP_STRIDE + CMP_LEN - 1, seq - 1)
        posc = positions[:, cmp_end][:, :, None]
        pad_w2 = lambda w: jnp.pad(w, ((0, 0), (0, LANES - HEAD_DIM))).astype(BF16)
        pos_rows = lambda p: jnp.broadcast_to(p.reshape(1, CMP_LEN * HEAD_DIM), (8, CMP_LEN * HEAD_DIM)).astype(BF16)
        kc, vc = _compress(chunks(kc_raw), chunks(vc_raw), posc,
                           w_cmp1_k[l].astype(BF16), pad_w2(w_cmp2_k[l]), pos_rows(cmp_pos_k[l]),
                           w_cmp1_v[l].astype(BF16), pad_w2(w_cmp2_v[l]), pos_rows(cmp_pos_v[l]),
                           invl, sgn, batch=batch, ncp=ncp)
        vct = jnp.pad(jnp.swapaxes(vc, 2, 3), ((0, 0), (0, 0), (0, LANES - HEAD_DIM), (0, 0)))
        n_ix = jnp.arange(ncp)[None, :]
        j_ix = jnp.arange(SLC_SLOTS)[:, None]
        ovl = ((n_ix * CMP_STRIDE < j_ix * SLC_BLOCK + SLC_BLOCK)
               & (n_ix * CMP_STRIDE + CMP_LEN > j_ix * SLC_BLOCK)
               & (n_ix < ncp - (CMP_LEN // CMP_STRIDE - 1))).astype(BF16)
        o_a = _nsa(q, kc, vct, ksa, vs, kw, vw, ga, sm, ovl, batch=batch, seq=seq, ncp=ncp, topk=topk)

        dtt = jnp.swapaxes(sm[:, DT_LANE0:DT_LANE0 + SSD_HEADS].reshape(batch, seq, SSD_HEADS), 1, 2)
        head_of_lane = jnp.arange(SSD_WIDTH) // SSD_HEAD_DIM
        k_ix = jnp.arange(LANES)[:, None]
        eh = (k_ix == DT_LANE0 + head_of_lane[None, :]).astype(BF16)
        ehw = (k_ix == DT_LANE0 + (jnp.arange(SSD_HEADS * LANES) // LANES)[None, :]).astype(BF16)
        t_ix = jnp.arange(SSD_CHUNK)
        tril = (t_ix[None, :] <= t_ix[:, None]).astype(BF16)
        o_b = _ssd(xbc, z, sm, dtt, conv_w[l], conv_b[l][None, :],
                   _lane_row(dt_bias[l], DT_LANE0), dt_bias[l].astype(F32)[:, None],
                   _lane_row(a_log[l], DT_LANE0), a_log[l].astype(F32)[:, None],
                   jnp.repeat(d_skip[l].astype(F32), SSD_HEAD_DIM)[None, :], g_ssd_norm[l][None, :],
                   eh, ehw, tril, tril.T, batch=batch, seq=seq)

        km, vm = _mem_kv(mem, g_mem[l][None, :], w_mem_kv[l].astype(BF16))
        last = l == g_in.shape[0] - 1
        assert last, "multi-layer stacking is not needed for this problem (DEPTH == 1)"
        h = _out_proj(h, o_a, o_b, qx, gx, km, vm, w_out[l].astype(BF16), g_final[None, :],
                      batch=batch, seq=seq)
    return h.reshape(batch, seq, D_MODEL)


def kernel(x, mem, positions, g_in, w_in, cmp_pos_k, w_cmp1_k, w_cmp2_k, cmp_pos_v, w_cmp1_v, w_cmp2_v,
           conv_w, conv_b, dt_bias, a_log, d_skip, g_ssd_norm, g_mem, w_mem_kv, w_out, g_final):
    return _forward(x, mem, positions, g_in, w_in, cmp_pos_k, w_cmp1_k, w_cmp2_k, cmp_pos_v, w_cmp1_v,
                    w_cmp2_v, conv_w, conv_b, dt_bias, a_log, d_skip, g_ssd_norm, g_mem, w_mem_kv,
                    w_out, g_final)
```

```python
import functools

import jax
import jax.numpy as jnp
from jax import lax
from jax.experimental import pallas as pl
from jax.experimental.pallas import tpu as pltpu

F32 = jnp.float32
BF16 = jnp.bfloat16
I32 = jnp.int32

D_MODEL = 1024
NSA_HEADS = 8
HEAD_DIM = 64
KV_GROUPS = 2
HEADS_PER_GROUP = NSA_HEADS // KV_GROUPS
NSA_WIDTH = NSA_HEADS * HEAD_DIM
KV_WIDTH = KV_GROUPS * HEAD_DIM
CMP_LEN = 32
CMP_STRIDE = 16
CMP_HIDDEN = 256
SLC_BLOCK = 64
SLC_TOPK = 16
SLC_SLOTS = 64
WINDOW = 512
BIG = 1e9
NEG = -1e30

SSD_HEADS = 8
SSD_HEAD_DIM = 64
SSD_WIDTH = SSD_HEADS * SSD_HEAD_DIM
SSD_GROUPS = 2
SSD_STATE = 128
SSD_CONV = 4
SSD_CHUNK = 128
CONV_DIM = SSD_WIDTH + 2 * SSD_GROUPS * SSD_STATE

XA_HEADS = 4
XA_HEAD_DIM = 128
XA_WIDTH = XA_HEADS * XA_HEAD_DIM
MIX_WIDTH = NSA_WIDTH + SSD_WIDTH + XA_WIDTH

ROPE_THETA = 500000.0
ROPE_DIM = HEAD_DIM // 4
ROPE_HALF = ROPE_DIM // 2
EPS = 1e-6

LANES = 128
GATE_LANE0 = 0
DT_LANE0 = 24

OFF_Q = 0
OFF_KV = OFF_Q + NSA_WIDTH
OFF_GA = OFF_KV + 6 * KV_WIDTH
OFF_Z = OFF_GA + NSA_WIDTH
OFF_XBC = OFF_Z + SSD_WIDTH
OFF_QX = OFF_XBC + CONV_DIM
OFF_GX = OFF_QX + XA_WIDTH
OFF_SM = OFF_GX + XA_WIDTH
N_PROJ = OFF_SM + LANES

ROW_TILE = 512
NSA_TILE = 256
V_ROWS = 80
LOG2E = 1.4426950408889634
SSD_TILE = 512
VMEM_LIMIT = 56 * 1024 * 1024


def _nt(a, b):
    return lax.dot_general(a, b, (((1,), (1,)), ((), ())), preferred_element_type=F32)


def _dot(a, b):
    return jnp.dot(a, b, preferred_element_type=F32)


def _split3(x):
    hi = x.astype(BF16)
    r1 = x - hi.astype(F32)
    mid = r1.astype(BF16)
    lo = (r1 - mid.astype(F32)).astype(BF16)
    return hi, mid, lo


def _dot3_l(x, w):
    hi, mid, lo = _split3(x)
    return _dot(hi, w) + _dot(mid, w) + _dot(lo, w)


def _dot3_r(w, x):
    hi, mid, lo = _split3(x)
    return _dot(w, hi) + _dot(w, mid) + _dot(w, lo)


def _silu(x):
    return x * jax.nn.sigmoid(x)


def _rope(a, cs, sn, first):
    r = jnp.where(first, pltpu.roll(a, LANES - ROPE_HALF, 1), pltpu.roll(a, ROPE_HALF, 1))
    return a * cs + r * sn


def _rope_tables(pos_f32, invl, sgn):
    ang = pos_f32 * invl
    return jnp.cos(ang), jnp.sin(ang) * sgn


def _inproj_body(x_ref, pos_ref, g_ref, w_ref, invl_ref, sgn_ref,
                 q_ref, kc_ref, vc_ref, ksa_ref, vs_ref, kw_ref, vw_ref,
                 ga_ref, z_ref, xbc_ref, qx_ref, gx_ref, sm_ref, *, tm, tiles_per_seq):
    x = x_ref[...]
    ms = jnp.mean(x * x, axis=-1, keepdims=True)
    xn = (x * lax.rsqrt(ms + EPS) * g_ref[...]).astype(BF16)

    cs, sn = _rope_tables(pos_ref[...].astype(F32), invl_ref[...], sgn_ref[...])
    lane = lax.broadcasted_iota(I32, (tm, LANES), 1)
    first = (lane % HEAD_DIM) < ROPE_HALF

    def mm(lo, n):
        return _dot(xn, w_ref[:, lo:lo + n])

    qf = mm(OFF_Q, NSA_WIDTH)
    for c in range(NSA_WIDTH // LANES):
        sl = slice(c * LANES, (c + 1) * LANES)
        q_ref[:, sl] = (_rope(qf[:, sl], cs, sn, first) * (HEAD_DIM ** -0.5 * LOG2E)).astype(BF16)

    kv = mm(OFF_KV, 6 * KV_WIDTH)
    k_c, v_c, k_s, v_s, k_w, v_w = [kv[:, n * KV_WIDTH:(n + 1) * KV_WIDTH] for n in range(6)]
    k_s = _rope(k_s, cs, sn, first)
    k_w = _rope(k_w, cs, sn, first)
    s_base = (pl.program_id(0) % tiles_per_seq) * tm
    blk = (s_base + lax.broadcasted_iota(I32, (tm, LANES), 0)) // SLC_BLOCK
    onehot = jnp.where(lane - HEAD_DIM == blk, 1.0, 0.0)
    low = lane < HEAD_DIM
    for g in range(KV_GROUPS):
        gs = slice(g * HEAD_DIM, (g + 1) * HEAD_DIM)
        kc_ref[0, g] = k_c[:, gs].astype(BF16)
        vc_ref[0, g] = v_c[:, gs].astype(BF16)
        vs_ref[0, g] = v_s[:, gs].astype(BF16)
        kw_ref[0, g] = k_w[:, gs].astype(BF16)
        vw_ref[0, g] = v_w[:, gs].astype(BF16)
        ks_g = k_s if g == 0 else pltpu.roll(k_s, HEAD_DIM, 1)
        ksa_ref[0, g] = jnp.where(low, ks_g, onehot).astype(BF16)

    ga_ref[...] = mm(OFF_GA, NSA_WIDTH)
    z_ref[...] = mm(OFF_Z, SSD_WIDTH)
    xbc_ref[...] = mm(OFF_XBC, CONV_DIM)
    qx_ref[...] = (mm(OFF_QX, XA_WIDTH) * (XA_HEAD_DIM ** -0.5)).astype(BF16)
    gx_ref[...] = mm(OFF_GX, XA_WIDTH)
    sm_ref[...] = mm(OFF_SM, LANES)


def _in_proj(x2, pos2, g_in, w_p, invl, sgn, *, batch, seq):
    rows = batch * seq
    tm = ROW_TILE
    tps = seq // tm
    row = lambda n: pl.BlockSpec((tm, n), lambda r: (r, 0))
    full = lambda a: pl.BlockSpec(a.shape, lambda r: (0,) * a.ndim)
    grp = lambda n: pl.BlockSpec((1, KV_GROUPS, tm, n), lambda r: (r // tps, 0, r % tps, 0))
    grp_shape = lambda n: jax.ShapeDtypeStruct((batch, KV_GROUPS, seq, n), BF16)
    flat = lambda n, dt: jax.ShapeDtypeStruct((rows, n), dt)
    return pl.pallas_call(
        functools.partial(_inproj_body, tm=tm, tiles_per_seq=tps),
        grid=(rows // tm,),
        in_specs=[row(D_MODEL), row(1), full(g_in), full(w_p), full(invl), full(sgn)],
        out_specs=[row(NSA_WIDTH), grp(HEAD_DIM), grp(HEAD_DIM), grp(LANES), grp(HEAD_DIM),
                   grp(HEAD_DIM), grp(HEAD_DIM), row(NSA_WIDTH), row(SSD_WIDTH), row(CONV_DIM),
                   row(XA_WIDTH), row(XA_WIDTH), row(LANES)],
        out_shape=[flat(NSA_WIDTH, BF16), grp_shape(HEAD_DIM), grp_shape(HEAD_DIM),
                   grp_shape(LANES), grp_shape(HEAD_DIM), grp_shape(HEAD_DIM), grp_shape(HEAD_DIM),
                   flat(NSA_WIDTH, F32), flat(SSD_WIDTH, F32), flat(CONV_DIM, F32),
                   flat(XA_WIDTH, BF16), flat(XA_WIDTH, F32), flat(LANES, F32)],
        compiler_params=pltpu.CompilerParams(dimension_semantics=("parallel",),
                                             vmem_limit_bytes=VMEM_LIMIT),
        name="in_proj",
    )(x2, pos2, g_in, w_p, invl, sgn)


def _compress_body(ck_ref, cv_ref, posc_ref, w1k_ref, w2k_ref, pk_ref, w1v_ref, w2v_ref, pv_ref,
                   invl_ref, sgn_ref, kc_ref, vc_ref, *, ncp):
    half = CMP_STRIDE * HEAD_DIM

    def mlp(c_ref, w1_ref, w2_ref, p_ref):
        c = c_ref[0, 0]
        a = _dot(c, w1_ref[0:half, :])
        b = _dot(c, w1_ref[half:2 * half, :])
        bias = _dot(p_ref[...], w1_ref[...])[0:1, :]
        h = a + pltpu.roll(b, ncp - 1, 0) + bias
        return _dot(_silu(h).astype(BF16), w2_ref[...])

    kc = mlp(ck_ref, w1k_ref, w2k_ref, pk_ref)
    vc = mlp(cv_ref, w1v_ref, w2v_ref, pv_ref)
    cs, sn = _rope_tables(posc_ref[0].astype(F32), invl_ref[...], sgn_ref[...])
    lane = lax.broadcasted_iota(I32, (ncp, LANES), 1)
    kc = _rope(kc, cs, sn, (lane % HEAD_DIM) < ROPE_HALF)
    kc_ref[0, 0] = kc[:, 0:HEAD_DIM].astype(BF16)
    vc_ref[0, 0] = vc[:, 0:HEAD_DIM].astype(BF16)


def _compress(ck, cv, posc, w1k, w2k, pk, w1v, w2v, pv, invl, sgn, *, batch, ncp):
    chunk = pl.BlockSpec((1, 1, ncp, CMP_STRIDE * HEAD_DIM), lambda b, g: (b, g, 0, 0))
    full = lambda a: pl.BlockSpec(a.shape, lambda b, g: (0,) * a.ndim)
    out = pl.BlockSpec((1, 1, ncp, HEAD_DIM), lambda b, g: (b, g, 0, 0))
    shp = jax.ShapeDtypeStruct((batch, KV_GROUPS, ncp, HEAD_DIM), BF16)
    return pl.pallas_call(
        functools.partial(_compress_body, ncp=ncp),
        grid=(batch, KV_GROUPS),
        in_specs=[chunk, chunk, pl.BlockSpec((1, ncp, 1), lambda b, g: (b, 0, 0)),
                  full(w1k), full(w2k), full(pk), full(w1v), full(w2v), full(pv),
                  full(invl), full(sgn)],
        out_specs=[out, out],
        out_shape=[shp, shp],
        compiler_params=pltpu.CompilerParams(dimension_semantics=("parallel", "parallel"),
                                             vmem_limit_bytes=VMEM_LIMIT),
        name="compress",
    )(ck, cv, posc, w1k, w2k, pk, w1v, w2v, pv, invl, sgn)


def _nsa_body(q_ref, kc_ref, vct_ref, ksa_ref, vst_ref, kw_ref, vwt_ref, ga_ref, sm_ref, ovl_ref,
              o_ref, qa_ref, m_ref, acc_ref, oc_ref, sa_ref, sb_ref, *, tq, ncp, topk):
    g = pl.program_id(1)
    i = pl.program_id(2)
    s0 = i * tq
    nh = HEADS_PER_GROUP
    rows = nh * tq

    q4 = q_ref[...]
    kc = kc_ref[0, 0]
    n_io = lax.broadcasted_iota(I32, (ncp, tq), 0)
    t_io = s0 + lax.broadcasted_iota(I32, (ncp, tq), 1)
    cmask = (n_io * CMP_STRIDE + (CMP_LEN - 1)) <= t_io
    cmask_f = jnp.where(cmask, 1.0, 0.0)
    p_sum = jnp.zeros((ncp, tq), F32)
    for e in range(nh):
        qe = q4[:, e * HEAD_DIM:(e + 1) * HEAD_DIM]
        qa_ref[e * tq:(e + 1) * tq, 0:HEAD_DIM] = qe
        st = jnp.where(cmask, _nt(kc, qe), NEG)
        st = st - jnp.max(st, axis=0, keepdims=True)
        p = jnp.exp2(st) * cmask_f
        p = p / jnp.maximum(jnp.sum(p, axis=0, keepdims=True), 1e-30)
        p_sum = p_sum + p
        oc_ref[e] = _dot(vct_ref[0, 0], p.astype(BF16))

    imp = _dot3_r(ovl_ref[...], p_sum)
    j_io = lax.broadcasted_iota(I32, (SLC_SLOTS, tq), 0)
    cur = (s0 + lax.broadcasted_iota(I32, (SLC_SLOTS, tq), 1)) // SLC_BLOCK
    forced = (j_io == 0) | (j_io == cur) | (j_io == cur - 1)
    valid = j_io <= cur
    v = jnp.where(forced, BIG, jnp.where(valid, imp, -BIG))
    sel = jnp.zeros((SLC_SLOTS, tq), F32)
    for _ in range(topk):
        mx = jnp.max(v, axis=0, keepdims=True)
        idx = jnp.min(jnp.where(v == mx, j_io, SLC_SLOTS), axis=0, keepdims=True)
        hit = j_io == idx
        sel = jnp.where(hit, 1.0, sel)
        v = jnp.where(hit, -3e38, v)
    selb = jnp.where((sel > 0.5) & valid, 0.0, NEG)
    selb_t = jnp.concatenate([jnp.zeros((SLC_SLOTS, tq), F32), selb], axis=0).T
    selb_t = selb_t.astype(BF16)
    for e in range(nh):
        qa_ref[e * tq:(e + 1) * tq, HEAD_DIM:LANES] = selb_t[:, HEAD_DIM:LANES]

    k_io = lax.broadcasted_iota(I32, (tq, rows), 0)
    t_io2 = lax.broadcasted_iota(I32, (tq, rows), 1) % tq

    def reset(br):
        m_ref[br] = jnp.full((1, rows), NEG, F32)
        acc_ref[br] = jnp.zeros((V_ROWS, rows), F32)

    def col_max(s):
        n = s.shape[0]
        while n > 8:
            n //= 2
            s = jnp.maximum(s[0:n], s[n:2 * n])
        return jnp.max(s, axis=0, keepdims=True)

    def consume(br, s, vt_tile, mask):
        if mask is not None:
            s = jnp.where(mask, s, NEG)
        m_prev = m_ref[br]
        m_new = jnp.maximum(m_prev, col_max(s))
        alpha = jnp.exp2(m_prev - m_new)
        p = jnp.exp2(s - m_new).astype(BF16)
        acc_ref[br] = alpha * acc_ref[br] + _dot(vt_tile, p)
        m_ref[br] = m_new

    def sel_scores(j):
        rows_j = pl.ds(pl.multiple_of(j * tq, tq), tq)
        return _nt(ksa_ref[0, 0, rows_j, :], qa_ref[...])

    def win_scores(j):
        rows_j = pl.ds(pl.multiple_of(j * tq, tq), tq)
        return _nt(kw_ref[0, 0, rows_j, :], qa_ref[:, 0:HEAD_DIM])

    diag = k_io <= t_io2

    reset(0)
    reset(1)
    sa_ref[...] = sel_scores(0)

    def sel_pair(jj, carry):
        j = 2 * jj
        sb_ref[...] = sel_scores(j + 1)
        consume(0, sa_ref[...], vst_ref[0, 0, j], None)
        sa_ref[...] = sel_scores(j + 2)
        consume(0, sb_ref[...], vst_ref[0, 0, j + 1], None)
        return carry

    lax.fori_loop(0, i // 2, sel_pair, 0)

    @pl.when(i % 2 == 0)
    def _():
        sb_ref[...] = win_scores(i)
        consume(0, sa_ref[...], vst_ref[0, 0, i], diag)
        consume(1, sb_ref[...], vwt_ref[0, 0, i], diag)

    @pl.when(i % 2 == 1)
    def _():
        sb_ref[...] = sel_scores(i)
        consume(0, sa_ref[...], vst_ref[0, 0, i - 1], None)
        sa_ref[...] = win_scores(i)
        consume(0, sb_ref[...], vst_ref[0, 0, i], diag)
        consume(1, sa_ref[...], vwt_ref[0, 0, i], diag)

    @pl.when(i >= 1)
    def _():
        consume(1, win_scores(i - 1), vwt_ref[0, 0, i - 1], None)

    @pl.when(i >= 2)
    def _():
        consume(1, win_scores(i - 2), vwt_ref[0, 0, i - 2], k_io > t_io2)

    gates_t = jax.nn.sigmoid(sm_ref[...]).T
    ga = ga_ref[...]

    def head_out(e):
        cs = slice(e * tq, (e + 1) * tq)

        def gate(c):
            c0 = GATE_LANE0 + e * 3 + c
            c1 = c0 + nh * 3
            return jnp.where(g == 0, gates_t[c0:c0 + 1, :], gates_t[c1:c1 + 1, :])

        o_s = acc_ref[0, 0:HEAD_DIM, cs] / acc_ref[0, HEAD_DIM:HEAD_DIM + 1, cs]
        o_w = acc_ref[1, 0:HEAD_DIM, cs] / acc_ref[1, HEAD_DIM:HEAD_DIM + 1, cs]
        return gate(0) * oc_ref[e, 0:HEAD_DIM, :] + gate(1) * o_s + gate(2) * o_w

    for pr in range(nh // 2):
        ls = slice(pr * LANES, (pr + 1) * LANES)
        pair = jnp.concatenate([head_out(2 * pr), head_out(2 * pr + 1)], axis=0).T
        o_ref[:, ls] = (pair * _silu(ga[:, ls])).astype(BF16)


def _nsa(q, kc, vct, ksa, vst, kw, vwt, ga, sm, ovl, *, batch, seq, ncp, topk):
    tq = NSA_TILE
    assert WINDOW == 2 * tq
    nq = seq // tq
    gw = HEADS_PER_GROUP * HEAD_DIM
    rows = HEADS_PER_GROUP * tq
    per_bg = lambda *shape: pl.BlockSpec((1, 1) + shape, lambda b, g, i: (b, g) + (0,) * len(shape))
    return pl.pallas_call(
        functools.partial(_nsa_body, tq=tq, ncp=ncp, topk=topk),
        grid=(batch, KV_GROUPS, nq),
        in_specs=[pl.BlockSpec((tq, gw), lambda b, g, i: (b * nq + i, g)),
                  per_bg(ncp, HEAD_DIM), per_bg(LANES, ncp),
                  per_bg(seq, LANES), per_bg(nq, V_ROWS, tq), per_bg(seq, HEAD_DIM),
                  per_bg(nq, V_ROWS, tq),
                  pl.BlockSpec((tq, gw), lambda b, g, i: (b * nq + i, g)),
                  pl.BlockSpec((tq, LANES), lambda b, g, i: (b * nq + i, 0)),
                  pl.BlockSpec(ovl.shape, lambda b, g, i: (0, 0))],
        out_specs=pl.BlockSpec((tq, gw), lambda b, g, i: (b * nq + i, g)),
        out_shape=jax.ShapeDtypeStruct((batch * seq, NSA_WIDTH), BF16),
        scratch_shapes=[pltpu.VMEM((rows, LANES), BF16),
                        pltpu.VMEM((2, 1, rows), F32),
                        pltpu.VMEM((2, V_ROWS, rows), F32),
                        pltpu.VMEM((HEADS_PER_GROUP, LANES, tq), F32),
                        pltpu.VMEM((tq, rows), F32), pltpu.VMEM((tq, rows), F32)],
        compiler_params=pltpu.CompilerParams(
            dimension_semantics=("parallel", "parallel", "arbitrary"),
            vmem_limit_bytes=VMEM_LIMIT),
        name="nsa",
    )(q, kc, vct, ksa, vst, kw, vwt, ga, sm, ovl)


def _ssd_body(xbc_ref, z_ref, sm_ref, dtt_ref, cw_ref, cb_ref, dtb_ref, dtbt_ref, al_ref, alt_ref,
              dsk_ref, gn_ref, eh_ref, ehw_ref, tril_ref, triu_ref,
              o_ref, ext_ref, xc_ref, y_ref, h_ref, *, ts):
    L = SSD_CHUNK
    N = SSD_STATE
    P = SSD_HEAD_DIM
    E = SSD_HEADS // SSD_GROUPS
    gw = E * P
    pad = 8

    @pl.when(pl.program_id(1) == 0)
    def _():
        ext_ref[0:pad, :] = jnp.zeros((pad, CONV_DIM), F32)
        h_ref[...] = jnp.zeros(h_ref.shape, F32)

    ext_ref[pad:pad + ts, :] = xbc_ref[...]
    acc = cb_ref[...] + ext_ref[pl.ds(pad - (SSD_CONV - 1), ts), :] * cw_ref[0:1, :]
    for k in range(1, SSD_CONV):
        acc = acc + ext_ref[pl.ds(pad - (SSD_CONV - 1) + k, ts), :] * cw_ref[k:k + 1, :]
    xc_ref[...] = _silu(acc)
    ext_ref[0:pad, :] = ext_ref[ts:ts + pad, :]

    a_row = -jnp.exp(al_ref[...])
    a_col = -jnp.exp(alt_ref[...])
    causal = lax.broadcasted_iota(I32, (L, L), 1) <= lax.broadcasted_iota(I32, (L, L), 0)

    for c in range(ts // L):
        rs = slice(c * L, (c + 1) * L)
        xs = xc_ref[rs, 0:SSD_WIDTH]
        dt = jax.nn.softplus(sm_ref[rs, :] + dtb_ref[...])
        a_cs = _dot3_r(tril_ref[...], dt * a_row)
        dt_x = _dot3_l(dt, eh_ref[...])
        acs_x = _dot3_l(a_cs, eh_ref[...])
        acs_w = _dot3_l(a_cs, ehw_ref[...])
        dtt = jax.nn.softplus(dtt_ref[0, :, rs] + dtbt_ref[...])
        acs_t = _dot3_l(dtt * a_col, triu_ref[...])
        a_last = acs_x[L - 1:L, :]
        xdt = xs * dt_x
        xdo = (xdt * jnp.exp(a_last - acs_x)).astype(BF16)
        xdt_b = xdt.astype(BF16)
        pre = jnp.exp(acs_x)
        cdec = jnp.exp(a_last)
        for gi in range(SSD_GROUPS):
            gs = slice(gi * gw, (gi + 1) * gw)
            bm = xc_ref[rs, SSD_WIDTH + gi * N:SSD_WIDTH + (gi + 1) * N]
            cm = xc_ref[rs, SSD_WIDTH + SSD_GROUPS * N + gi * N:SSD_WIDTH + SSD_GROUPS * N + (gi + 1) * N]
            cm_b = cm.astype(BF16)
            cbm = _nt(cm_b, bm.astype(BF16))
            h_prev = h_ref[gi]
            y_off = _dot(cm_b, h_prev.astype(BF16)) * pre[:, gs]
            h_ref[gi] = h_prev * cdec[:, gs] + _dot(bm.T.astype(BF16), xdo[:, gs])
            y_ref[rs, gs] = y_off + xs[:, gs] * dsk_ref[:, gs]
            for e in range(E):
                h = gi * E + e
                hs = slice(h * P, (h + 1) * P)
                d = acs_w[:, h * LANES:(h + 1) * LANES] - acs_t[h:h + 1, :]
                dec = jnp.exp(jnp.where(causal, d, NEG))
                y_ref[rs, hs] += _dot((cbm * dec).astype(BF16), xdt_b[:, hs])

    y = y_ref[...] * _silu(z_ref[...])
    ms = jnp.mean(y * y, axis=-1, keepdims=True)
    o_ref[...] = (y * lax.rsqrt(ms + EPS) * gn_ref[...]).astype(BF16)


def _ssd(xbc, z, sm, dtt, cw, cb, dtb, dtbt, al, alt, dsk, gn, eh, ehw, tril, triu, *, batch, seq):
    ts = SSD_TILE
    nt = seq // ts
    row = lambda n: pl.BlockSpec((ts, n), lambda b, c: (b * nt + c, 0))
    full = lambda a: pl.BlockSpec(a.shape, lambda b, c: (0,) * a.ndim)
    consts = [cw, cb, dtb, dtbt, al, alt, dsk, gn, eh, ehw, tril, triu]
    return pl.pallas_call(
        functools.partial(_ssd_body, ts=ts),
        grid=(batch, nt),
        in_specs=[row(CONV_DIM), row(SSD_WIDTH), row(LANES),
                  pl.BlockSpec((1, SSD_HEADS, ts), lambda b, c: (b, 0, c))]
                 + [full(a) for a in consts],
        out_specs=row(SSD_WIDTH),
        out_shape=jax.ShapeDtypeStruct((batch * seq, SSD_WIDTH), BF16),
        scratch_shapes=[pltpu.VMEM((ts + 8, CONV_DIM), F32), pltpu.VMEM((ts, CONV_DIM), F32),
                        pltpu.VMEM((ts, SSD_WIDTH), F32),
                        pltpu.VMEM((SSD_GROUPS, SSD_STATE, SSD_WIDTH // SSD_GROUPS), F32)],
        compiler_params=pltpu.CompilerParams(dimension_semantics=("parallel", "arbitrary"),
                                             vmem_limit_bytes=VMEM_LIMIT),
        name="ssd",
    )(xbc, z, sm, dtt, *consts)


def _memkv_body(mem_ref, g_ref, w_ref, k_ref, v_ref):
    x = mem_ref[0]
    ms = jnp.mean(x * x, axis=-1, keepdims=True)
    xn = (x * lax.rsqrt(ms + EPS) * g_ref[...]).astype(BF16)
    k_ref[0] = _dot(xn, w_ref[:, 0:XA_WIDTH]).astype(BF16)
    v_ref[0] = _dot(xn, w_ref[:, XA_WIDTH:2 * XA_WIDTH]).astype(BF16)


def _mem_kv(mem, g_mem, w_kv):
    batch, mlen, _ = mem.shape
    full = lambda a: pl.BlockSpec(a.shape, lambda b: (0,) * a.ndim)
    out = pl.BlockSpec((1, mlen, XA_WIDTH), lambda b: (b, 0, 0))
    shp = jax.ShapeDtypeStruct((batch, mlen, XA_WIDTH), BF16)
    return pl.pallas_call(
        _memkv_body,
        grid=(batch,),
        in_specs=[pl.BlockSpec((1, mlen, D_MODEL), lambda b: (b, 0, 0)), full(g_mem), full(w_kv)],
        out_specs=[out, out],
        out_shape=[shp, shp],
        compiler_params=pltpu.CompilerParams(dimension_semantics=("parallel",),
                                             vmem_limit_bytes=VMEM_LIMIT),
        name="mem_kv",
    )(mem, g_mem, w_kv)


def _outproj_body(x_ref, oa_ref, ob_ref, qx_ref, gx_ref, k_ref, v_ref, w_ref, g_ref, o_ref):
    acc = x_ref[...] + _dot(oa_ref[...], w_ref[0:NSA_WIDTH, :])
    acc = acc + _dot(ob_ref[...], w_ref[NSA_WIDTH:NSA_WIDTH + SSD_WIDTH, :])
    for h in range(XA_HEADS):
        hs = slice(h * XA_HEAD_DIM, (h + 1) * XA_HEAD_DIM)
        s = _nt(qx_ref[:, hs], k_ref[0, :, hs])
        s = s - jnp.max(s, axis=-1, keepdims=True)
        p = jnp.exp(s)
        p = p / jnp.sum(p, axis=-1, keepdims=True)
        oc = _dot(p.astype(BF16), v_ref[0, :, hs]) * _silu(gx_ref[:, hs])
        off = NSA_WIDTH + SSD_WIDTH + h * XA_HEAD_DIM
        acc = acc + _dot(oc.astype(BF16), w_ref[off:off + XA_HEAD_DIM, :])
    ms = jnp.mean(acc * acc, axis=-1, keepdims=True)
    o_ref[...] = acc * lax.rsqrt(ms + EPS) * g_ref[...]


def _out_proj(x2, oa, ob, qx, gx, km, vm, w_out, g_final, *, batch, seq):
    rows = batch * seq
    tm = ROW_TILE
    tps = seq // tm
    mlen = km.shape[1]
    row = lambda n: pl.BlockSpec((tm, n), lambda r: (r, 0))
    full = lambda a: pl.BlockSpec(a.shape, lambda r: (0,) * a.ndim)
    mem = pl.BlockSpec((1, mlen, XA_WIDTH), lambda r: (r // tps, 0, 0))
    return pl.pallas_call(
        _outproj_body,
        grid=(rows // tm,),
        in_specs=[row(D_MODEL), row(NSA_WIDTH), row(SSD_WIDTH), row(XA_WIDTH), row(XA_WIDTH),
                  mem, mem, full(w_out), full(g_final)],
        out_specs=row(D_MODEL),
        out_shape=jax.ShapeDtypeStruct((rows, D_MODEL), F32),
        compiler_params=pltpu.CompilerParams(dimension_semantics=("parallel",),
                                             vmem_limit_bytes=VMEM_LIMIT),
        name="out_proj",
    )(x2, oa, ob, qx, gx, km, vm, w_out, g_final)


def _permute_w_in(w):
    sizes = [NSA_WIDTH] + [KV_WIDTH] * 6 + [NSA_HEADS * 3, NSA_WIDTH, SSD_WIDTH, CONV_DIM, SSD_HEADS,
                                          XA_WIDTH, XA_WIDTH]
    offs = [0]
    for s in sizes:
        offs.append(offs[-1] + s)
    sec = lambda n: w[:, offs[n]:offs[n + 1]]
    small = jnp.concatenate(
        [sec(7), sec(11), jnp.zeros((w.shape[0], LANES - NSA_HEADS * 3 - SSD_HEADS), w.dtype)], axis=1)
    cols = [sec(0)] + [sec(n) for n in range(1, 7)] + [sec(8), sec(9), sec(10), sec(12), sec(13), small]
    return jnp.concatenate(cols, axis=1).astype(BF16)


def _value_tiles(v):
    b, g, s, d = v.shape
    vt = jnp.swapaxes(v.reshape(b, g, s // NSA_TILE, NSA_TILE, d), 3, 4)
    ones = jnp.ones((b, g, s // NSA_TILE, 1, NSA_TILE), v.dtype)
    zeros = jnp.zeros((b, g, s // NSA_TILE, V_ROWS - d - 1, NSA_TILE), v.dtype)
    return jnp.concatenate([vt, ones, zeros], axis=3)


def _lane_row(vals, lane0):
    return jnp.zeros((1, LANES), F32).at[0, lane0:lane0 + vals.shape[0]].set(vals.astype(F32))


def _forward(x, mem, positions, g_in, w_in, cmp_pos_k, w_cmp1_k, w_cmp2_k, cmp_pos_v, w_cmp1_v,
             w_cmp2_v, conv_w, conv_b, dt_bias, a_log, d_skip, g_ssd_norm, g_mem, w_mem_kv, w_out,
             g_final):
    batch, seq, _ = x.shape
    ncp = seq // CMP_STRIDE
    n_slc = seq // SLC_BLOCK
    assert n_slc <= SLC_SLOTS and seq % ROW_TILE == 0 and seq % SSD_TILE == 0
    topk = min(SLC_TOPK, n_slc)
    rows = batch * seq

    inv = ROPE_THETA ** (-jnp.arange(0, ROPE_DIM, 2, dtype=F32) / ROPE_DIM)
    head_inv = jnp.concatenate([inv, inv, jnp.zeros((HEAD_DIM - ROPE_DIM,), F32)])
    invl = jnp.tile(head_inv, LANES // HEAD_DIM)[None, :]
    head_sgn = jnp.concatenate([-jnp.ones((ROPE_HALF,), F32), jnp.ones((ROPE_HALF,), F32),
                                jnp.zeros((HEAD_DIM - ROPE_DIM,), F32)])
    sgn = jnp.tile(head_sgn, LANES // HEAD_DIM)[None, :]

    x2 = x.reshape(rows, D_MODEL)
    pos2 = positions.reshape(rows, 1)
    h = x2
    for l in range(g_in.shape[0]):
        (q, kc_raw, vc_raw, ksa, vs, kw, vw, ga, z, xbc, qx, gx, sm) = _in_proj(
            h, pos2, g_in[l][None, :], _permute_w_in(w_in[l]), invl, sgn, batch=batch, seq=seq)

        chunks = lambda a: a.reshape(batch, KV_GROUPS, ncp, CMP_STRIDE * HEAD_DIM)
        cmp_end = jnp.minimum(jnp.arange(ncp) * CMP_STRIDE + CMP_LEN - 1, seq - 1)
        posc = positions[:, cmp_end][:, :, None]
        pad_w2 = lambda w: jnp.pad(w, ((0, 0), (0, LANES - HEAD_DIM))).astype(BF16)
        pos_rows = lambda p: jnp.broadcast_to(p.reshape(1, CMP_LEN * HEAD_DIM), (8, CMP_LEN * HEAD_DIM)).astype(BF16)
        kc, vc = _compress(chunks(kc_raw), chunks(vc_raw), posc,
                           w_cmp1_k[l].astype(BF16), pad_w2(w_cmp2_k[l]), pos_rows(cmp_pos_k[l]),
                           w_cmp1_v[l].astype(BF16), pad_w2(w_cmp2_v[l]), pos_rows(cmp_pos_v[l]),
                           invl, sgn, batch=batch, ncp=ncp)
        vct = jnp.pad(jnp.swapaxes(vc, 2, 3), ((0, 0), (0, 0), (0, LANES - HEAD_DIM), (0, 0)))
        n_ix = jnp.arange(ncp)[None, :]
        j_ix = jnp.arange(SLC_SLOTS)[:, None]
        ovl = ((n_ix * CMP_STRIDE < j_ix * SLC_BLOCK + SLC_BLOCK)
               & (n_ix * CMP_STRIDE + CMP_LEN > j_ix * SLC_BLOCK)
               & (n_ix < ncp - (CMP_LEN // CMP_STRIDE - 1))).astype(BF16)
        o_a = _nsa(q, kc, vct, ksa, _value_tiles(vs), kw, _value_tiles(vw), ga, sm, ovl,
                   batch=batch, seq=seq, ncp=ncp, topk=topk)

        dtt = jnp.swapaxes(sm[:, DT_LANE0:DT_LANE0 + SSD_HEADS].reshape(batch, seq, SSD_HEADS), 1, 2)
        head_of_lane = jnp.arange(SSD_WIDTH) // SSD_HEAD_DIM
        k_ix = jnp.arange(LANES)[:, None]
        eh = (k_ix == DT_LANE0 + head_of_lane[None, :]).astype(BF16)
        ehw = (k_ix == DT_LANE0 + (jnp.arange(SSD_HEADS * LANES) // LANES)[None, :]).astype(BF16)
        t_ix = jnp.arange(SSD_CHUNK)
        tril = (t_ix[None, :] <= t_ix[:, None]).astype(BF16)
        o_b = _ssd(xbc, z, sm, dtt, conv_w[l], conv_b[l][None, :],
                   _lane_row(dt_bias[l], DT_LANE0), dt_bias[l].astype(F32)[:, None],
                   _lane_row(a_log[l], DT_LANE0), a_log[l].astype(F32)[:, None],
                   jnp.repeat(d_skip[l].astype(F32), SSD_HEAD_DIM)[None, :], g_ssd_norm[l][None, :],
                   eh, ehw, tril, tril.T, batch=batch, seq=seq)

        km, vm = _mem_kv(mem, g_mem[l][None, :], w_mem_kv[l].astype(BF16))
        last = l == g_in.shape[0] - 1
        assert last, "multi-layer stacking is not needed for this problem (DEPTH == 1)"
        h = _out_proj(h, o_a, o_b, qx, gx, km, vm, w_out[l].astype(BF16), g_final[None, :],
                      batch=batch, seq=seq)
    return h.reshape(batch, seq, D_MODEL)


def kernel(x, mem, positions, g_in, w_in, cmp_pos_k, w_cmp1_k, w_cmp2_k, cmp_pos_v, w_cmp1_v, w_cmp2_v,
           conv_w, conv_b, dt_bias, a_log, d_skip, g_ssd_norm, g_mem, w_mem_kv, w_out, g_final):
    return _forward(x, mem, positions, g_in, w_in, cmp_pos_k, w_cmp1_k, w_cmp2_k, cmp_pos_v, w_cmp1_v,
                    w_cmp2_v, conv_w, conv_b, dt_bias, a_log, d_skip, g_ssd_norm, g_mem, w_mem_kv,
                    w_out, g_final)
```

```python
import functools

import jax
import jax.numpy as jnp
from jax import lax
from jax.experimental import pallas as pl
from jax.experimental.pallas import tpu as pltpu

F32 = jnp.float32
BF16 = jnp.bfloat16
I32 = jnp.int32

D_MODEL = 1024
NSA_HEADS = 8
HEAD_DIM = 64
KV_GROUPS = 2
HEADS_PER_GROUP = NSA_HEADS // KV_GROUPS
NSA_WIDTH = NSA_HEADS * HEAD_DIM
KV_WIDTH = KV_GROUPS * HEAD_DIM
CMP_LEN = 32
CMP_STRIDE = 16
CMP_HIDDEN = 256
SLC_BLOCK = 64
SLC_TOPK = 16
SLC_SLOTS = 64
WINDOW = 512
BIG = 1e9
NEG = -1e30

SSD_HEADS = 8
SSD_HEAD_DIM = 64
SSD_WIDTH = SSD_HEADS * SSD_HEAD_DIM
SSD_GROUPS = 2
SSD_STATE = 128
SSD_CONV = 4
SSD_CHUNK = 128
CONV_DIM = SSD_WIDTH + 2 * SSD_GROUPS * SSD_STATE

XA_HEADS = 4
XA_HEAD_DIM = 128
XA_WIDTH = XA_HEADS * XA_HEAD_DIM
MIX_WIDTH = NSA_WIDTH + SSD_WIDTH + XA_WIDTH

ROPE_THETA = 500000.0
ROPE_DIM = HEAD_DIM // 4
ROPE_HALF = ROPE_DIM // 2
EPS = 1e-6

LANES = 128
GATE_LANE0 = 0
DT_LANE0 = 24

OFF_Q = 0
OFF_KV = OFF_Q + NSA_WIDTH
OFF_GA = OFF_KV + 6 * KV_WIDTH
OFF_Z = OFF_GA + NSA_WIDTH
OFF_XBC = OFF_Z + SSD_WIDTH
OFF_QX = OFF_XBC + CONV_DIM
OFF_GX = OFF_QX + XA_WIDTH
OFF_SM = OFF_GX + XA_WIDTH
N_PROJ = OFF_SM + LANES

ROW_TILE = 512
NSA_TILE = 256
V_ROWS = 80
LOG2E = 1.4426950408889634
CAT_OVL_ROW0 = 80
CAT_ROWS = 144
SSD_TILE = 512
VMEM_LIMIT = 56 * 1024 * 1024


def _nt(a, b):
    return lax.dot_general(a, b, (((1,), (1,)), ((), ())), preferred_element_type=F32)


def _dot(a, b):
    return jnp.dot(a, b, preferred_element_type=F32)


def _split3(x):
    hi = x.astype(BF16)
    r1 = x - hi.astype(F32)
    mid = r1.astype(BF16)
    lo = (r1 - mid.astype(F32)).astype(BF16)
    return hi, mid, lo


def _dot3_l(x, w):
    hi, mid, lo = _split3(x)
    return _dot(hi, w) + _dot(mid, w) + _dot(lo, w)


def _dot3_r(w, x):
    hi, mid, lo = _split3(x)
    return _dot(w, hi) + _dot(w, mid) + _dot(w, lo)


def _silu(x):
    h = 0.5 * x
    return h + h * jnp.tanh(h)


def _rope(a, cs, sn, first):
    r = jnp.where(first, pltpu.roll(a, LANES - ROPE_HALF, 1), pltpu.roll(a, ROPE_HALF, 1))
    return a * cs + r * sn


def _rope_tables(pos_f32, invl, sgn):
    ang = pos_f32 * invl
    return jnp.cos(ang), jnp.sin(ang) * sgn


def _inproj_body(x_ref, pos_ref, g_ref, w_ref, invl_ref, sgn_ref,
                 q_ref, kc_ref, vc_ref, ksa_ref, vs_ref, kw_ref, vw_ref,
                 ga_ref, z_ref, xbc_ref, qx_ref, gx_ref, sm_ref, *, tm, tiles_per_seq):
    x = x_ref[...]
    ms = jnp.mean(x * x, axis=-1, keepdims=True)
    xn = (x * lax.rsqrt(ms + EPS) * g_ref[...]).astype(BF16)

    cs, sn = _rope_tables(pos_ref[...].astype(F32), invl_ref[...], sgn_ref[...])
    lane = lax.broadcasted_iota(I32, (tm, LANES), 1)
    first = (lane % HEAD_DIM) < ROPE_HALF

    def mm(lo, n):
        return _dot(xn, w_ref[:, lo:lo + n])

    qf = mm(OFF_Q, NSA_WIDTH)
    for c in range(NSA_WIDTH // LANES):
        sl = slice(c * LANES, (c + 1) * LANES)
        q_ref[:, sl] = (_rope(qf[:, sl], cs, sn, first) * (HEAD_DIM ** -0.5 * LOG2E)).astype(BF16)

    kv = mm(OFF_KV, 6 * KV_WIDTH)
    k_c, v_c, k_s, v_s, k_w, v_w = [kv[:, n * KV_WIDTH:(n + 1) * KV_WIDTH] for n in range(6)]
    k_s = _rope(k_s, cs, sn, first)
    k_w = _rope(k_w, cs, sn, first)
    s_base = (pl.program_id(0) % tiles_per_seq) * tm
    blk = (s_base + lax.broadcasted_iota(I32, (tm, LANES), 0)) // SLC_BLOCK
    onehot = jnp.where(lane - HEAD_DIM == blk, 1.0, 0.0)
    low = lane < HEAD_DIM
    for g in range(KV_GROUPS):
        gs = slice(g * HEAD_DIM, (g + 1) * HEAD_DIM)
        kc_ref[0, g] = k_c[:, gs].astype(BF16)
        vc_ref[0, g] = v_c[:, gs].astype(BF16)
        vs_ref[0, g] = v_s[:, gs].astype(BF16)
        kw_ref[0, g] = k_w[:, gs].astype(BF16)
        vw_ref[0, g] = v_w[:, gs].astype(BF16)
        ks_g = k_s if g == 0 else pltpu.roll(k_s, HEAD_DIM, 1)
        ksa_ref[0, g] = jnp.where(low, ks_g, onehot).astype(BF16)

    ga_ref[...] = mm(OFF_GA, NSA_WIDTH)
    z_ref[...] = mm(OFF_Z, SSD_WIDTH)
    xbc_ref[...] = mm(OFF_XBC, CONV_DIM)
    qx_ref[...] = (mm(OFF_QX, XA_WIDTH) * (XA_HEAD_DIM ** -0.5)).astype(BF16)
    gx_ref[...] = mm(OFF_GX, XA_WIDTH)
    sm_ref[...] = mm(OFF_SM, LANES)


def _in_proj(x2, pos2, g_in, w_p, invl, sgn, *, batch, seq):
    rows = batch * seq
    tm = ROW_TILE
    tps = seq // tm
    row = lambda n: pl.BlockSpec((tm, n), lambda r: (r, 0))
    full = lambda a: pl.BlockSpec(a.shape, lambda r: (0,) * a.ndim)
    grp = lambda n: pl.BlockSpec((1, KV_GROUPS, tm, n), lambda r: (r // tps, 0, r % tps, 0))
    grp_shape = lambda n: jax.ShapeDtypeStruct((batch, KV_GROUPS, seq, n), BF16)
    flat = lambda n, dt: jax.ShapeDtypeStruct((rows, n), dt)
    return pl.pallas_call(
        functools.partial(_inproj_body, tm=tm, tiles_per_seq=tps),
        grid=(rows // tm,),
        in_specs=[row(D_MODEL), row(1), full(g_in), full(w_p), full(invl), full(sgn)],
        out_specs=[row(NSA_WIDTH), grp(HEAD_DIM), grp(HEAD_DIM), grp(LANES), grp(HEAD_DIM),
                   grp(HEAD_DIM), grp(HEAD_DIM), row(NSA_WIDTH), row(SSD_WIDTH), row(CONV_DIM),
                   row(XA_WIDTH), row(XA_WIDTH), row(LANES)],
        out_shape=[flat(NSA_WIDTH, BF16), grp_shape(HEAD_DIM), grp_shape(HEAD_DIM),
                   grp_shape(LANES), grp_shape(HEAD_DIM), grp_shape(HEAD_DIM), grp_shape(HEAD_DIM),
                   flat(NSA_WIDTH, F32), flat(SSD_WIDTH, F32), flat(CONV_DIM, F32),
                   flat(XA_WIDTH, BF16), flat(XA_WIDTH, F32), flat(LANES, F32)],
        compiler_params=pltpu.CompilerParams(dimension_semantics=("parallel",),
                                             vmem_limit_bytes=VMEM_LIMIT),
        name="in_proj",
    )(x2, pos2, g_in, w_p, invl, sgn)


def _compress_body(ck_ref, cv_ref, posc_ref, w1k_ref, w2k_ref, pk_ref, w1v_ref, w2v_ref, pv_ref,
                   invl_ref, sgn_ref, kc_ref, vc_ref, *, ncp):
    half = CMP_STRIDE * HEAD_DIM

    def mlp(c_ref, w1_ref, w2_ref, p_ref):
        c = c_ref[0, 0]
        a = _dot(c, w1_ref[0:half, :])
        b = _dot(c, w1_ref[half:2 * half, :])
        bias = _dot(p_ref[...], w1_ref[...])[0:1, :]
        h = a + pltpu.roll(b, ncp - 1, 0) + bias
        return _dot(_silu(h).astype(BF16), w2_ref[...])

    kc = mlp(ck_ref, w1k_ref, w2k_ref, pk_ref)
    vc = mlp(cv_ref, w1v_ref, w2v_ref, pv_ref)
    cs, sn = _rope_tables(posc_ref[0].astype(F32), invl_ref[...], sgn_ref[...])
    lane = lax.broadcasted_iota(I32, (ncp, LANES), 1)
    kc = _rope(kc, cs, sn, (lane % HEAD_DIM) < ROPE_HALF)
    kc_ref[0, 0] = kc[:, 0:HEAD_DIM].astype(BF16)
    vc_ref[0, 0] = vc[:, 0:HEAD_DIM].astype(BF16)


def _compress(ck, cv, posc, w1k, w2k, pk, w1v, w2v, pv, invl, sgn, *, batch, ncp):
    chunk = pl.BlockSpec((1, 1, ncp, CMP_STRIDE * HEAD_DIM), lambda b, g: (b, g, 0, 0))
    full = lambda a: pl.BlockSpec(a.shape, lambda b, g: (0,) * a.ndim)
    out = pl.BlockSpec((1, 1, ncp, HEAD_DIM), lambda b, g: (b, g, 0, 0))
    shp = jax.ShapeDtypeStruct((batch, KV_GROUPS, ncp, HEAD_DIM), BF16)
    return pl.pallas_call(
        functools.partial(_compress_body, ncp=ncp),
        grid=(batch, KV_GROUPS),
        in_specs=[chunk, chunk, pl.BlockSpec((1, ncp, 1), lambda b, g: (b, 0, 0)),
                  full(w1k), full(w2k), full(pk), full(w1v), full(w2v), full(pv),
                  full(invl), full(sgn)],
        out_specs=[out, out],
        out_shape=[shp, shp],
        compiler_params=pltpu.CompilerParams(dimension_semantics=("parallel", "parallel"),
                                             vmem_limit_bytes=VMEM_LIMIT),
        name="compress",
    )(ck, cv, posc, w1k, w2k, pk, w1v, w2v, pv, invl, sgn)


def _nsa_body(q_ref, kc_ref, cat_ref, ksa_ref, vst_ref, kw_ref, vwt_ref, ga_ref, sm_ref,
              o_ref, qa_ref, m_ref, acc_ref, oc_ref, sa_ref, sb_ref, *, tq, ncp, topk):
    g = pl.program_id(1)
    i = pl.program_id(2)
    s0 = i * tq
    nh = HEADS_PER_GROUP
    rows = nh * tq

    def col_max(s):
        n = s.shape[0]
        while n > 8:
            n //= 2
            s = jnp.maximum(s[0:n], s[n:2 * n])
        return jnp.max(s, axis=0, keepdims=True)

    def reset(br):
        m_ref[br] = jnp.full((1, rows), NEG, F32)
        acc_ref[br] = jnp.zeros((V_ROWS, rows), F32)

    def consume(br, s, vt_tile, mask):
        if mask is not None:
            s = jnp.where(mask, s, NEG)
        m_prev = m_ref[br]
        m_new = jnp.maximum(m_prev, col_max(s))
        alpha = jnp.exp2(m_prev - m_new)
        p = jnp.exp2(s - m_new).astype(BF16)
        acc_ref[br] = alpha * acc_ref[br] + _dot(vt_tile, p)
        m_ref[br] = m_new

    def sel_scores(j):
        rows_j = pl.ds(pl.multiple_of(j * tq, tq), tq)
        return _nt(ksa_ref[0, 0, rows_j, :], qa_ref[...])

    def win_scores(j):
        rows_j = pl.ds(pl.multiple_of(j * tq, tq), tq)
        return _nt(kw_ref[0, 0, rows_j, :], qa_ref[:, 0:HEAD_DIM])

    q4 = q_ref[...]
    for e in range(nh):
        qa_ref[e * tq:(e + 1) * tq, 0:HEAD_DIM] = q4[:, e * HEAD_DIM:(e + 1) * HEAD_DIM]

    n_io = lax.broadcasted_iota(I32, (ncp, rows), 0)
    t_io = s0 + lax.broadcasted_iota(I32, (ncp, rows), 1) % tq
    s = jnp.where((n_io * CMP_STRIDE + (CMP_LEN - 1)) <= t_io, _nt(kc_ref[0, 0], qa_ref[:, 0:HEAD_DIM]), NEG)
    p = jnp.exp2(s - col_max(s)).astype(BF16)
    big = _dot(cat_ref[0, 0], p)
    t_row = s0 + lax.broadcasted_iota(I32, (1, rows), 1) % tq
    inv = jnp.where(t_row >= CMP_LEN - 1, 1.0 / jnp.maximum(big[HEAD_DIM:HEAD_DIM + 1, :], 1e-30), 0.0)
    oc_ref[...] = big[0:HEAD_DIM, :] * inv
    imp_x = big[CAT_OVL_ROW0:CAT_OVL_ROW0 + SLC_SLOTS, :] * inv
    imp = imp_x[:, 0:tq]
    for e in range(1, nh):
        imp = imp + imp_x[:, e * tq:(e + 1) * tq]

    k_io = lax.broadcasted_iota(I32, (tq, rows), 0)
    t_io2 = lax.broadcasted_iota(I32, (tq, rows), 1) % tq
    diag = k_io <= t_io2
    reset(1)
    consume(1, win_scores(i), vwt_ref[0, 0, i], diag)
    j1 = jnp.maximum(i - 1, 0)
    consume(1, win_scores(j1), vwt_ref[0, 0, j1], i >= 1)
    j2 = jnp.maximum(i - 2, 0)
    consume(1, win_scores(j2), vwt_ref[0, 0, j2], (k_io > t_io2) & (i >= 2))

    j_io = lax.broadcasted_iota(I32, (SLC_SLOTS, tq), 0)
    cur = (s0 + lax.broadcasted_iota(I32, (SLC_SLOTS, tq), 1)) // SLC_BLOCK
    forced = (j_io == 0) | (j_io == cur) | (j_io == cur - 1)
    valid = j_io <= cur
    v = jnp.where(forced, BIG, jnp.where(valid, imp, -BIG))
    sel = jnp.zeros((SLC_SLOTS, tq), F32)
    for _ in range(topk):
        mx = jnp.max(v, axis=0, keepdims=True)
        idx = jnp.min(jnp.where(v == mx, j_io, SLC_SLOTS), axis=0, keepdims=True)
        hit = j_io == idx
        sel = jnp.where(hit, 1.0, sel)
        v = jnp.where(hit, -3e38, v)
    selb = jnp.where((sel > 0.5) & valid, 0.0, NEG)
    selb_t = jnp.concatenate([jnp.zeros((SLC_SLOTS, tq), F32), selb], axis=0).T
    selb_t = selb_t.astype(BF16)
    for e in range(nh):
        qa_ref[e * tq:(e + 1) * tq, HEAD_DIM:LANES] = selb_t[:, HEAD_DIM:LANES]

    reset(0)
    sa_ref[...] = sel_scores(0)

    def sel_pair(jj, carry):
        j = 2 * jj
        sb_ref[...] = sel_scores(j + 1)
        consume(0, sa_ref[...], vst_ref[0, 0, j], None)
        sa_ref[...] = sel_scores(j + 2)
        consume(0, sb_ref[...], vst_ref[0, 0, j + 1], None)
        return carry

    lax.fori_loop(0, i // 2, sel_pair, 0)

    @pl.when(i % 2 == 0)
    def _():
        consume(0, sa_ref[...], vst_ref[0, 0, i], diag)

    @pl.when(i % 2 == 1)
    def _():
        sb_ref[...] = sel_scores(i)
        consume(0, sa_ref[...], vst_ref[0, 0, i - 1], None)
        consume(0, sb_ref[...], vst_ref[0, 0, i], diag)

    gates_t = jax.nn.sigmoid(sm_ref[...]).T
    ga = ga_ref[...]

    def head_out(e):
        cs = slice(e * tq, (e + 1) * tq)

        def gate(c):
            c0 = GATE_LANE0 + e * 3 + c
            c1 = c0 + nh * 3
            return jnp.where(g == 0, gates_t[c0:c0 + 1, :], gates_t[c1:c1 + 1, :])

        o_s = acc_ref[0, 0:HEAD_DIM, cs] / acc_ref[0, HEAD_DIM:HEAD_DIM + 1, cs]
        o_w = acc_ref[1, 0:HEAD_DIM, cs] / acc_ref[1, HEAD_DIM:HEAD_DIM + 1, cs]
        return gate(0) * oc_ref[:, cs] + gate(1) * o_s + gate(2) * o_w

    for pr in range(nh // 2):
        ls = slice(pr * LANES, (pr + 1) * LANES)
        pair = jnp.concatenate([head_out(2 * pr), head_out(2 * pr + 1)], axis=0).T
        o_ref[:, ls] = (pair * _silu(ga[:, ls])).astype(BF16)


def _nsa(q, kc, cat, ksa, vst, kw, vwt, ga, sm, *, batch, seq, ncp, topk):
    tq = NSA_TILE
    assert WINDOW == 2 * tq
    nq = seq // tq
    gw = HEADS_PER_GROUP * HEAD_DIM
    rows = HEADS_PER_GROUP * tq
    per_bg = lambda *shape: pl.BlockSpec((1, 1) + shape, lambda b, g, i: (b, g) + (0,) * len(shape))
    return pl.pallas_call(
        functools.partial(_nsa_body, tq=tq, ncp=ncp, topk=topk),
        grid=(batch, KV_GROUPS, nq),
        in_specs=[pl.BlockSpec((tq, gw), lambda b, g, i: (b * nq + i, g)),
                  per_bg(ncp, HEAD_DIM), per_bg(CAT_ROWS, ncp),
                  per_bg(seq, LANES), per_bg(nq, V_ROWS, tq), per_bg(seq, HEAD_DIM),
                  per_bg(nq, V_ROWS, tq),
                  pl.BlockSpec((tq, gw), lambda b, g, i: (b * nq + i, g)),
                  pl.BlockSpec((tq, LANES), lambda b, g, i: (b * nq + i, 0))],
        out_specs=pl.BlockSpec((tq, gw), lambda b, g, i: (b * nq + i, g)),
        out_shape=jax.ShapeDtypeStruct((batch * seq, NSA_WIDTH), BF16),
        scratch_shapes=[pltpu.VMEM((rows, LANES), BF16),
                        pltpu.VMEM((2, 1, rows), F32),
                        pltpu.VMEM((2, V_ROWS, rows), F32),
                        pltpu.VMEM((HEAD_DIM, rows), F32),
                        pltpu.VMEM((tq, rows), F32), pltpu.VMEM((tq, rows), F32)],
        compiler_params=pltpu.CompilerParams(
            dimension_semantics=("parallel", "parallel", "arbitrary"),
            vmem_limit_bytes=VMEM_LIMIT),
        name="nsa",
    )(q, kc, cat, ksa, vst, kw, vwt, ga, sm)


def _ssd_body(xbc_ref, z_ref, sm_ref, dtt_ref, cw_ref, cb_ref, dtb_ref, dtbt_ref, al_ref, alt_ref,
              dsk_ref, gn_ref, eh_ref, ehw_ref, tril_ref, triu_ref,
              o_ref, ext_ref, xc_ref, y_ref, h_ref, *, ts):
    L = SSD_CHUNK
    N = SSD_STATE
    P = SSD_HEAD_DIM
    E = SSD_HEADS // SSD_GROUPS
    gw = E * P
    pad = 8

    @pl.when(pl.program_id(1) == 0)
    def _():
        ext_ref[0:pad, :] = jnp.zeros((pad, CONV_DIM), F32)
        h_ref[...] = jnp.zeros(h_ref.shape, F32)

    ext_ref[pad:pad + ts, :] = xbc_ref[...]
    acc = cb_ref[...] + ext_ref[pl.ds(pad - (SSD_CONV - 1), ts), :] * cw_ref[0:1, :]
    for k in range(1, SSD_CONV):
        acc = acc + ext_ref[pl.ds(pad - (SSD_CONV - 1) + k, ts), :] * cw_ref[k:k + 1, :]
    xc_ref[...] = _silu(acc)
    ext_ref[0:pad, :] = ext_ref[ts:ts + pad, :]

    a_row = -jnp.exp(al_ref[...])
    a_col = -jnp.exp(alt_ref[...])
    causal = lax.broadcasted_iota(I32, (L, L), 1) <= lax.broadcasted_iota(I32, (L, L), 0)

    for c in range(ts // L):
        rs = slice(c * L, (c + 1) * L)
        xs = xc_ref[rs, 0:SSD_WIDTH]
        dt = jax.nn.softplus(sm_ref[rs, :] + dtb_ref[...])
        a_cs = _dot3_r(tril_ref[...], dt * a_row)
        dt_x = _dot3_l(dt, eh_ref[...])
        acs_x = _dot3_l(a_cs, eh_ref[...])
        acs_w = _dot3_l(a_cs, ehw_ref[...])
        dtt = jax.nn.softplus(dtt_ref[0, :, rs] + dtbt_ref[...])
        acs_t = _dot3_l(dtt * a_col, triu_ref[...])
        a_last = acs_x[L - 1:L, :]
        xdt = xs * dt_x
        xdo = (xdt * jnp.exp(a_last - acs_x)).astype(BF16)
        xdt_b = xdt.astype(BF16)
        pre = jnp.exp(acs_x)
        cdec = jnp.exp(a_last)
        for gi in range(SSD_GROUPS):
            gs = slice(gi * gw, (gi + 1) * gw)
            bm = xc_ref[rs, SSD_WIDTH + gi * N:SSD_WIDTH + (gi + 1) * N]
            cm = xc_ref[rs, SSD_WIDTH + SSD_GROUPS * N + gi * N:SSD_WIDTH + SSD_GROUPS * N + (gi + 1) * N]
            cm_b = cm.astype(BF16)
            cbm = _nt(cm_b, bm.astype(BF16))
            h_prev = h_ref[gi]
            y_off = _dot(cm_b, h_prev.astype(BF16)) * pre[:, gs]
            h_ref[gi] = h_prev * cdec[:, gs] + _dot(bm.T.astype(BF16), xdo[:, gs])
            y_ref[rs, gs] = y_off + xs[:, gs] * dsk_ref[:, gs]
            for e in range(E):
                h = gi * E + e
                hs = slice(h * P, (h + 1) * P)
                d = acs_w[:, h * LANES:(h + 1) * LANES] - acs_t[h:h + 1, :]
                dec = jnp.exp(jnp.where(causal, d, NEG))
                y_ref[rs, hs] += _dot((cbm * dec).astype(BF16), xdt_b[:, hs])

    y = y_ref[...] * _silu(z_ref[...])
    ms = jnp.mean(y * y, axis=-1, keepdims=True)
    o_ref[...] = (y * lax.rsqrt(ms + EPS) * gn_ref[...]).astype(BF16)


def _ssd(xbc, z, sm, dtt, cw, cb, dtb, dtbt, al, alt, dsk, gn, eh, ehw, tril, triu, *, batch, seq):
    ts = SSD_TILE
    nt = seq // ts
    row = lambda n: pl.BlockSpec((ts, n), lambda b, c: (b * nt + c, 0))
    full = lambda a: pl.BlockSpec(a.shape, lambda b, c: (0,) * a.ndim)
    consts = [cw, cb, dtb, dtbt, al, alt, dsk, gn, eh, ehw, tril, triu]
    return pl.pallas_call(
        functools.partial(_ssd_body, ts=ts),
        grid=(batch, nt),
        in_specs=[row(CONV_DIM), row(SSD_WIDTH), row(LANES),
                  pl.BlockSpec((1, SSD_HEADS, ts), lambda b, c: (b, 0, c))]
                 + [full(a) for a in consts],
        out_specs=row(SSD_WIDTH),
        out_shape=jax.ShapeDtypeStruct((batch * seq, SSD_WIDTH), BF16),
        scratch_shapes=[pltpu.VMEM((ts + 8, CONV_DIM), F32), pltpu.VMEM((ts, CONV_DIM), F32),
                        pltpu.VMEM((ts, SSD_WIDTH), F32),
                        pltpu.VMEM((SSD_GROUPS, SSD_STATE, SSD_WIDTH // SSD_GROUPS), F32)],
        compiler_params=pltpu.CompilerParams(dimension_semantics=("parallel", "arbitrary"),
                                             vmem_limit_bytes=VMEM_LIMIT),
        name="ssd",
    )(xbc, z, sm, dtt, *consts)


def _memkv_body(mem_ref, g_ref, w_ref, k_ref, v_ref):
    x = mem_ref[0]
    ms = jnp.mean(x * x, axis=-1, keepdims=True)
    xn = (x * lax.rsqrt(ms + EPS) * g_ref[...]).astype(BF16)
    k_ref[0] = _dot(xn, w_ref[:, 0:XA_WIDTH]).astype(BF16)
    v_ref[0] = _dot(xn, w_ref[:, XA_WIDTH:2 * XA_WIDTH]).astype(BF16)


def _mem_kv(mem, g_mem, w_kv):
    batch, mlen, _ = mem.shape
    full = lambda a: pl.BlockSpec(a.shape, lambda b: (0,) * a.ndim)
    out = pl.BlockSpec((1, mlen, XA_WIDTH), lambda b: (b, 0, 0))
    shp = jax.ShapeDtypeStruct((batch, mlen, XA_WIDTH), BF16)
    return pl.pallas_call(
        _memkv_body,
        grid=(batch,),
        in_specs=[pl.BlockSpec((1, mlen, D_MODEL), lambda b: (b, 0, 0)), full(g_mem), full(w_kv)],
        out_specs=[out, out],
        out_shape=[shp, shp],
        compiler_params=pltpu.CompilerParams(dimension_semantics=("parallel",),
                                             vmem_limit_bytes=VMEM_LIMIT),
        name="mem_kv",
    )(mem, g_mem, w_kv)


def _outproj_body(x_ref, oa_ref, ob_ref, qx_ref, gx_ref, k_ref, v_ref, w_ref, g_ref, o_ref, mix_ref):
    mix_ref[:, 0:NSA_WIDTH] = oa_ref[...]
    mix_ref[:, NSA_WIDTH:NSA_WIDTH + SSD_WIDTH] = ob_ref[...]
    for h in range(XA_HEADS):
        hs = slice(h * XA_HEAD_DIM, (h + 1) * XA_HEAD_DIM)
        s = _nt(qx_ref[:, hs], k_ref[0, :, hs])
        s = s - jnp.max(s, axis=-1, keepdims=True)
        p = jnp.exp(s)
        p = p / jnp.sum(p, axis=-1, keepdims=True)
        oc = _dot(p.astype(BF16), v_ref[0, :, hs]) * _silu(gx_ref[:, hs])
        off = NSA_WIDTH + SSD_WIDTH + h * XA_HEAD_DIM
        mix_ref[:, off:off + XA_HEAD_DIM] = oc.astype(BF16)
    acc = x_ref[...] + _dot(mix_ref[...], w_ref[...])
    ms = jnp.mean(acc * acc, axis=-1, keepdims=True)
    o_ref[...] = acc * lax.rsqrt(ms + EPS) * g_ref[...]


def _out_proj(x2, oa, ob, qx, gx, km, vm, w_out, g_final, *, batch, seq):
    rows = batch * seq
    tm = ROW_TILE
    tps = seq // tm
    mlen = km.shape[1]
    row = lambda n: pl.BlockSpec((tm, n), lambda r: (r, 0))
    full = lambda a: pl.BlockSpec(a.shape, lambda r: (0,) * a.ndim)
    mem = pl.BlockSpec((1, mlen, XA_WIDTH), lambda r: (r // tps, 0, 0))
    return pl.pallas_call(
        _outproj_body,
        grid=(rows // tm,),
        in_specs=[row(D_MODEL), row(NSA_WIDTH), row(SSD_WIDTH), row(XA_WIDTH), row(XA_WIDTH),
                  mem, mem, full(w_out), full(g_final)],
        out_specs=row(D_MODEL),
        out_shape=jax.ShapeDtypeStruct((rows, D_MODEL), F32),
        scratch_shapes=[pltpu.VMEM((tm, MIX_WIDTH), BF16)],
        compiler_params=pltpu.CompilerParams(dimension_semantics=("parallel",),
                                             vmem_limit_bytes=VMEM_LIMIT),
        name="out_proj",
    )(x2, oa, ob, qx, gx, km, vm, w_out, g_final)


def _permute_w_in(w):
    sizes = [NSA_WIDTH] + [KV_WIDTH] * 6 + [NSA_HEADS * 3, NSA_WIDTH, SSD_WIDTH, CONV_DIM, SSD_HEADS,
                                          XA_WIDTH, XA_WIDTH]
    offs = [0]
    for s in sizes:
        offs.append(offs[-1] + s)
    sec = lambda n: w[:, offs[n]:offs[n + 1]]
    small = jnp.concatenate(
        [sec(7), sec(11), jnp.zeros((w.shape[0], LANES - NSA_HEADS * 3 - SSD_HEADS), w.dtype)], axis=1)
    cols = [sec(0)] + [sec(n) for n in range(1, 7)] + [sec(8), sec(9), sec(10), sec(12), sec(13), small]
    return jnp.concatenate(cols, axis=1).astype(BF16)


def _value_tiles(v):
    b, g, s, d = v.shape
    vt = jnp.swapaxes(v.reshape(b, g, s // NSA_TILE, NSA_TILE, d), 3, 4)
    ones = jnp.ones((b, g, s // NSA_TILE, 1, NSA_TILE), v.dtype)
    zeros = jnp.zeros((b, g, s // NSA_TILE, V_ROWS - d - 1, NSA_TILE), v.dtype)
    return jnp.concatenate([vt, ones, zeros], axis=3)


def _lane_row(vals, lane0):
    return jnp.zeros((1, LANES), F32).at[0, lane0:lane0 + vals.shape[0]].set(vals.astype(F32))


def _forward(x, mem, positions, g_in, w_in, cmp_pos_k, w_cmp1_k, w_cmp2_k, cmp_pos_v, w_cmp1_v,
             w_cmp2_v, conv_w, conv_b, dt_bias, a_log, d_skip, g_ssd_norm, g_mem, w_mem_kv, w_out,
             g_final):
    batch, seq, _ = x.shape
    ncp = seq // CMP_STRIDE
    n_slc = seq // SLC_BLOCK
    assert n_slc <= SLC_SLOTS and seq % ROW_TILE == 0 and seq % SSD_TILE == 0
    topk = min(SLC_TOPK, n_slc)
    rows = batch * seq

    inv = ROPE_THETA ** (-jnp.arange(0, ROPE_DIM, 2, dtype=F32) / ROPE_DIM)
    head_inv = jnp.concatenate([inv, inv, jnp.zeros((HEAD_DIM - ROPE_DIM,), F32)])
    invl = jnp.tile(head_inv, LANES // HEAD_DIM)[None, :]
    head_sgn = jnp.concatenate([-jnp.ones((ROPE_HALF,), F32), jnp.ones((ROPE_HALF,), F32),
                                jnp.zeros((HEAD_DIM - ROPE_DIM,), F32)])
    sgn = jnp.tile(head_sgn, LANES // HEAD_DIM)[None, :]

    x2 = x.reshape(rows, D_MODEL)
    pos2 = positions.reshape(rows, 1)
    h = x2
    for l in range(g_in.shape[0]):
        (q, kc_raw, vc_raw, ksa, vs, kw, vw, ga, z, xbc, qx, gx, sm) = _in_proj(
            h, pos2, g_in[l][None, :], _permute_w_in(w_in[l]), invl, sgn, batch=batch, seq=seq)

        chunks = lambda a: a.reshape(batch, KV_GROUPS, ncp, CMP_STRIDE * HEAD_DIM)
        cmp_end = jnp.minimum(jnp.arange(ncp) * CMP_STRIDE + CMP_LEN - 1, seq - 1)
        posc = positions[:, cmp_end][:, :, None]
        pad_w2 = lambda w: jnp.pad(w, ((0, 0), (0, LANES - HEAD_DIM))).astype(BF16)
        pos_rows = lambda p: jnp.broadcast_to(p.reshape(1, CMP_LEN * HEAD_DIM), (8, CMP_LEN * HEAD_DIM)).astype(BF16)
        kc, vc = _compress(chunks(kc_raw), chunks(vc_raw), posc,
                           w_cmp1_k[l].astype(BF16), pad_w2(w_cmp2_k[l]), pos_rows(cmp_pos_k[l]),
                           w_cmp1_v[l].astype(BF16), pad_w2(w_cmp2_v[l]), pos_rows(cmp_pos_v[l]),
                           invl, sgn, batch=batch, ncp=ncp)
        n_ix = jnp.arange(ncp)[None, :]
        j_ix = jnp.arange(SLC_SLOTS)[:, None]
        ovl = ((n_ix * CMP_STRIDE < j_ix * SLC_BLOCK + SLC_BLOCK)
               & (n_ix * CMP_STRIDE + CMP_LEN > j_ix * SLC_BLOCK)
               & (n_ix < ncp - (CMP_LEN // CMP_STRIDE - 1))).astype(BF16)
        bg = (batch, KV_GROUPS)
        cat = jnp.concatenate(
            [jnp.swapaxes(vc, 2, 3), jnp.ones(bg + (1, ncp), BF16),
             jnp.zeros(bg + (CAT_OVL_ROW0 - HEAD_DIM - 1, ncp), BF16),
             jnp.broadcast_to(ovl, bg + ovl.shape),
             jnp.zeros(bg + (CAT_ROWS - CAT_OVL_ROW0 - SLC_SLOTS, ncp), BF16)], axis=2)
        o_a = _nsa(q, kc, cat, ksa, _value_tiles(vs), kw, _value_tiles(vw), ga, sm,
                   batch=batch, seq=seq, ncp=ncp, topk=topk)

        dtt = jnp.swapaxes(sm[:, DT_LANE0:DT_LANE0 + SSD_HEADS].reshape(batch, seq, SSD_HEADS), 1, 2)
        head_of_lane = jnp.arange(SSD_WIDTH) // SSD_HEAD_DIM
        k_ix = jnp.arange(LANES)[:, None]
        eh = (k_ix == DT_LANE0 + head_of_lane[None, :]).astype(BF16)
        ehw = (k_ix == DT_LANE0 + (jnp.arange(SSD_HEADS * LANES) // LANES)[None, :]).astype(BF16)
        t_ix = jnp.arange(SSD_CHUNK)
        tril = (t_ix[None, :] <= t_ix[:, None]).astype(BF16)
        o_b = _ssd(xbc, z, sm, dtt, conv_w[l], conv_b[l][None, :],
                   _lane_row(dt_bias[l], DT_LANE0), dt_bias[l].astype(F32)[:, None],
                   _lane_row(a_log[l], DT_LANE0), a_log[l].astype(F32)[:, None],
                   jnp.repeat(d_skip[l].astype(F32), SSD_HEAD_DIM)[None, :], g_ssd_norm[l][None, :],
                   eh, ehw, tril, tril.T, batch=batch, seq=seq)

        km, vm = _mem_kv(mem, g_mem[l][None, :], w_mem_kv[l].astype(BF16))
        last = l == g_in.shape[0] - 1
        assert last, "multi-layer stacking is not needed for this problem (DEPTH == 1)"
        h = _out_proj(h, o_a, o_b, qx, gx, km, vm, w_out[l].astype(BF16), g_final[None, :],
                      batch=batch, seq=seq)
    return h.reshape(batch, seq, D_MODEL)


def kernel(x, mem, positions, g_in, w_in, cmp_pos_k, w_cmp1_k, w_cmp2_k, cmp_pos_v, w_cmp1_v, w_cmp2_v,
           conv_w, conv_b, dt_bias, a_log, d_skip, g_ssd_norm, g_mem, w_mem_kv, w_out, g_final):
    return _forward(x, mem, positions, g_in, w_in, cmp_pos_k, w_cmp1_k, w_cmp2_k, cmp_pos_v, w_cmp1_v,
                    w_cmp2_v, conv_w, conv_b, dt_bias, a_log, d_skip, g_ssd_norm, g_mem, w_mem_kv,
                    w_out, g_final)
```

```python
import functools

import jax
import jax.numpy as jnp
from jax import lax
from jax.experimental import pallas as pl
from jax.experimental.pallas import tpu as pltpu

F32 = jnp.float32
BF16 = jnp.bfloat16
I32 = jnp.int32

D_MODEL = 1024
NSA_HEADS = 8
HEAD_DIM = 64
KV_GROUPS = 2
HEADS_PER_GROUP = NSA_HEADS // KV_GROUPS
NSA_WIDTH = NSA_HEADS * HEAD_DIM
KV_WIDTH = KV_GROUPS * HEAD_DIM
CMP_LEN = 32
CMP_STRIDE = 16
CMP_HIDDEN = 256
SLC_BLOCK = 64
SLC_TOPK = 16
SLC_SLOTS = 64
WINDOW = 512
BIG = 1e9
NEG = -1e30

SSD_HEADS = 8
SSD_HEAD_DIM = 64
SSD_WIDTH = SSD_HEADS * SSD_HEAD_DIM
SSD_GROUPS = 2
SSD_STATE = 128
SSD_CONV = 4
SSD_CHUNK = 128
CONV_DIM = SSD_WIDTH + 2 * SSD_GROUPS * SSD_STATE

XA_HEADS = 4
XA_HEAD_DIM = 128
XA_WIDTH = XA_HEADS * XA_HEAD_DIM
MIX_WIDTH = NSA_WIDTH + SSD_WIDTH + XA_WIDTH

ROPE_THETA = 500000.0
ROPE_DIM = HEAD_DIM // 4
ROPE_HALF = ROPE_DIM // 2
EPS = 1e-6

LANES = 128
GATE_LANE0 = 0
DT_LANE0 = 24

OFF_Q = 0
OFF_KV = OFF_Q + NSA_WIDTH
OFF_GA = OFF_KV + 6 * KV_WIDTH
OFF_Z = OFF_GA + NSA_WIDTH
OFF_XBC = OFF_Z + SSD_WIDTH
OFF_QX = OFF_XBC + CONV_DIM
OFF_GX = OFF_QX + XA_WIDTH
OFF_SM = OFF_GX + XA_WIDTH
N_PROJ = OFF_SM + LANES

ROW_TILE = 512
NSA_TILE = 256
V_ROWS = 80
LOG2E = 1.4426950408889634
CAT_OVL_ROW0 = 80
CAT_ROWS = 144
SSD_TILE = 512
VMEM_LIMIT = 56 * 1024 * 1024


def _nt(a, b):
    return lax.dot_general(a, b, (((1,), (1,)), ((), ())), preferred_element_type=F32)


def _dot(a, b):
    return jnp.dot(a, b, preferred_element_type=F32)


def _split3(x):
    hi = x.astype(BF16)
    r1 = x - hi.astype(F32)
    mid = r1.astype(BF16)
    lo = (r1 - mid.astype(F32)).astype(BF16)
    return hi, mid, lo


def _dot3_l(x, w):
    hi, mid, lo = _split3(x)
    return _dot(hi, w) + _dot(mid, w) + _dot(lo, w)


def _dot3_r(w, x):
    hi, mid, lo = _split3(x)
    return _dot(w, hi) + _dot(w, mid) + _dot(w, lo)


def _silu(x):
    h = 0.5 * x
    return h + h * jnp.tanh(h)


def _rope(a, cs, sn, first):
    r = jnp.where(first, pltpu.roll(a, LANES - ROPE_HALF, 1), pltpu.roll(a, ROPE_HALF, 1))
    return a * cs + r * sn


def _rope_tables(pos_f32, invl, sgn):
    ang = pos_f32 * invl
    return jnp.cos(ang), jnp.sin(ang) * sgn


def _inproj_body(x_ref, pos_ref, g_ref, w_ref, invc_ref, sgn_ref,
                 q_ref, ck_ref, cv_ref, ksa_ref, vst_ref, kw_ref, vwt_ref,
                 ga_ref, z_ref, xbc_ref, qx_ref, gx_ref, sm_ref, smt_ref, sk_ref, sv_ref, *, tm, tiles_per_seq):
    x = x_ref[...]
    ms = jnp.mean(x * x, axis=-1, keepdims=True)
    xn = (x * lax.rsqrt(ms + EPS) * g_ref[...]).astype(BF16)

    ang = (invc_ref[...] * pos_ref[0].astype(F32)).T
    cs = jnp.cos(ang)
    sn = jnp.sin(ang) * sgn_ref[...]
    lane = lax.broadcasted_iota(I32, (tm, LANES), 1)
    first = (lane % HEAD_DIM) < ROPE_HALF

    def mm(lo, n):
        return _dot(xn, w_ref[:, lo:lo + n])

    qf = mm(OFF_Q, NSA_WIDTH)
    for c in range(NSA_WIDTH // LANES):
        sl = slice(c * LANES, (c + 1) * LANES)
        q_ref[:, sl] = (_rope(qf[:, sl], cs, sn, first) * (HEAD_DIM ** -0.5 * LOG2E)).astype(BF16)

    kv = mm(OFF_KV, 6 * KV_WIDTH)
    k_c, v_c, k_s, v_s, k_w, v_w = [kv[:, n * KV_WIDTH:(n + 1) * KV_WIDTH] for n in range(6)]
    k_s = _rope(k_s, cs, sn, first)
    k_w = _rope(k_w, cs, sn, first)
    s_base = (pl.program_id(0) % tiles_per_seq) * tm
    blk = (s_base + lax.broadcasted_iota(I32, (tm, LANES), 0)) // SLC_BLOCK
    onehot = jnp.where(lane - HEAD_DIM == blk, 1.0, 0.0)
    low = lane < HEAD_DIM
    vs_t = v_s.T
    vw_t = v_w.T
    tail = jnp.where(lax.broadcasted_iota(I32, (V_ROWS - HEAD_DIM, NSA_TILE), 0) == 0, 1.0, 0.0).astype(BF16)
    for g in range(KV_GROUPS):
        gs = slice(g * HEAD_DIM, (g + 1) * HEAD_DIM)
        kw_ref[0, g] = k_w[:, gs].astype(BF16)
        ks_g = k_s if g == 0 else pltpu.roll(k_s, HEAD_DIM, 1)
        ksa_ref[0, g] = jnp.where(low, ks_g, onehot).astype(BF16)
        for c in range(tm // NSA_TILE):
            cols = slice(c * NSA_TILE, (c + 1) * NSA_TILE)
            vst_ref[0, g, c, 0:HEAD_DIM, :] = vs_t[gs, cols].astype(BF16)
            vst_ref[0, g, c, HEAD_DIM:V_ROWS, :] = tail
            vwt_ref[0, g, c, 0:HEAD_DIM, :] = vw_t[gs, cols].astype(BF16)
            vwt_ref[0, g, c, HEAD_DIM:V_ROWS, :] = tail

    sk_ref[...] = k_c
    sv_ref[...] = v_c
    nchunk = tm // CMP_STRIDE
    for t in range(CMP_STRIDE):
        dst = slice(t * HEAD_DIM, (t + 1) * HEAD_DIM)
        kt = sk_ref[pl.ds(t, nchunk, stride=CMP_STRIDE), :].astype(BF16)
        vt = sv_ref[pl.ds(t, nchunk, stride=CMP_STRIDE), :].astype(BF16)
        for g in range(KV_GROUPS):
            gs = slice(g * HEAD_DIM, (g + 1) * HEAD_DIM)
            ck_ref[0, g, :, dst] = kt[:, gs]
            cv_ref[0, g, :, dst] = vt[:, gs]

    ga_ref[...] = mm(OFF_GA, NSA_WIDTH)
    z_ref[...] = mm(OFF_Z, SSD_WIDTH)
    xbc_ref[...] = mm(OFF_XBC, CONV_DIM)
    qx_ref[...] = (mm(OFF_QX, XA_WIDTH) * (XA_HEAD_DIM ** -0.5)).astype(BF16)
    gx_ref[...] = mm(OFF_GX, XA_WIDTH)
    sm = mm(OFF_SM, LANES)
    sm_ref[...] = sm
    smt_ref[0] = sm.T


def _in_proj(x2, pos3, g_in, w_p, invc, sgn, *, batch, seq):
    rows = batch * seq
    tm = ROW_TILE
    tps = seq // tm
    row = lambda n: pl.BlockSpec((tm, n), lambda r: (r, 0))
    full = lambda a: pl.BlockSpec(a.shape, lambda r: (0,) * a.ndim)
    grp = lambda n: pl.BlockSpec((1, KV_GROUPS, tm, n), lambda r: (r // tps, 0, r % tps, 0))
    grp_shape = lambda n: jax.ShapeDtypeStruct((batch, KV_GROUPS, seq, n), BF16)
    chunk = pl.BlockSpec((1, KV_GROUPS, tm // CMP_STRIDE, CMP_STRIDE * HEAD_DIM),
                         lambda r: (r // tps, 0, r % tps, 0))
    chunk_shape = jax.ShapeDtypeStruct((batch, KV_GROUPS, seq // CMP_STRIDE, CMP_STRIDE * HEAD_DIM), BF16)
    vtile = pl.BlockSpec((1, KV_GROUPS, tm // NSA_TILE, V_ROWS, NSA_TILE), lambda r: (r // tps, 0, r % tps, 0, 0))
    vtile_shape = jax.ShapeDtypeStruct((batch, KV_GROUPS, seq // NSA_TILE, V_ROWS, NSA_TILE), BF16)
    flat = lambda n, dt: jax.ShapeDtypeStruct((rows, n), dt)
    return pl.pallas_call(
        functools.partial(_inproj_body, tm=tm, tiles_per_seq=tps),
        grid=(rows // tm,),
        in_specs=[row(D_MODEL), pl.BlockSpec((1, 1, tm), lambda r: (r, 0, 0)),
                  full(g_in), full(w_p), full(invc), full(sgn)],
        out_specs=[row(NSA_WIDTH), chunk, chunk, grp(LANES), vtile, grp(HEAD_DIM), vtile,
                   row(NSA_WIDTH), row(SSD_WIDTH), row(CONV_DIM),
                   row(XA_WIDTH), row(XA_WIDTH), row(LANES),
                   pl.BlockSpec((1, LANES, tm), lambda r: (r, 0, 0))],
        out_shape=[flat(NSA_WIDTH, BF16), chunk_shape, chunk_shape, grp_shape(LANES), vtile_shape,
                   grp_shape(HEAD_DIM), vtile_shape,
                   flat(NSA_WIDTH, F32), flat(SSD_WIDTH, F32), flat(CONV_DIM, F32),
                   flat(XA_WIDTH, BF16), flat(XA_WIDTH, F32), flat(LANES, F32),
                   jax.ShapeDtypeStruct((rows // tm, LANES, tm), F32)],
        scratch_shapes=[pltpu.VMEM((tm, KV_WIDTH), F32), pltpu.VMEM((tm, KV_WIDTH), F32)],
        compiler_params=pltpu.CompilerParams(dimension_semantics=("parallel",),
                                             vmem_limit_bytes=VMEM_LIMIT),
        name="in_proj",
    )(x2, pos3, g_in, w_p, invc, sgn)


def _compress_body(ck_ref, cv_ref, posc_ref, w1k_ref, w2k_ref, pk_ref, w1v_ref, w2v_ref, pv_ref,
                   invl_ref, sgn_ref, kc_ref, vc_ref, *, ncp):
    half = CMP_STRIDE * HEAD_DIM

    def mlp(c_ref, w1_ref, w2_ref, p_ref):
        c = c_ref[0, 0]
        a = _dot(c, w1_ref[0:half, :])
        b = _dot(c, w1_ref[half:2 * half, :])
        bias = _dot(p_ref[...], w1_ref[...])[0:1, :]
        h = a + pltpu.roll(b, ncp - 1, 0) + bias
        return _dot(_silu(h).astype(BF16), w2_ref[...])

    kc = mlp(ck_ref, w1k_ref, w2k_ref, pk_ref)
    vc = mlp(cv_ref, w1v_ref, w2v_ref, pv_ref)
    cs, sn = _rope_tables(posc_ref[0].astype(F32), invl_ref[...], sgn_ref[...])
    lane = lax.broadcasted_iota(I32, (ncp, LANES), 1)
    kc = _rope(kc, cs, sn, (lane % HEAD_DIM) < ROPE_HALF)
    kc_ref[0, 0] = kc[:, 0:HEAD_DIM].astype(BF16)
    vc_ref[0, 0] = vc[:, 0:HEAD_DIM].astype(BF16)


def _compress(ck, cv, posc, w1k, w2k, pk, w1v, w2v, pv, invl, sgn, *, batch, ncp):
    chunk = pl.BlockSpec((1, 1, ncp, CMP_STRIDE * HEAD_DIM), lambda b, g: (b, g, 0, 0))
    full = lambda a: pl.BlockSpec(a.shape, lambda b, g: (0,) * a.ndim)
    out = pl.BlockSpec((1, 1, ncp, HEAD_DIM), lambda b, g: (b, g, 0, 0))
    shp = jax.ShapeDtypeStruct((batch, KV_GROUPS, ncp, HEAD_DIM), BF16)
    return pl.pallas_call(
        functools.partial(_compress_body, ncp=ncp),
        grid=(batch, KV_GROUPS),
        in_specs=[chunk, chunk, pl.BlockSpec((1, ncp, 1), lambda b, g: (b, 0, 0)),
                  full(w1k), full(w2k), full(pk), full(w1v), full(w2v), full(pv),
                  full(invl), full(sgn)],
        out_specs=[out, out],
        out_shape=[shp, shp],
        compiler_params=pltpu.CompilerParams(dimension_semantics=("parallel", "parallel"),
                                             vmem_limit_bytes=VMEM_LIMIT),
        name="compress",
    )(ck, cv, posc, w1k, w2k, pk, w1v, w2v, pv, invl, sgn)


def _nsa_body(q_ref, kc_ref, cat_ref, ksa_ref, vst_ref, kw_ref, vwt_ref, ga_ref, sm_ref,
              o_ref, qa_ref, m_ref, acc_ref, oc_ref, sc_ref, mc_ref, *, tq, ncp, topk):
    g = pl.program_id(1)
    i = pl.program_id(2)
    s0 = i * tq
    nh = HEADS_PER_GROUP
    rows = nh * tq

    def col_max(s):
        n = s.shape[0]
        while n > 8:
            n //= 2
            s = jnp.maximum(s[0:n], s[n:2 * n])
        return jnp.max(s, axis=0, keepdims=True)

    def reset(br):
        m_ref[br] = jnp.full((1, rows), NEG, F32)
        acc_ref[br] = jnp.zeros((V_ROWS, rows), F32)

    def consume(br, s, vt_tile, mask, s_max=None):
        if mask is not None:
            s = jnp.where(mask, s, NEG)
        if s_max is None:
            s_max = col_max(s)
        m_prev = m_ref[br]
        m_new = jnp.maximum(m_prev, s_max)
        alpha = jnp.exp2(m_prev - m_new)
        p = jnp.exp2(s - m_new).astype(BF16)
        acc_ref[br] = alpha * acc_ref[br] + _dot(vt_tile, p)
        m_ref[br] = m_new

    def sel_scores(j):
        rows_j = pl.ds(pl.multiple_of(j * tq, tq), tq)
        return _nt(ksa_ref[0, 0, rows_j, :], qa_ref[...])

    def win_scores(j):
        rows_j = pl.ds(pl.multiple_of(j * tq, tq), tq)
        return _nt(kw_ref[0, 0, rows_j, :], qa_ref[:, 0:HEAD_DIM])

    q4 = q_ref[...]
    for e in range(nh):
        qa_ref[e * tq:(e + 1) * tq, 0:HEAD_DIM] = q4[:, e * HEAD_DIM:(e + 1) * HEAD_DIM]

    n_io = lax.broadcasted_iota(I32, (ncp, rows), 0)
    t_io = s0 + lax.broadcasted_iota(I32, (ncp, rows), 1) % tq
    k_io = lax.broadcasted_iota(I32, (tq, rows), 0)
    t_io2 = lax.broadcasted_iota(I32, (tq, rows), 1) % tq
    diag = k_io <= t_io2
    j1 = jnp.maximum(i - 1, 0)
    j2 = jnp.maximum(i - 2, 0)
    s = jnp.where((n_io * CMP_STRIDE + (CMP_LEN - 1)) <= t_io, _nt(kc_ref[0, 0], qa_ref[:, 0:HEAD_DIM]), NEG)
    sc_ref[0, 0:ncp, :] = s
    mc_ref[0] = col_max(s)
    sc_ref[1] = win_scores(i)
    p = jnp.exp2(sc_ref[0, 0:ncp, :] - mc_ref[0]).astype(BF16)
    big = _dot(cat_ref[0, 0], p)
    sc_ref[0] = win_scores(j1)
    t_row = s0 + lax.broadcasted_iota(I32, (1, rows), 1) % tq
    inv = jnp.where(t_row >= CMP_LEN - 1, 1.0 / jnp.maximum(big[HEAD_DIM:HEAD_DIM + 1, :], 1e-30), 0.0)
    oc_ref[...] = big[0:HEAD_DIM, :] * inv
    imp_x = big[CAT_OVL_ROW0:CAT_OVL_ROW0 + SLC_SLOTS, :] * inv
    imp = imp_x[:, 0:tq]
    for e in range(1, nh):
        imp = imp + imp_x[:, e * tq:(e + 1) * tq]

    reset(1)
    consume(1, sc_ref[1], vwt_ref[0, 0, i], diag)
    sc_ref[1] = win_scores(j2)
    consume(1, sc_ref[0], vwt_ref[0, 0, j1], i >= 1)
    consume(1, sc_ref[1], vwt_ref[0, 0, j2], (k_io > t_io2) & (i >= 2))

    j_io = lax.broadcasted_iota(I32, (SLC_SLOTS, tq), 0)
    cur = (s0 + lax.broadcasted_iota(I32, (SLC_SLOTS, tq), 1)) // SLC_BLOCK
    forced = (j_io == 0) | (j_io == cur) | (j_io == cur - 1)
    valid = j_io <= cur
    v = jnp.where(forced, BIG, jnp.where(valid, imp, -BIG))
    sel = jnp.zeros((SLC_SLOTS, tq), F32)
    for _ in range(topk):
        mx = jnp.max(v, axis=0, keepdims=True)
        idx = jnp.min(jnp.where(v == mx, j_io, SLC_SLOTS), axis=0, keepdims=True)
        hit = j_io == idx
        sel = jnp.where(hit, 1.0, sel)
        v = jnp.where(hit, -3e38, v)
    selb = jnp.where((sel > 0.5) & valid, 0.0, NEG)
    selb_t = jnp.concatenate([jnp.zeros((SLC_SLOTS, tq), F32), selb], axis=0).T
    selb_t = selb_t.astype(BF16)
    for e in range(nh):
        qa_ref[e * tq:(e + 1) * tq, HEAD_DIM:LANES] = selb_t[:, HEAD_DIM:LANES]

    def produce(slot, j):
        s = sel_scores(j)
        sc_ref[slot] = s
        mc_ref[slot] = col_max(s)

    reset(0)
    produce(0, 0)

    def sel_pair(j):
        produce(1, j + 1)
        consume(0, sc_ref[0], vst_ref[0, 0, j], None, mc_ref[0])
        produce(0, j + 2)
        consume(0, sc_ref[1], vst_ref[0, 0, j + 1], None, mc_ref[1])

    def sel_quad(jj, carry):
        sel_pair(4 * jj)
        sel_pair(4 * jj + 2)
        return carry

    def sel_rest(jj, carry):
        sel_pair((i // 4) * 4 + 2 * jj)
        return carry

    lax.fori_loop(0, i // 4, sel_quad, 0)
    lax.fori_loop(0, (i % 4) // 2, sel_rest, 0)

    @pl.when(i % 2 == 0)
    def _():
        consume(0, sc_ref[0], vst_ref[0, 0, i], diag)

    @pl.when(i % 2 == 1)
    def _():
        sc_ref[1] = sel_scores(i)
        consume(0, sc_ref[0], vst_ref[0, 0, i - 1], None, mc_ref[0])
        consume(0, sc_ref[1], vst_ref[0, 0, i], diag)

    gates_t = jax.nn.sigmoid(sm_ref[...]).T
    ga = ga_ref[...]

    def head_out(e):
        cs = slice(e * tq, (e + 1) * tq)

        def gate(c):
            c0 = GATE_LANE0 + e * 3 + c
            c1 = c0 + nh * 3
            return jnp.where(g == 0, gates_t[c0:c0 + 1, :], gates_t[c1:c1 + 1, :])

        o_s = acc_ref[0, 0:HEAD_DIM, cs] / acc_ref[0, HEAD_DIM:HEAD_DIM + 1, cs]
        o_w = acc_ref[1, 0:HEAD_DIM, cs] / acc_ref[1, HEAD_DIM:HEAD_DIM + 1, cs]
        return gate(0) * oc_ref[:, cs] + gate(1) * o_s + gate(2) * o_w

    for pr in range(nh // 2):
        ls = slice(pr * LANES, (pr + 1) * LANES)
        pair = jnp.concatenate([head_out(2 * pr), head_out(2 * pr + 1)], axis=0).T
        o_ref[:, ls] = (pair * _silu(ga[:, ls])).astype(BF16)


def _nsa(q, kc, cat, ksa, vst, kw, vwt, ga, sm, *, batch, seq, ncp, topk):
    tq = NSA_TILE
    assert WINDOW == 2 * tq
    nq = seq // tq
    gw = HEADS_PER_GROUP * HEAD_DIM
    rows = HEADS_PER_GROUP * tq
    per_bg = lambda *shape: pl.BlockSpec((1, 1) + shape, lambda b, g, i: (b, g) + (0,) * len(shape))
    return pl.pallas_call(
        functools.partial(_nsa_body, tq=tq, ncp=ncp, topk=topk),
        grid=(batch, KV_GROUPS, nq),
        in_specs=[pl.BlockSpec((tq, gw), lambda b, g, i: (b * nq + i, g)),
                  per_bg(ncp, HEAD_DIM), per_bg(CAT_ROWS, ncp),
                  per_bg(seq, LANES), per_bg(nq, V_ROWS, tq), per_bg(seq, HEAD_DIM),
                  per_bg(nq, V_ROWS, tq),
                  pl.BlockSpec((tq, gw), lambda b, g, i: (b * nq + i, g)),
                  pl.BlockSpec((tq, LANES), lambda b, g, i: (b * nq + i, 0))],
        out_specs=pl.BlockSpec((tq, gw), lambda b, g, i: (b * nq + i, g)),
        out_shape=jax.ShapeDtypeStruct((batch * seq, NSA_WIDTH), BF16),
        scratch_shapes=[pltpu.VMEM((rows, LANES), BF16),
                        pltpu.VMEM((2, 1, rows), F32),
                        pltpu.VMEM((2, V_ROWS, rows), F32),
                        pltpu.VMEM((HEAD_DIM, rows), F32),
                        pltpu.VMEM((2, tq, rows), F32), pltpu.VMEM((2, 1, rows), F32)],
        compiler_params=pltpu.CompilerParams(
            dimension_semantics=("parallel", "parallel", "arbitrary"),
            vmem_limit_bytes=VMEM_LIMIT),
        name="nsa",
    )(q, kc, cat, ksa, vst, kw, vwt, ga, sm)


def _ssd_body(xbc_ref, z_ref, sm_ref, dtt_ref, cw_ref, cb_ref, dtb_ref, dtbt_ref, al_ref, alt_ref,
              dsk_ref, gn_ref, eh_ref, ehw_ref, tril_ref, triu_ref,
              o_ref, ext_ref, xc_ref, y_ref, h_ref, *, ts):
    L = SSD_CHUNK
    N = SSD_STATE
    P = SSD_HEAD_DIM
    E = SSD_HEADS // SSD_GROUPS
    gw = E * P
    pad = 8

    @pl.when(pl.program_id(1) == 0)
    def _():
        ext_ref[0:pad, :] = jnp.zeros((pad, CONV_DIM), F32)
        h_ref[...] = jnp.zeros(h_ref.shape, F32)

    ext_ref[pad:pad + ts, :] = xbc_ref[...]
    acc = cb_ref[...] + ext_ref[pl.ds(pad - (SSD_CONV - 1), ts), :] * cw_ref[0:1, :]
    for k in range(1, SSD_CONV):
        acc = acc + ext_ref[pl.ds(pad - (SSD_CONV - 1) + k, ts), :] * cw_ref[k:k + 1, :]
    xc_ref[...] = _silu(acc)
    ext_ref[0:pad, :] = ext_ref[ts:ts + pad, :]

    a_row = -jnp.exp(al_ref[...])
    a_col = -jnp.exp(alt_ref[...])
    causal = lax.broadcasted_iota(I32, (L, L), 1) <= lax.broadcasted_iota(I32, (L, L), 0)

    for c in range(ts // L):
        rs = slice(c * L, (c + 1) * L)
        xs = xc_ref[rs, 0:SSD_WIDTH]
        dt = jax.nn.softplus(sm_ref[rs, :] + dtb_ref[...])
        a_cs = _dot3_r(tril_ref[...], dt * a_row)
        dt_x = _dot3_l(dt, eh_ref[...])
        acs_x = _dot3_l(a_cs, eh_ref[...])
        acs_w = _dot3_l(a_cs, ehw_ref[...])
        dtt = jax.nn.softplus(dtt_ref[0, DT_LANE0:DT_LANE0 + SSD_HEADS, rs] + dtbt_ref[...])
        acs_t = _dot3_l(dtt * a_col, triu_ref[...])
        a_last = acs_x[L - 1:L, :]
        xdt = xs * dt_x
        xdo = (xdt * jnp.exp(a_last - acs_x)).astype(BF16)
        xdt_b = xdt.astype(BF16)
        pre = jnp.exp(acs_x)
        cdec = jnp.exp(a_last)
        for gi in range(SSD_GROUPS):
            gs = slice(gi * gw, (gi + 1) * gw)
            bm = xc_ref[rs, SSD_WIDTH + gi * N:SSD_WIDTH + (gi + 1) * N]
            cm = xc_ref[rs, SSD_WIDTH + SSD_GROUPS * N + gi * N:SSD_WIDTH + SSD_GROUPS * N + (gi + 1) * N]
            cm_b = cm.astype(BF16)
            cbm = _nt(cm_b, bm.astype(BF16))
            h_prev = h_ref[gi]
            y_off = _dot(cm_b, h_prev.astype(BF16)) * pre[:, gs]
            h_ref[gi] = h_prev * cdec[:, gs] + _dot(bm.T.astype(BF16), xdo[:, gs])
            y_ref[rs, gs] = y_off + xs[:, gs] * dsk_ref[:, gs]
            for e in range(E):
                h = gi * E + e
                hs = slice(h * P, (h + 1) * P)
                d = acs_w[:, h * LANES:(h + 1) * LANES] - acs_t[h:h + 1, :]
                dec = jnp.exp(jnp.where(causal, d, NEG))
                y_ref[rs, hs] += _dot((cbm * dec).astype(BF16), xdt_b[:, hs])

    y = y_ref[...] * _silu(z_ref[...])
    ms = jnp.mean(y * y, axis=-1, keepdims=True)
    o_ref[...] = (y * lax.rsqrt(ms + EPS) * gn_ref[...]).astype(BF16)


def _ssd(xbc, z, sm, dtt, cw, cb, dtb, dtbt, al, alt, dsk, gn, eh, ehw, tril, triu, *, batch, seq):
    ts = SSD_TILE
    nt = seq // ts
    row = lambda n: pl.BlockSpec((ts, n), lambda b, c: (b * nt + c, 0))
    full = lambda a: pl.BlockSpec(a.shape, lambda b, c: (0,) * a.ndim)
    consts = [cw, cb, dtb, dtbt, al, alt, dsk, gn, eh, ehw, tril, triu]
    return pl.pallas_call(
        functools.partial(_ssd_body, ts=ts),
        grid=(batch, nt),
        in_specs=[row(CONV_DIM), row(SSD_WIDTH), row(LANES),
                  pl.BlockSpec((1, LANES, ts), lambda b, c: (b * nt + c, 0, 0))]
                 + [full(a) for a in consts],
        out_specs=row(SSD_WIDTH),
        out_shape=jax.ShapeDtypeStruct((batch * seq, SSD_WIDTH), BF16),
        scratch_shapes=[pltpu.VMEM((ts + 8, CONV_DIM), F32), pltpu.VMEM((ts, CONV_DIM), F32),
                        pltpu.VMEM((ts, SSD_WIDTH), F32),
                        pltpu.VMEM((SSD_GROUPS, SSD_STATE, SSD_WIDTH // SSD_GROUPS), F32)],
        compiler_params=pltpu.CompilerParams(dimension_semantics=("parallel", "arbitrary"),
                                             vmem_limit_bytes=VMEM_LIMIT),
        name="ssd",
    )(xbc, z, sm, dtt, *consts)


def _memkv_body(mem_ref, g_ref, w_ref, k_ref, v_ref):
    x = mem_ref[0]
    ms = jnp.mean(x * x, axis=-1, keepdims=True)
    xn = (x * lax.rsqrt(ms + EPS) * g_ref[...]).astype(BF16)
    k_ref[0] = _dot(xn, w_ref[:, 0:XA_WIDTH]).astype(BF16)
    v_ref[0] = _dot(xn, w_ref[:, XA_WIDTH:2 * XA_WIDTH]).astype(BF16)


def _mem_kv(mem, g_mem, w_kv):
    batch, mlen, _ = mem.shape
    full = lambda a: pl.BlockSpec(a.shape, lambda b: (0,) * a.ndim)
    out = pl.BlockSpec((1, mlen, XA_WIDTH), lambda b: (b, 0, 0))
    shp = jax.ShapeDtypeStruct((batch, mlen, XA_WIDTH), BF16)
    return pl.pallas_call(
        _memkv_body,
        grid=(batch,),
        in_specs=[pl.BlockSpec((1, mlen, D_MODEL), lambda b: (b, 0, 0)), full(g_mem), full(w_kv)],
        out_specs=[out, out],
        out_shape=[shp, shp],
        compiler_params=pltpu.CompilerParams(dimension_semantics=("parallel",),
                                             vmem_limit_bytes=VMEM_LIMIT),
        name="mem_kv",
    )(mem, g_mem, w_kv)


def _outproj_body(x_ref, oa_ref, ob_ref, qx_ref, gx_ref, k_ref, v_ref, w_ref, g_ref, o_ref, mix_ref):
    mix_ref[:, 0:NSA_WIDTH] = oa_ref[...]
    mix_ref[:, NSA_WIDTH:NSA_WIDTH + SSD_WIDTH] = ob_ref[...]
    for h in range(XA_HEADS):
        hs = slice(h * XA_HEAD_DIM, (h + 1) * XA_HEAD_DIM)
        s = _nt(qx_ref[:, hs], k_ref[0, :, hs])
        s = s - jnp.max(s, axis=-1, keepdims=True)
        p = jnp.exp(s)
        p = p / jnp.sum(p, axis=-1, keepdims=True)
        oc = _dot(p.astype(BF16), v_ref[0, :, hs]) * _silu(gx_ref[:, hs])
        off = NSA_WIDTH + SSD_WIDTH + h * XA_HEAD_DIM
        mix_ref[:, off:off + XA_HEAD_DIM] = oc.astype(BF16)
    acc = x_ref[...] + _dot(mix_ref[...], w_ref[...])
    ms = jnp.mean(acc * acc, axis=-1, keepdims=True)
    o_ref[...] = acc * lax.rsqrt(ms + EPS) * g_ref[...]


def _out_proj(x2, oa, ob, qx, gx, km, vm, w_out, g_final, *, batch, seq):
    rows = batch * seq
    tm = ROW_TILE
    tps = seq // tm
    mlen = km.shape[1]
    row = lambda n: pl.BlockSpec((tm, n), lambda r: (r, 0))
    full = lambda a: pl.BlockSpec(a.shape, lambda r: (0,) * a.ndim)
    mem = pl.BlockSpec((1, mlen, XA_WIDTH), lambda r: (r // tps, 0, 0))
    return pl.pallas_call(
        _outproj_body,
        grid=(rows // tm,),
        in_specs=[row(D_MODEL), row(NSA_WIDTH), row(SSD_WIDTH), row(XA_WIDTH), row(XA_WIDTH),
                  mem, mem, full(w_out), full(g_final)],
        out_specs=row(D_MODEL),
        out_shape=jax.ShapeDtypeStruct((rows, D_MODEL), F32),
        scratch_shapes=[pltpu.VMEM((tm, MIX_WIDTH), BF16)],
        compiler_params=pltpu.CompilerParams(dimension_semantics=("parallel",),
                                             vmem_limit_bytes=VMEM_LIMIT),
        name="out_proj",
    )(x2, oa, ob, qx, gx, km, vm, w_out, g_final)


def _permute_w_in(w):
    sizes = [NSA_WIDTH] + [KV_WIDTH] * 6 + [NSA_HEADS * 3, NSA_WIDTH, SSD_WIDTH, CONV_DIM, SSD_HEADS,
                                          XA_WIDTH, XA_WIDTH]
    offs = [0]
    for s in sizes:
        offs.append(offs[-1] + s)
    sec = lambda n: w[:, offs[n]:offs[n + 1]]
    small = jnp.concatenate(
        [sec(7), sec(11), jnp.zeros((w.shape[0], LANES - NSA_HEADS * 3 - SSD_HEADS), w.dtype)], axis=1)
    cols = [sec(0)] + [sec(n) for n in range(1, 7)] + [sec(8), sec(9), sec(10), sec(12), sec(13), small]
    return jnp.concatenate(cols, axis=1).astype(BF16)


def _lane_row(vals, lane0):
    return jnp.zeros((1, LANES), F32).at[0, lane0:lane0 + vals.shape[0]].set(vals.astype(F32))


def _forward(x, mem, positions, g_in, w_in, cmp_pos_k, w_cmp1_k, w_cmp2_k, cmp_pos_v, w_cmp1_v,
             w_cmp2_v, conv_w, conv_b, dt_bias, a_log, d_skip, g_ssd_norm, g_mem, w_mem_kv, w_out,
             g_final):
    batch, seq, _ = x.shape
    ncp = seq // CMP_STRIDE
    n_slc = seq // SLC_BLOCK
    assert n_slc <= SLC_SLOTS and seq % ROW_TILE == 0 and SSD_TILE == ROW_TILE and ncp <= NSA_TILE
    topk = min(SLC_TOPK, n_slc)
    rows = batch * seq

    inv = ROPE_THETA ** (-jnp.arange(0, ROPE_DIM, 2, dtype=F32) / ROPE_DIM)
    head_inv = jnp.concatenate([inv, inv, jnp.zeros((HEAD_DIM - ROPE_DIM,), F32)])
    invl = jnp.tile(head_inv, LANES // HEAD_DIM)[None, :]
    head_sgn = jnp.concatenate([-jnp.ones((ROPE_HALF,), F32), jnp.ones((ROPE_HALF,), F32),
                                jnp.zeros((HEAD_DIM - ROPE_DIM,), F32)])
    sgn = jnp.tile(head_sgn, LANES // HEAD_DIM)[None, :]

    x2 = x.reshape(rows, D_MODEL)
    pos3 = positions.reshape(rows // ROW_TILE, 1, ROW_TILE)
    h = x2
    for l in range(g_in.shape[0]):
        (q, ck, cv, ksa, vst, kw, vwt, ga, z, xbc, qx, gx, sm, smt) = _in_proj(
            h, pos3, g_in[l][None, :], _permute_w_in(w_in[l]), invl.T, sgn, batch=batch, seq=seq)

        cmp_end = jnp.minimum(jnp.arange(ncp) * CMP_STRIDE + CMP_LEN - 1, seq - 1)
        posc = positions[:, cmp_end][:, :, None]
        pad_w2 = lambda w: jnp.pad(w, ((0, 0), (0, LANES - HEAD_DIM))).astype(BF16)
        pos_rows = lambda p: jnp.broadcast_to(p.reshape(1, CMP_LEN * HEAD_DIM), (8, CMP_LEN * HEAD_DIM)).astype(BF16)
        kc, vc = _compress(ck, cv, posc,
                           w_cmp1_k[l].astype(BF16), pad_w2(w_cmp2_k[l]), pos_rows(cmp_pos_k[l]),
                           w_cmp1_v[l].astype(BF16), pad_w2(w_cmp2_v[l]), pos_rows(cmp_pos_v[l]),
                           invl, sgn, batch=batch, ncp=ncp)
        n_ix = jnp.arange(ncp)[None, :]
        j_ix = jnp.arange(SLC_SLOTS)[:, None]
        ovl = ((n_ix * CMP_STRIDE < j_ix * SLC_BLOCK + SLC_BLOCK)
               & (n_ix * CMP_STRIDE + CMP_LEN > j_ix * SLC_BLOCK)
               & (n_ix < ncp - (CMP_LEN // CMP_STRIDE - 1))).astype(BF16)
        bg = (batch, KV_GROUPS)
        cat = jnp.concatenate(
            [jnp.swapaxes(vc, 2, 3), jnp.ones(bg + (1, ncp), BF16),
             jnp.zeros(bg + (CAT_OVL_ROW0 - HEAD_DIM - 1, ncp), BF16),
             jnp.broadcast_to(ovl, bg + ovl.shape),
             jnp.zeros(bg + (CAT_ROWS - CAT_OVL_ROW0 - SLC_SLOTS, ncp), BF16)], axis=2)
        o_a = _nsa(q, kc, cat, ksa, vst, kw, vwt, ga, sm,
                   batch=batch, seq=seq, ncp=ncp, topk=topk)

        head_of_lane = jnp.arange(SSD_WIDTH) // SSD_HEAD_DIM
        k_ix = jnp.arange(LANES)[:, None]
        eh = (k_ix == DT_LANE0 + head_of_lane[None, :]).astype(BF16)
        ehw = (k_ix == DT_LANE0 + (jnp.arange(SSD_HEADS * LANES) // LANES)[None, :]).astype(BF16)
        t_ix = jnp.arange(SSD_CHUNK)
        tril = (t_ix[None, :] <= t_ix[:, None]).astype(BF16)
        o_b = _ssd(xbc, z, sm, smt, conv_w[l], conv_b[l][None, :],
                   _lane_row(dt_bias[l], DT_LANE0), dt_bias[l].astype(F32)[:, None],
                   _lane_row(a_log[l], DT_LANE0), a_log[l].astype(F32)[:, None],
                   jnp.repeat(d_skip[l].astype(F32), SSD_HEAD_DIM)[None, :], g_ssd_norm[l][None, :],
                   eh, ehw, tril, tril.T, batch=batch, seq=seq)

        km, vm = _mem_kv(mem, g_mem[l][None, :], w_mem_kv[l].astype(BF16))
        last = l == g_in.shape[0] - 1
        assert last, "multi-layer stacking is not needed for this problem (DEPTH == 1)"
        h = _out_proj(h, o_a, o_b, qx, gx, km, vm, w_out[l].astype(BF16), g_final[None, :],
                      batch=batch, seq=seq)
    return h.reshape(batch, seq, D_MODEL)


def kernel(x, mem, positions, g_in, w_in, cmp_pos_k, w_cmp1_k, w_cmp2_k, cmp_pos_v, w_cmp1_v, w_cmp2_v,
           conv_w, conv_b, dt_bias, a_log, d_skip, g_ssd_norm, g_mem, w_mem_kv, w_out, g_final):
    return _forward(x, mem, positions, g_in, w_in, cmp_pos_k, w_cmp1_k, w_cmp2_k, cmp_pos_v, w_cmp1_v,
                    w_cmp2_v, conv_w, conv_b, dt_bias, a_log, d_skip, g_ssd_norm, g_mem, w_mem_kv,
                    w_out, g_final)
```

```python
import functools

import jax
import jax.numpy as jnp
from jax import lax
from jax.experimental import pallas as pl
from jax.experimental.pallas import tpu as pltpu

F32 = jnp.float32
BF16 = jnp.bfloat16
I32 = jnp.int32

D_MODEL = 1024
NSA_HEADS = 8
HEAD_DIM = 64
KV_GROUPS = 2
HEADS_PER_GROUP = NSA_HEADS // KV_GROUPS
NSA_WIDTH = NSA_HEADS * HEAD_DIM
KV_WIDTH = KV_GROUPS * HEAD_DIM
CMP_LEN = 32
CMP_STRIDE = 16
CMP_HIDDEN = 256
SLC_BLOCK = 64
SLC_TOPK = 16
SLC_SLOTS = 64
WINDOW = 512
BIG = 1e9
NEG = -1e30

SSD_HEADS = 8
SSD_HEAD_DIM = 64
SSD_WIDTH = SSD_HEADS * SSD_HEAD_DIM
SSD_GROUPS = 2
SSD_STATE = 128
SSD_CONV = 4
SSD_CHUNK = 128
CONV_DIM = SSD_WIDTH + 2 * SSD_GROUPS * SSD_STATE

XA_HEADS = 4
XA_HEAD_DIM = 128
XA_WIDTH = XA_HEADS * XA_HEAD_DIM
MIX_WIDTH = NSA_WIDTH + SSD_WIDTH + XA_WIDTH

ROPE_THETA = 500000.0
ROPE_DIM = HEAD_DIM // 4
ROPE_HALF = ROPE_DIM // 2
EPS = 1e-6

LANES = 128
GATE_LANE0 = 0
DT_LANE0 = 24

OFF_Q = 0
OFF_KV = OFF_Q + NSA_WIDTH
OFF_GA = OFF_KV + 6 * KV_WIDTH
OFF_Z = OFF_GA + NSA_WIDTH
OFF_XBC = OFF_Z + SSD_WIDTH
OFF_QX = OFF_XBC + CONV_DIM
OFF_GX = OFF_QX + XA_WIDTH
OFF_SM = OFF_GX + XA_WIDTH
N_PROJ = OFF_SM + LANES

ROW_TILE = 512
NSA_TILE = 256
V_ROWS = 80
LOG2E = 1.4426950408889634
CAT_OVL_ROW0 = 80
CAT_ROWS = 144
VMEM_LIMIT = 56 * 1024 * 1024


def _nt(a, b):
    return lax.dot_general(a, b, (((1,), (1,)), ((), ())), preferred_element_type=F32)


def _dot(a, b):
    return jnp.dot(a, b, preferred_element_type=F32)


def _split2(x):
    hi = x.astype(BF16)
    return hi, (x - hi.astype(F32)).astype(BF16)


def _dot2_l(x, w):
    hi, lo = _split2(x)
    return _dot(hi, w) + _dot(lo, w)


def _dot2_r(w, x):
    hi, lo = _split2(x)
    return _dot(w, hi) + _dot(w, lo)


def _silu(x):
    h = 0.5 * x
    return h + h * jnp.tanh(h)


def _rope(a, cs, sn, first):
    r = jnp.where(first, pltpu.roll(a, LANES - ROPE_HALF, 1), pltpu.roll(a, ROPE_HALF, 1))
    return a * cs + r * sn


def _rope_tables(pos_f32, invl, sgn):
    ang = pos_f32 * invl
    return jnp.cos(ang), jnp.sin(ang) * sgn


def _inproj_body(x_ref, pos_ref, g_ref, w_ref, invc_ref, sgn_ref,
                 q_ref, ck_ref, cv_ref, ksa_ref, vst_ref, kw_ref, vwt_ref,
                 ga_ref, z_ref, xbc_ref, qx_ref, gx_ref, sm_ref, smt_ref, sk_ref, sv_ref, *, tm, tiles_per_seq):
    x = x_ref[...]
    ms = jnp.mean(x * x, axis=-1, keepdims=True)
    xn = (x * lax.rsqrt(ms + EPS) * g_ref[...]).astype(BF16)

    ang = (invc_ref[...] * pos_ref[0].astype(F32)).T
    cs = jnp.cos(ang)
    sn = jnp.sin(ang) * sgn_ref[...]
    lane = lax.broadcasted_iota(I32, (tm, LANES), 1)
    first = (lane % HEAD_DIM) < ROPE_HALF

    def mm(lo, n):
        return _dot(xn, w_ref[:, lo:lo + n])

    qf = mm(OFF_Q, NSA_WIDTH)
    for c in range(NSA_WIDTH // LANES):
        sl = slice(c * LANES, (c + 1) * LANES)
        q_ref[:, sl] = (_rope(qf[:, sl], cs, sn, first) * (HEAD_DIM ** -0.5 * LOG2E)).astype(BF16)

    kv = mm(OFF_KV, 6 * KV_WIDTH)
    k_c, v_c, k_s, v_s, k_w, v_w = [kv[:, n * KV_WIDTH:(n + 1) * KV_WIDTH] for n in range(6)]
    k_s = _rope(k_s, cs, sn, first)
    k_w = _rope(k_w, cs, sn, first)
    s_base = (pl.program_id(0) % tiles_per_seq) * tm
    blk = (s_base + lax.broadcasted_iota(I32, (tm, LANES), 0)) // SLC_BLOCK
    onehot = jnp.where(lane - HEAD_DIM == blk, 1.0, 0.0)
    low = lane < HEAD_DIM
    vs_t = v_s.T
    vw_t = v_w.T
    tail = jnp.where(lax.broadcasted_iota(I32, (V_ROWS - HEAD_DIM, NSA_TILE), 0) == 0, 1.0, 0.0).astype(BF16)
    for g in range(KV_GROUPS):
        gs = slice(g * HEAD_DIM, (g + 1) * HEAD_DIM)
        kw_ref[0, g] = k_w[:, gs].astype(BF16)
        ks_g = k_s if g == 0 else pltpu.roll(k_s, HEAD_DIM, 1)
        ksa_ref[0, g] = jnp.where(low, ks_g, onehot).astype(BF16)
        for c in range(tm // NSA_TILE):
            cols = slice(c * NSA_TILE, (c + 1) * NSA_TILE)
            vst_ref[0, g, c, 0:HEAD_DIM, :] = vs_t[gs, cols].astype(BF16)
            vst_ref[0, g, c, HEAD_DIM:V_ROWS, :] = tail
            vwt_ref[0, g, c, 0:HEAD_DIM, :] = vw_t[gs, cols].astype(BF16)
            vwt_ref[0, g, c, HEAD_DIM:V_ROWS, :] = tail

    sk_ref[...] = k_c
    sv_ref[...] = v_c
    nchunk = tm // CMP_STRIDE
    for t in range(CMP_STRIDE):
        dst = slice(t * HEAD_DIM, (t + 1) * HEAD_DIM)
        kt = sk_ref[pl.ds(t, nchunk, stride=CMP_STRIDE), :].astype(BF16)
        vt = sv_ref[pl.ds(t, nchunk, stride=CMP_STRIDE), :].astype(BF16)
        for g in range(KV_GROUPS):
            gs = slice(g * HEAD_DIM, (g + 1) * HEAD_DIM)
            ck_ref[0, g, :, dst] = kt[:, gs]
            cv_ref[0, g, :, dst] = vt[:, gs]

    ga_ref[...] = mm(OFF_GA, NSA_WIDTH)
    z_ref[...] = mm(OFF_Z, SSD_WIDTH)
    xbc_ref[...] = mm(OFF_XBC, CONV_DIM)
    qx_ref[...] = (mm(OFF_QX, XA_WIDTH) * (XA_HEAD_DIM ** -0.5)).astype(BF16)
    gx_ref[...] = mm(OFF_GX, XA_WIDTH)
    sm = mm(OFF_SM, LANES)
    sm_ref[...] = sm
    smt_ref[0] = sm.T


def _in_proj(x2, pos3, g_in, w_p, invc, sgn, *, batch, seq):
    rows = batch * seq
    tm = ROW_TILE
    tps = seq // tm
    row = lambda n: pl.BlockSpec((tm, n), lambda r: (r, 0))
    full = lambda a: pl.BlockSpec(a.shape, lambda r: (0,) * a.ndim)
    grp = lambda n: pl.BlockSpec((1, KV_GROUPS, tm, n), lambda r: (r // tps, 0, r % tps, 0))
    grp_shape = lambda n: jax.ShapeDtypeStruct((batch, KV_GROUPS, seq, n), BF16)
    chunk = pl.BlockSpec((1, KV_GROUPS, tm // CMP_STRIDE, CMP_STRIDE * HEAD_DIM),
                         lambda r: (r // tps, 0, r % tps, 0))
    chunk_shape = jax.ShapeDtypeStruct((batch, KV_GROUPS, seq // CMP_STRIDE, CMP_STRIDE * HEAD_DIM), BF16)
    vtile = pl.BlockSpec((1, KV_GROUPS, tm // NSA_TILE, V_ROWS, NSA_TILE), lambda r: (r // tps, 0, r % tps, 0, 0))
    vtile_shape = jax.ShapeDtypeStruct((batch, KV_GROUPS, seq // NSA_TILE, V_ROWS, NSA_TILE), BF16)
    flat = lambda n, dt: jax.ShapeDtypeStruct((rows, n), dt)
    return pl.pallas_call(
        functools.partial(_inproj_body, tm=tm, tiles_per_seq=tps),
        grid=(rows // tm,),
        in_specs=[row(D_MODEL), pl.BlockSpec((1, 1, tm), lambda r: (r, 0, 0)),
                  full(g_in), full(w_p), full(invc), full(sgn)],
        out_specs=[row(NSA_WIDTH), chunk, chunk, grp(LANES), vtile, grp(HEAD_DIM), vtile,
                   row(NSA_WIDTH), row(SSD_WIDTH), row(CONV_DIM),
                   row(XA_WIDTH), row(XA_WIDTH), row(LANES),
                   pl.BlockSpec((1, LANES, tm), lambda r: (r, 0, 0))],
        out_shape=[flat(NSA_WIDTH, BF16), chunk_shape, chunk_shape, grp_shape(LANES), vtile_shape,
                   grp_shape(HEAD_DIM), vtile_shape,
                   flat(NSA_WIDTH, F32), flat(SSD_WIDTH, F32), flat(CONV_DIM, F32),
                   flat(XA_WIDTH, BF16), flat(XA_WIDTH, F32), flat(LANES, F32),
                   jax.ShapeDtypeStruct((rows // tm, LANES, tm), F32)],
        scratch_shapes=[pltpu.VMEM((tm, KV_WIDTH), F32), pltpu.VMEM((tm, KV_WIDTH), F32)],
        compiler_params=pltpu.CompilerParams(dimension_semantics=("parallel",),
                                             vmem_limit_bytes=VMEM_LIMIT),
        name="in_proj",
    )(x2, pos3, g_in, w_p, invc, sgn)


def _compress_body(ck_ref, cv_ref, posc_ref, w1k_ref, w2k_ref, pk_ref, w1v_ref, w2v_ref, pv_ref,
                   invl_ref, sgn_ref, kc_ref, vc_ref, *, ncp):
    half = CMP_STRIDE * HEAD_DIM

    def mlp(c_ref, w1_ref, w2_ref, p_ref):
        c = c_ref[0, 0]
        a = _dot(c, w1_ref[0:half, :])
        b = _dot(c, w1_ref[half:2 * half, :])
        bias = _dot(p_ref[...], w1_ref[...])[0:1, :]
        h = a + pltpu.roll(b, ncp - 1, 0) + bias
        return _dot(_silu(h).astype(BF16), w2_ref[...])

    kc = mlp(ck_ref, w1k_ref, w2k_ref, pk_ref)
    vc = mlp(cv_ref, w1v_ref, w2v_ref, pv_ref)
    cs, sn = _rope_tables(posc_ref[0].astype(F32), invl_ref[...], sgn_ref[...])
    lane = lax.broadcasted_iota(I32, (ncp, LANES), 1)
    kc = _rope(kc, cs, sn, (lane % HEAD_DIM) < ROPE_HALF)
    kc_ref[0, 0] = kc[:, 0:HEAD_DIM].astype(BF16)
    vc_ref[0, 0] = vc[:, 0:HEAD_DIM].astype(BF16)


def _compress(ck, cv, posc, w1k, w2k, pk, w1v, w2v, pv, invl, sgn, *, batch, ncp):
    chunk = pl.BlockSpec((1, 1, ncp, CMP_STRIDE * HEAD_DIM), lambda b, g: (b, g, 0, 0))
    full = lambda a: pl.BlockSpec(a.shape, lambda b, g: (0,) * a.ndim)
    out = pl.BlockSpec((1, 1, ncp, HEAD_DIM), lambda b, g: (b, g, 0, 0))
    shp = jax.ShapeDtypeStruct((batch, KV_GROUPS, ncp, HEAD_DIM), BF16)
    return pl.pallas_call(
        functools.partial(_compress_body, ncp=ncp),
        grid=(batch, KV_GROUPS),
        in_specs=[chunk, chunk, pl.BlockSpec((1, ncp, 1), lambda b, g: (b, 0, 0)),
                  full(w1k), full(w2k), full(pk), full(w1v), full(w2v), full(pv),
                  full(invl), full(sgn)],
        out_specs=[out, out],
        out_shape=[shp, shp],
        compiler_params=pltpu.CompilerParams(dimension_semantics=("parallel", "parallel"),
                                             vmem_limit_bytes=VMEM_LIMIT),
        name="compress",
    )(ck, cv, posc, w1k, w2k, pk, w1v, w2v, pv, invl, sgn)


def _nsa_body(q_ref, kc_ref, cat_ref, ksa_ref, vst_ref, kw_ref, vwt_ref, ga_ref, sm_ref,
              o_ref, qa_ref, m_ref, acc_ref, oc_ref, sc_ref, mc_ref, *, tq, ncp, topk):
    g = pl.program_id(1)
    i = pl.program_id(2)
    s0 = i * tq
    nh = HEADS_PER_GROUP
    rows = nh * tq

    def col_max(s):
        n = s.shape[0]
        while n > 8:
            n //= 2
            s = jnp.maximum(s[0:n], s[n:2 * n])
        return jnp.max(s, axis=0, keepdims=True)

    def reset(br):
        m_ref[br] = jnp.full((1, rows), NEG, F32)
        acc_ref[br] = jnp.zeros((V_ROWS, rows), F32)

    def consume(br, s, vt_tile, mask, s_max=None):
        if mask is not None:
            s = jnp.where(mask, s, NEG)
        if s_max is None:
            s_max = col_max(s)
        m_prev = m_ref[br]
        m_new = jnp.maximum(m_prev, s_max)
        alpha = jnp.exp2(m_prev - m_new)
        p = jnp.exp2(s - m_new).astype(BF16)
        acc_ref[br] = alpha * acc_ref[br] + _dot(vt_tile, p)
        m_ref[br] = m_new

    def sel_scores(j):
        rows_j = pl.ds(pl.multiple_of(j * tq, tq), tq)
        return _nt(ksa_ref[0, 0, rows_j, :], qa_ref[...])

    def win_scores(j):
        rows_j = pl.ds(pl.multiple_of(j * tq, tq), tq)
        return _nt(kw_ref[0, 0, rows_j, :], qa_ref[:, 0:HEAD_DIM])

    q4 = q_ref[...]
    for e in range(nh):
        qa_ref[e * tq:(e + 1) * tq, 0:HEAD_DIM] = q4[:, e * HEAD_DIM:(e + 1) * HEAD_DIM]

    n_io = lax.broadcasted_iota(I32, (ncp, rows), 0)
    t_io = s0 + lax.broadcasted_iota(I32, (ncp, rows), 1) % tq
    k_io = lax.broadcasted_iota(I32, (tq, rows), 0)
    t_io2 = lax.broadcasted_iota(I32, (tq, rows), 1) % tq
    diag = k_io <= t_io2
    j1 = jnp.maximum(i - 1, 0)
    j2 = jnp.maximum(i - 2, 0)
    s = jnp.where((n_io * CMP_STRIDE + (CMP_LEN - 1)) <= t_io, _nt(kc_ref[0, 0], qa_ref[:, 0:HEAD_DIM]), NEG)
    sc_ref[0, 0:ncp, :] = s
    mc_ref[0] = col_max(s)
    sc_ref[1] = win_scores(i)
    p = jnp.exp2(sc_ref[0, 0:ncp, :] - mc_ref[0]).astype(BF16)
    big = _dot(cat_ref[0, 0], p)
    sc_ref[0] = win_scores(j1)
    t_row = s0 + lax.broadcasted_iota(I32, (1, rows), 1) % tq
    inv = jnp.where(t_row >= CMP_LEN - 1, 1.0 / jnp.maximum(big[HEAD_DIM:HEAD_DIM + 1, :], 1e-30), 0.0)
    oc_ref[...] = big[0:HEAD_DIM, :] * inv
    imp_x = big[CAT_OVL_ROW0:CAT_OVL_ROW0 + SLC_SLOTS, :] * inv
    imp = imp_x[:, 0:tq]
    for e in range(1, nh):
        imp = imp + imp_x[:, e * tq:(e + 1) * tq]

    reset(1)
    consume(1, sc_ref[1], vwt_ref[0, 0, i], diag)
    sc_ref[1] = win_scores(j2)
    consume(1, sc_ref[0], vwt_ref[0, 0, j1], i >= 1)
    consume(1, sc_ref[1], vwt_ref[0, 0, j2], (k_io > t_io2) & (i >= 2))

    j_io = lax.broadcasted_iota(I32, (SLC_SLOTS, tq), 0)
    cur = (s0 + lax.broadcasted_iota(I32, (SLC_SLOTS, tq), 1)) // SLC_BLOCK
    forced = (j_io == 0) | (j_io == cur) | (j_io == cur - 1)
    valid = j_io <= cur
    v = jnp.where(forced, BIG, jnp.where(valid, imp, -BIG))
    sel = jnp.zeros((SLC_SLOTS, tq), F32)
    for _ in range(topk):
        mx = jnp.max(v, axis=0, keepdims=True)
        idx = jnp.min(jnp.where(v == mx, j_io, SLC_SLOTS), axis=0, keepdims=True)
        hit = j_io == idx
        sel = jnp.where(hit, 1.0, sel)
        v = jnp.where(hit, -3e38, v)
    selb = jnp.where((sel > 0.5) & valid, 0.0, NEG)
    selb_t = jnp.concatenate([jnp.zeros((SLC_SLOTS, tq), F32), selb], axis=0).T
    selb_t = selb_t.astype(BF16)
    for e in range(nh):
        qa_ref[e * tq:(e + 1) * tq, HEAD_DIM:LANES] = selb_t[:, HEAD_DIM:LANES]

    def produce(slot, j):
        s = sel_scores(j)
        sc_ref[slot] = s
        mc_ref[slot] = col_max(s)

    reset(0)
    produce(0, 0)

    def sel_pair(j):
        produce(1, j + 1)
        consume(0, sc_ref[0], vst_ref[0, 0, j], None, mc_ref[0])
        produce(0, j + 2)
        consume(0, sc_ref[1], vst_ref[0, 0, j + 1], None, mc_ref[1])

    def sel_quad(jj, carry):
        sel_pair(4 * jj)
        sel_pair(4 * jj + 2)
        return carry

    def sel_rest(jj, carry):
        sel_pair((i // 4) * 4 + 2 * jj)
        return carry

    lax.fori_loop(0, i // 4, sel_quad, 0)
    lax.fori_loop(0, (i % 4) // 2, sel_rest, 0)

    @pl.when(i % 2 == 0)
    def _():
        consume(0, sc_ref[0], vst_ref[0, 0, i], diag)

    @pl.when(i % 2 == 1)
    def _():
        sc_ref[1] = sel_scores(i)
        consume(0, sc_ref[0], vst_ref[0, 0, i - 1], None, mc_ref[0])
        consume(0, sc_ref[1], vst_ref[0, 0, i], diag)

    gates_t = jax.nn.sigmoid(sm_ref[...]).T
    ga = ga_ref[...]

    def head_out(e):
        cs = slice(e * tq, (e + 1) * tq)

        def gate(c):
            c0 = GATE_LANE0 + e * 3 + c
            c1 = c0 + nh * 3
            return jnp.where(g == 0, gates_t[c0:c0 + 1, :], gates_t[c1:c1 + 1, :])

        o_s = acc_ref[0, 0:HEAD_DIM, cs] / acc_ref[0, HEAD_DIM:HEAD_DIM + 1, cs]
        o_w = acc_ref[1, 0:HEAD_DIM, cs] / acc_ref[1, HEAD_DIM:HEAD_DIM + 1, cs]
        return gate(0) * oc_ref[:, cs] + gate(1) * o_s + gate(2) * o_w

    for pr in range(nh // 2):
        ls = slice(pr * LANES, (pr + 1) * LANES)
        pair = jnp.concatenate([head_out(2 * pr), head_out(2 * pr + 1)], axis=0).T
        o_ref[:, ls] = (pair * _silu(ga[:, ls])).astype(BF16)


def _nsa(q, kc, cat, ksa, vst, kw, vwt, ga, sm, *, batch, seq, ncp, topk):
    tq = NSA_TILE
    assert WINDOW == 2 * tq
    nq = seq // tq
    gw = HEADS_PER_GROUP * HEAD_DIM
    rows = HEADS_PER_GROUP * tq
    per_bg = lambda *shape: pl.BlockSpec((1, 1) + shape, lambda b, g, i: (b, g) + (0,) * len(shape))
    return pl.pallas_call(
        functools.partial(_nsa_body, tq=tq, ncp=ncp, topk=topk),
        grid=(batch, KV_GROUPS, nq),
        in_specs=[pl.BlockSpec((tq, gw), lambda b, g, i: (b * nq + i, g)),
                  per_bg(ncp, HEAD_DIM), per_bg(CAT_ROWS, ncp),
                  per_bg(seq, LANES), per_bg(nq, V_ROWS, tq), per_bg(seq, HEAD_DIM),
                  per_bg(nq, V_ROWS, tq),
                  pl.BlockSpec((tq, gw), lambda b, g, i: (b * nq + i, g)),
                  pl.BlockSpec((tq, LANES), lambda b, g, i: (b * nq + i, 0))],
        out_specs=pl.BlockSpec((tq, gw), lambda b, g, i: (b * nq + i, g)),
        out_shape=jax.ShapeDtypeStruct((batch * seq, NSA_WIDTH), BF16),
        scratch_shapes=[pltpu.VMEM((rows, LANES), BF16),
                        pltpu.VMEM((2, 1, rows), F32),
                        pltpu.VMEM((2, V_ROWS, rows), F32),
                        pltpu.VMEM((HEAD_DIM, rows), F32),
                        pltpu.VMEM((2, tq, rows), F32), pltpu.VMEM((2, 1, rows), F32)],
        compiler_params=pltpu.CompilerParams(
            dimension_semantics=("parallel", "parallel", "arbitrary"),
            vmem_limit_bytes=VMEM_LIMIT),
        name="nsa",
    )(q, kc, cat, ksa, vst, kw, vwt, ga, sm)


def _ssdout_body(x_ref, oa_ref, qx_ref, gx_ref, k_ref, v_ref, w_ref, g_ref,
                 xbc_ref, z_ref, sm_ref, dtt_ref, cw_ref, cb_ref, dtb_ref, dtbt_ref, al_ref, alt_ref,
                 dsk_ref, gn_ref, eh_ref, ehw_ref, tril_ref, triu_ref,
                 o_ref, acc_ref, mix_ref, ext_ref, xc_ref, y_ref, h_ref, *, ts, tiles_per_seq):
    L = SSD_CHUNK
    N = SSD_STATE
    P = SSD_HEAD_DIM
    E = SSD_HEADS // SSD_GROUPS
    gw = E * P
    pad = 8
    w_b = NSA_WIDTH
    w_c = NSA_WIDTH + SSD_WIDTH

    @pl.when(pl.program_id(0) % tiles_per_seq == 0)
    def _():
        ext_ref[0:pad, :] = jnp.zeros((pad, CONV_DIM), F32)
        h_ref[...] = jnp.zeros(h_ref.shape, F32)

    acc_ref[...] = x_ref[...] + _dot(oa_ref[...], w_ref[0:w_b, :])

    ext_ref[pad:pad + ts, :] = xbc_ref[...]
    ext = ext_ref[...]
    taps = ext * cw_ref[0:1, :]
    for k in range(1, SSD_CONV):
        taps = ext * cw_ref[k:k + 1, :] + pltpu.roll(taps, 1, 0)
    xc_ref[...] = _silu(cb_ref[...] + taps[pad:pad + ts, :])
    ext_ref[0:pad, :] = ext_ref[ts:ts + pad, :]

    a_row = -jnp.exp(al_ref[...])
    a_col = -jnp.exp(alt_ref[...])
    causal = lax.broadcasted_iota(I32, (L, L), 1) <= lax.broadcasted_iota(I32, (L, L), 0)
    low = lax.broadcasted_iota(I32, (ts, LANES), 1) < P
    nck = ts // L
    chunk = lambda c: slice(c * L, (c + 1) * L)

    xs = xc_ref[:, 0:SSD_WIDTH]
    dt = jax.nn.softplus(sm_ref[...] + dtb_ref[...])
    da = dt * a_row
    a_cs = jnp.concatenate([_dot2_r(tril_ref[...], da[chunk(c), :]) for c in range(nck)], axis=0)
    dt_x = _dot2_l(dt, eh_ref[...])
    acs_w = _dot2_l(a_cs, ehw_ref[...])
    acs_x = jnp.concatenate(
        [jnp.where(low, acs_w[:, (2 * n) * LANES:(2 * n + 1) * LANES],
                   acs_w[:, (2 * n + 1) * LANES:(2 * n + 2) * LANES]) for n in range(SSD_HEADS // 2)],
        axis=1)
    a_last = [acs_x[c * L + L - 1:c * L + L, :] for c in range(nck)]
    a_last_x = jnp.concatenate([jnp.broadcast_to(a, (L, SSD_WIDTH)) for a in a_last], axis=0)
    dat = jax.nn.softplus(dtt_ref[0, DT_LANE0:DT_LANE0 + SSD_HEADS, :] + dtbt_ref[...]) * a_col
    acs_t = [_dot2_l(dat[:, chunk(c)], triu_ref[...]) for c in range(nck)]
    xdt = xs * dt_x
    xdo = (xdt * jnp.exp(a_last_x - acs_x)).astype(BF16)
    xdt_b = xdt.astype(BF16)
    y_ref[...] = xs * dsk_ref[...]
    pre = jnp.exp(acs_x)

    def attn_head(h):
        hs = slice(h * XA_HEAD_DIM, (h + 1) * XA_HEAD_DIM)
        s = _nt(qx_ref[:, hs], k_ref[0, :, hs])
        s = s - jnp.max(s, axis=-1, keepdims=True)
        p = jnp.exp(s)
        p = p / jnp.sum(p, axis=-1, keepdims=True)
        oc = _dot(p.astype(BF16), v_ref[0, :, hs]) * _silu(gx_ref[:, hs])
        mix_ref[:, SSD_WIDTH + h * XA_HEAD_DIM:SSD_WIDTH + (h + 1) * XA_HEAD_DIM] = oc.astype(BF16)

    for gi in range(SSD_GROUPS):
        gs = slice(gi * gw, (gi + 1) * gw)
        bm = xc_ref[:, SSD_WIDTH + gi * N:SSD_WIDTH + (gi + 1) * N]
        cm_b = xc_ref[:, SSD_WIDTH + SSD_GROUPS * N + gi * N:SSD_WIDTH + SSD_GROUPS * N + (gi + 1) * N].astype(BF16)
        bm_b = bm.astype(BF16)
        h_c = h_ref[gi]
        h_in = []
        for c in range(nck):
            h_in.append(h_c.astype(BF16))
            h_c = h_c * jnp.exp(a_last[c][:, gs]) + _dot(bm[chunk(c), :].T.astype(BF16), xdo[chunk(c), gs])
        h_ref[gi] = h_c
        for c in range(nck):
            rs = chunk(c)
            cbm = _nt(cm_b[rs, :], bm_b[rs, :])
            y_ref[rs, gs] += _dot(cm_b[rs, :], h_in[c]) * pre[rs, gs]
            for e in range(E):
                h = gi * E + e
                hs = slice(h * P, (h + 1) * P)
                d = acs_w[rs, h * LANES:(h + 1) * LANES] - acs_t[c][h:h + 1, :]
                dec = jnp.exp(jnp.where(causal, d, NEG))
                y_ref[rs, hs] += _dot((cbm * dec).astype(BF16), xdt_b[rs, hs])

    y = y_ref[...] * _silu(z_ref[...])
    ms = jnp.mean(y * y, axis=-1, keepdims=True)
    mix_ref[:, 0:SSD_WIDTH] = (y * lax.rsqrt(ms + EPS) * gn_ref[...]).astype(BF16)
    acc_ref[...] += _dot(mix_ref[:, 0:SSD_WIDTH], w_ref[w_b:w_c, :])

    for h in range(XA_HEADS):
        attn_head(h)
    acc = acc_ref[...] + _dot(mix_ref[:, SSD_WIDTH:], w_ref[w_c:, :])
    ms = jnp.mean(acc * acc, axis=-1, keepdims=True)
    o_ref[...] = acc * lax.rsqrt(ms + EPS) * g_ref[...]


def _ssd_out(x2, oa, qx, gx, km, vm, w_out, g_final, xbc, z, sm, smt, consts, *, batch, seq):
    rows = batch * seq
    ts = ROW_TILE
    tps = seq // ts
    mlen = km.shape[1]
    row = lambda n: pl.BlockSpec((ts, n), lambda r: (r, 0))
    full = lambda a: pl.BlockSpec(a.shape, lambda r: (0,) * a.ndim)
    mem = pl.BlockSpec((1, mlen, XA_WIDTH), lambda r: (r // tps, 0, 0))
    return pl.pallas_call(
        functools.partial(_ssdout_body, ts=ts, tiles_per_seq=tps),
        grid=(rows // ts,),
        in_specs=[row(D_MODEL), row(NSA_WIDTH), row(XA_WIDTH), row(XA_WIDTH), mem, mem,
                  full(w_out), full(g_final),
                  row(CONV_DIM), row(SSD_WIDTH), row(LANES), pl.BlockSpec((1, LANES, ts), lambda r: (r, 0, 0))]
                 + [full(a) for a in consts],
        out_specs=row(D_MODEL),
        out_shape=jax.ShapeDtypeStruct((rows, D_MODEL), F32),
        scratch_shapes=[pltpu.VMEM((ts, D_MODEL), F32), pltpu.VMEM((ts, SSD_WIDTH + XA_WIDTH), BF16),
                        pltpu.VMEM((ts + 8, CONV_DIM), F32), pltpu.VMEM((ts, CONV_DIM), F32),
                        pltpu.VMEM((ts, SSD_WIDTH), F32),
                        pltpu.VMEM((SSD_GROUPS, SSD_STATE, SSD_WIDTH // SSD_GROUPS), F32)],
        compiler_params=pltpu.CompilerParams(dimension_semantics=("arbitrary",),
                                             vmem_limit_bytes=VMEM_LIMIT),
        name="ssd_out",
    )(x2, oa, qx, gx, km, vm, w_out, g_final, xbc, z, sm, smt, *consts)


def _memkv_body(mem_ref, g_ref, w_ref, k_ref, v_ref):
    x = mem_ref[0]
    ms = jnp.mean(x * x, axis=-1, keepdims=True)
    xn = (x * lax.rsqrt(ms + EPS) * g_ref[...]).astype(BF16)
    k_ref[0] = _dot(xn, w_ref[:, 0:XA_WIDTH]).astype(BF16)
    v_ref[0] = _dot(xn, w_ref[:, XA_WIDTH:2 * XA_WIDTH]).astype(BF16)


def _mem_kv(mem, g_mem, w_kv):
    batch, mlen, _ = mem.shape
    full = lambda a: pl.BlockSpec(a.shape, lambda b: (0,) * a.ndim)
    out = pl.BlockSpec((1, mlen, XA_WIDTH), lambda b: (b, 0, 0))
    shp = jax.ShapeDtypeStruct((batch, mlen, XA_WIDTH), BF16)
    return pl.pallas_call(
        _memkv_body,
        grid=(batch,),
        in_specs=[pl.BlockSpec((1, mlen, D_MODEL), lambda b: (b, 0, 0)), full(g_mem), full(w_kv)],
        out_specs=[out, out],
        out_shape=[shp, shp],
        compiler_params=pltpu.CompilerParams(dimension_semantics=("parallel",),
                                             vmem_limit_bytes=VMEM_LIMIT),
        name="mem_kv",
    )(mem, g_mem, w_kv)


def _permute_w_in(w):
    sizes = [NSA_WIDTH] + [KV_WIDTH] * 6 + [NSA_HEADS * 3, NSA_WIDTH, SSD_WIDTH, CONV_DIM, SSD_HEADS,
                                          XA_WIDTH, XA_WIDTH]
    offs = [0]
    for s in sizes:
        offs.append(offs[-1] + s)
    sec = lambda n: w[:, offs[n]:offs[n + 1]]
    small = jnp.concatenate(
        [sec(7), sec(11), jnp.zeros((w.shape[0], LANES - NSA_HEADS * 3 - SSD_HEADS), w.dtype)], axis=1)
    cols = [sec(0)] + [sec(n) for n in range(1, 7)] + [sec(8), sec(9), sec(10), sec(12), sec(13), small]
    return jnp.concatenate(cols, axis=1).astype(BF16)


def _lane_row(vals, lane0):
    return jnp.zeros((1, LANES), F32).at[0, lane0:lane0 + vals.shape[0]].set(vals.astype(F32))


def _forward(x, mem, positions, g_in, w_in, cmp_pos_k, w_cmp1_k, w_cmp2_k, cmp_pos_v, w_cmp1_v,
             w_cmp2_v, conv_w, conv_b, dt_bias, a_log, d_skip, g_ssd_norm, g_mem, w_mem_kv, w_out,
             g_final):
    batch, seq, _ = x.shape
    ncp = seq // CMP_STRIDE
    n_slc = seq // SLC_BLOCK
    assert n_slc <= SLC_SLOTS and seq % ROW_TILE == 0 and ncp <= NSA_TILE
    topk = min(SLC_TOPK, n_slc)
    rows = batch * seq

    inv = ROPE_THETA ** (-jnp.arange(0, ROPE_DIM, 2, dtype=F32) / ROPE_DIM)
    head_inv = jnp.concatenate([inv, inv, jnp.zeros((HEAD_DIM - ROPE_DIM,), F32)])
    invl = jnp.tile(head_inv, LANES // HEAD_DIM)[None, :]
    head_sgn = jnp.concatenate([-jnp.ones((ROPE_HALF,), F32), jnp.ones((ROPE_HALF,), F32),
                                jnp.zeros((HEAD_DIM - ROPE_DIM,), F32)])
    sgn = jnp.tile(head_sgn, LANES // HEAD_DIM)[None, :]

    x2 = x.reshape(rows, D_MODEL)
    pos3 = positions.reshape(rows // ROW_TILE, 1, ROW_TILE)
    h = x2
    for l in range(g_in.shape[0]):
        (q, ck, cv, ksa, vst, kw, vwt, ga, z, xbc, qx, gx, sm, smt) = _in_proj(
            h, pos3, g_in[l][None, :], _permute_w_in(w_in[l]), invl.T, sgn, batch=batch, seq=seq)

        cmp_end = jnp.minimum(jnp.arange(ncp) * CMP_STRIDE + CMP_LEN - 1, seq - 1)
        posc = positions[:, cmp_end][:, :, None]
        pad_w2 = lambda w: jnp.pad(w, ((0, 0), (0, LANES - HEAD_DIM))).astype(BF16)
        pos_rows = lambda p: jnp.broadcast_to(p.reshape(1, CMP_LEN * HEAD_DIM), (8, CMP_LEN * HEAD_DIM)).astype(BF16)
        kc, vc = _compress(ck, cv, posc,
                           w_cmp1_k[l].astype(BF16), pad_w2(w_cmp2_k[l]), pos_rows(cmp_pos_k[l]),
                           w_cmp1_v[l].astype(BF16), pad_w2(w_cmp2_v[l]), pos_rows(cmp_pos_v[l]),
                           invl, sgn, batch=batch, ncp=ncp)
        n_ix = jnp.arange(ncp)[None, :]
        j_ix = jnp.arange(SLC_SLOTS)[:, None]
        ovl = ((n_ix * CMP_STRIDE < j_ix * SLC_BLOCK + SLC_BLOCK)
               & (n_ix * CMP_STRIDE + CMP_LEN > j_ix * SLC_BLOCK)
               & (n_ix < ncp - (CMP_LEN // CMP_STRIDE - 1))).astype(BF16)
        bg = (batch, KV_GROUPS)
        cat = jnp.concatenate(
            [jnp.swapaxes(vc, 2, 3), jnp.ones(bg + (1, ncp), BF16),
             jnp.zeros(bg + (CAT_OVL_ROW0 - HEAD_DIM - 1, ncp), BF16),
             jnp.broadcast_to(ovl, bg + ovl.shape),
             jnp.zeros(bg + (CAT_ROWS - CAT_OVL_ROW0 - SLC_SLOTS, ncp), BF16)], axis=2)
        o_a = _nsa(q, kc, cat, ksa, vst, kw, vwt, ga, sm,
                   batch=batch, seq=seq, ncp=ncp, topk=topk)

        head_of_lane = jnp.arange(SSD_WIDTH) // SSD_HEAD_DIM
        k_ix = jnp.arange(LANES)[:, None]
        eh = (k_ix == DT_LANE0 + head_of_lane[None, :]).astype(BF16)
        ehw = (k_ix == DT_LANE0 + (jnp.arange(SSD_HEADS * LANES) // LANES)[None, :]).astype(BF16)
        t_ix = jnp.arange(SSD_CHUNK)
        tril = (t_ix[None, :] <= t_ix[:, None]).astype(BF16)
        ssd_consts = [conv_w[l], conv_b[l][None, :],
                      _lane_row(dt_bias[l], DT_LANE0), dt_bias[l].astype(F32)[:, None],
                      _lane_row(a_log[l], DT_LANE0), a_log[l].astype(F32)[:, None],
                      jnp.repeat(d_skip[l].astype(F32), SSD_HEAD_DIM)[None, :], g_ssd_norm[l][None, :],
                      eh, ehw, tril, tril.T]

        km, vm = _mem_kv(mem, g_mem[l][None, :], w_mem_kv[l].astype(BF16))
        assert g_in.shape[0] == 1, "the final RMSNorm is fused into the (single) layer's last kernel"
        h = _ssd_out(h, o_a, qx, gx, km, vm, w_out[l].astype(BF16), g_final[None, :],
                     xbc, z, sm, smt, ssd_consts, batch=batch, seq=seq)
    return h.reshape(batch, seq, D_MODEL)


def kernel(x, mem, positions, g_in, w_in, cmp_pos_k, w_cmp1_k, w_cmp2_k, cmp_pos_v, w_cmp1_v, w_cmp2_v,
           conv_w, conv_b, dt_bias, a_log, d_skip, g_ssd_norm, g_mem, w_mem_kv, w_out, g_final):
    return _forward(x, mem, positions, g_in, w_in, cmp_pos_k, w_cmp1_k, w_cmp2_k, cmp_pos_v, w_cmp1_v,
                    w_cmp2_v, conv_w, conv_b, dt_bias, a_log, d_skip, g_ssd_norm, g_mem, w_mem_kv,
                    w_out, g_final)
```

```python
import functools

import jax
import jax.numpy as jnp
from jax import lax
from jax.experimental import pallas as pl
from jax.experimental.pallas import tpu as pltpu

F32 = jnp.float32
BF16 = jnp.bfloat16
I32 = jnp.int32

D_MODEL = 1024
NSA_HEADS = 8
HEAD_DIM = 64
KV_GROUPS = 2
HEADS_PER_GROUP = NSA_HEADS // KV_GROUPS
NSA_WIDTH = NSA_HEADS * HEAD_DIM
KV_WIDTH = KV_GROUPS * HEAD_DIM
CMP_LEN = 32
CMP_STRIDE = 16
CMP_HIDDEN = 256
SLC_BLOCK = 64
SLC_TOPK = 16
SLC_SLOTS = 64
WINDOW = 512
BIG = 1e9
NEG = -1e30

SSD_HEADS = 8
SSD_HEAD_DIM = 64
SSD_WIDTH = SSD_HEADS * SSD_HEAD_DIM
SSD_GROUPS = 2
SSD_STATE = 128
SSD_CONV = 4
SSD_CHUNK = 128
CONV_DIM = SSD_WIDTH + 2 * SSD_GROUPS * SSD_STATE

XA_HEADS = 4
XA_HEAD_DIM = 128
XA_WIDTH = XA_HEADS * XA_HEAD_DIM
MIX_WIDTH = NSA_WIDTH + SSD_WIDTH + XA_WIDTH

ROPE_THETA = 500000.0
ROPE_DIM = HEAD_DIM // 4
ROPE_HALF = ROPE_DIM // 2
ROPE_FREQ_ROWS = 16
EPS = 1e-6

LANES = 128
GATE_LANE0 = 0
DT_LANE0 = 24

OFF_Q = 0
OFF_KV = OFF_Q + NSA_WIDTH
OFF_GA = OFF_KV + 6 * KV_WIDTH
OFF_Z = OFF_GA + NSA_WIDTH
OFF_XBC = OFF_Z + SSD_WIDTH
OFF_QX = OFF_XBC + CONV_DIM
OFF_GX = OFF_QX + XA_WIDTH
OFF_SM = OFF_GX + XA_WIDTH
N_PROJ = OFF_SM + LANES

ROW_TILE = 512
NSA_TILE = 256
V_ROWS = 80
LOG2E = 1.4426950408889634
CAT_OVL_ROW0 = 80
CAT_ROWS = 144
VMEM_LIMIT = 56 * 1024 * 1024


def _nt(a, b):
    return lax.dot_general(a, b, (((1,), (1,)), ((), ())), preferred_element_type=F32)


def _dot(a, b):
    return jnp.dot(a, b, preferred_element_type=F32)


def _split2(x):
    hi = x.astype(BF16)
    return hi, (x - hi.astype(F32)).astype(BF16)


def _dot2_l(x, w):
    hi, lo = _split2(x)
    return _dot(hi, w) + _dot(lo, w)


def _dot2_r(w, x):
    hi, lo = _split2(x)
    return _dot(w, hi) + _dot(w, lo)


def _silu(x):
    h = 0.5 * x
    return h + h * jnp.tanh(h)


def _rope(a, cs, sn, first):
    r = jnp.where(first, pltpu.roll(a, LANES - ROPE_HALF, 1), pltpu.roll(a, ROPE_HALF, 1))
    return a * cs + r * sn


def _rope_tables(pos_f32, invl, sgn):
    ang = pos_f32 * invl
    return jnp.cos(ang), jnp.sin(ang) * sgn


def _inproj_body(x_ref, pos_ref, g_ref, w_ref, invc_ref, ec_ref, es_ref,
                 q_ref, ck_ref, cv_ref, ksa_ref, vst_ref, kw_ref, vwt_ref,
                 ga_ref, z_ref, xbc_ref, qx_ref, gx_ref, sm_ref, smt_ref, sk_ref, sv_ref, *, tm, tiles_per_seq):
    x = x_ref[...]
    ms = jnp.mean(x * x, axis=-1, keepdims=True)
    xn = (x * lax.rsqrt(ms + EPS) * g_ref[...]).astype(BF16)

    def mm(lo, n):
        return _dot(xn, w_ref[:, lo:lo + n])

    ga_ref[...] = mm(OFF_GA, NSA_WIDTH)
    z_ref[...] = mm(OFF_Z, SSD_WIDTH)

    ang = invc_ref[...] * pos_ref[0].astype(F32)
    cs = _dot2_r(ec_ref[...], jnp.cos(ang)).T
    sn = _dot2_r(es_ref[...], jnp.sin(ang)).T
    lane = lax.broadcasted_iota(I32, (tm, LANES), 1)
    first = (lane % HEAD_DIM) < ROPE_HALF

    qf = mm(OFF_Q, NSA_WIDTH)
    for c in range(NSA_WIDTH // LANES):
        sl = slice(c * LANES, (c + 1) * LANES)
        q_ref[:, sl] = (_rope(qf[:, sl], cs, sn, first) * (HEAD_DIM ** -0.5 * LOG2E)).astype(BF16)

    kv = mm(OFF_KV, 6 * KV_WIDTH)
    k_c, v_c, k_s, v_s, k_w, v_w = [kv[:, n * KV_WIDTH:(n + 1) * KV_WIDTH] for n in range(6)]
    k_s = _rope(k_s, cs, sn, first)
    k_w = _rope(k_w, cs, sn, first)
    s_base = (pl.program_id(0) % tiles_per_seq) * tm
    blk = (s_base + lax.broadcasted_iota(I32, (tm, LANES), 0)) // SLC_BLOCK
    onehot = jnp.where(lane - HEAD_DIM == blk, 1.0, 0.0)
    low = lane < HEAD_DIM
    vs_t = v_s.T
    vw_t = v_w.T
    tail = jnp.where(lax.broadcasted_iota(I32, (V_ROWS - HEAD_DIM, NSA_TILE), 0) == 0, 1.0, 0.0).astype(BF16)
    for g in range(KV_GROUPS):
        gs = slice(g * HEAD_DIM, (g + 1) * HEAD_DIM)
        kw_ref[0, g] = k_w[:, gs].astype(BF16)
        ks_g = k_s if g == 0 else pltpu.roll(k_s, HEAD_DIM, 1)
        ksa_ref[0, g] = jnp.where(low, ks_g, onehot).astype(BF16)
        for c in range(tm // NSA_TILE):
            cols = slice(c * NSA_TILE, (c + 1) * NSA_TILE)
            vst_ref[0, g, c, 0:HEAD_DIM, :] = vs_t[gs, cols].astype(BF16)
            vst_ref[0, g, c, HEAD_DIM:V_ROWS, :] = tail
            vwt_ref[0, g, c, 0:HEAD_DIM, :] = vw_t[gs, cols].astype(BF16)
            vwt_ref[0, g, c, HEAD_DIM:V_ROWS, :] = tail

    sk_ref[...] = k_c
    sv_ref[...] = v_c
    nchunk = tm // CMP_STRIDE
    for t in range(CMP_STRIDE):
        dst = slice(t * HEAD_DIM, (t + 1) * HEAD_DIM)
        kt = sk_ref[pl.ds(t, nchunk, stride=CMP_STRIDE), :].astype(BF16)
        vt = sv_ref[pl.ds(t, nchunk, stride=CMP_STRIDE), :].astype(BF16)
        for g in range(KV_GROUPS):
            gs = slice(g * HEAD_DIM, (g + 1) * HEAD_DIM)
            ck_ref[0, g, :, dst] = kt[:, gs]
            cv_ref[0, g, :, dst] = vt[:, gs]

    xbc_ref[...] = mm(OFF_XBC, CONV_DIM)
    qx_ref[...] = (mm(OFF_QX, XA_WIDTH) * (XA_HEAD_DIM ** -0.5)).astype(BF16)
    gx_ref[...] = mm(OFF_GX, XA_WIDTH)
    sm = mm(OFF_SM, LANES)
    sm_ref[...] = sm
    smt_ref[0] = sm.T


def _in_proj(x2, pos3, g_in, w_p, invc, ec, es, *, batch, seq):
    rows = batch * seq
    tm = ROW_TILE
    tps = seq // tm
    row = lambda n: pl.BlockSpec((tm, n), lambda r: (r, 0))
    full = lambda a: pl.BlockSpec(a.shape, lambda r: (0,) * a.ndim)
    grp = lambda n: pl.BlockSpec((1, KV_GROUPS, tm, n), lambda r: (r // tps, 0, r % tps, 0))
    grp_shape = lambda n: jax.ShapeDtypeStruct((batch, KV_GROUPS, seq, n), BF16)
    chunk = pl.BlockSpec((1, KV_GROUPS, tm // CMP_STRIDE, CMP_STRIDE * HEAD_DIM),
                         lambda r: (r // tps, 0, r % tps, 0))
    chunk_shape = jax.ShapeDtypeStruct((batch, KV_GROUPS, seq // CMP_STRIDE, CMP_STRIDE * HEAD_DIM), BF16)
    vtile = pl.BlockSpec((1, KV_GROUPS, tm // NSA_TILE, V_ROWS, NSA_TILE), lambda r: (r // tps, 0, r % tps, 0, 0))
    vtile_shape = jax.ShapeDtypeStruct((batch, KV_GROUPS, seq // NSA_TILE, V_ROWS, NSA_TILE), BF16)
    flat = lambda n, dt: jax.ShapeDtypeStruct((rows, n), dt)
    return pl.pallas_call(
        functools.partial(_inproj_body, tm=tm, tiles_per_seq=tps),
        grid=(rows // tm,),
        in_specs=[row(D_MODEL), pl.BlockSpec((1, 1, tm), lambda r: (r, 0, 0)),
                  full(g_in), full(w_p), full(invc), full(ec), full(es)],
        out_specs=[row(NSA_WIDTH), chunk, chunk, grp(LANES), vtile, grp(HEAD_DIM), vtile,
                   row(NSA_WIDTH), row(SSD_WIDTH), row(CONV_DIM),
                   row(XA_WIDTH), row(XA_WIDTH), row(LANES),
                   pl.BlockSpec((1, LANES, tm), lambda r: (r, 0, 0))],
        out_shape=[flat(NSA_WIDTH, BF16), chunk_shape, chunk_shape, grp_shape(LANES), vtile_shape,
                   grp_shape(HEAD_DIM), vtile_shape,
                   flat(NSA_WIDTH, F32), flat(SSD_WIDTH, F32), flat(CONV_DIM, F32),
                   flat(XA_WIDTH, BF16), flat(XA_WIDTH, F32), flat(LANES, F32),
                   jax.ShapeDtypeStruct((rows // tm, LANES, tm), F32)],
        scratch_shapes=[pltpu.VMEM((tm, KV_WIDTH), F32), pltpu.VMEM((tm, KV_WIDTH), F32)],
        compiler_params=pltpu.CompilerParams(dimension_semantics=("parallel",),
                                             vmem_limit_bytes=VMEM_LIMIT),
        name="in_proj",
    )(x2, pos3, g_in, w_p, invc, ec, es)


def _compress_body(ck_ref, cv_ref, posc_ref, w1k_ref, w2k_ref, pk_ref, w1v_ref, w2v_ref, pv_ref,
                   invl_ref, sgn_ref, kc_ref, vc_ref, *, ncp):
    half = CMP_STRIDE * HEAD_DIM

    def mlp(c_ref, w1_ref, w2_ref, p_ref):
        c = c_ref[0, 0]
        a = _dot(c, w1_ref[0:half, :])
        b = _dot(c, w1_ref[half:2 * half, :])
        bias = _dot(p_ref[...], w1_ref[...])[0:1, :]
        h = a + pltpu.roll(b, ncp - 1, 0) + bias
        return _dot(_silu(h).astype(BF16), w2_ref[...])

    kc = mlp(ck_ref, w1k_ref, w2k_ref, pk_ref)
    vc = mlp(cv_ref, w1v_ref, w2v_ref, pv_ref)
    cs, sn = _rope_tables(posc_ref[0].astype(F32), invl_ref[...], sgn_ref[...])
    lane = lax.broadcasted_iota(I32, (ncp, LANES), 1)
    kc = _rope(kc, cs, sn, (lane % HEAD_DIM) < ROPE_HALF)
    kc_ref[0, 0] = kc[:, 0:HEAD_DIM].astype(BF16)
    vc_ref[0, 0] = vc[:, 0:HEAD_DIM].astype(BF16)


def _compress(ck, cv, posc, w1k, w2k, pk, w1v, w2v, pv, invl, sgn, *, batch, ncp):
    chunk = pl.BlockSpec((1, 1, ncp, CMP_STRIDE * HEAD_DIM), lambda b, g: (b, g, 0, 0))
    full = lambda a: pl.BlockSpec(a.shape, lambda b, g: (0,) * a.ndim)
    out = pl.BlockSpec((1, 1, ncp, HEAD_DIM), lambda b, g: (b, g, 0, 0))
    shp = jax.ShapeDtypeStruct((batch, KV_GROUPS, ncp, HEAD_DIM), BF16)
    return pl.pallas_call(
        functools.partial(_compress_body, ncp=ncp),
        grid=(batch, KV_GROUPS),
        in_specs=[chunk, chunk, pl.BlockSpec((1, ncp, 1), lambda b, g: (b, 0, 0)),
                  full(w1k), full(w2k), full(pk), full(w1v), full(w2v), full(pv),
                  full(invl), full(sgn)],
        out_specs=[out, out],
        out_shape=[shp, shp],
        compiler_params=pltpu.CompilerParams(dimension_semantics=("parallel", "parallel"),
                                             vmem_limit_bytes=VMEM_LIMIT),
        name="compress",
    )(ck, cv, posc, w1k, w2k, pk, w1v, w2v, pv, invl, sgn)


def _nsa_body(q_ref, kc_ref, cat_ref, ksa_ref, vst_ref, kw_ref, vwt_ref, ga_ref, sm_ref,
              o_ref, qa_ref, m_ref, acc_ref, oc_ref, sc_ref, mc_ref, *, tq, ncp, topk):
    g = pl.program_id(1)
    i = pl.program_id(2)
    s0 = i * tq
    nh = HEADS_PER_GROUP
    rows = nh * tq

    def col_max(s):
        n = s.shape[0]
        while n > 8:
            n //= 2
            s = jnp.maximum(s[0:n], s[n:2 * n])
        return jnp.max(s, axis=0, keepdims=True)

    def reset(br):
        m_ref[br] = jnp.full((1, rows), NEG, F32)
        acc_ref[br] = jnp.zeros((V_ROWS, rows), F32)

    def consume(br, s, vt_tile, mask, s_max=None):
        if mask is not None:
            s = jnp.where(mask, s, NEG)
        if s_max is None:
            s_max = col_max(s)
        m_prev = m_ref[br]
        m_new = jnp.maximum(m_prev, s_max)
        alpha = jnp.exp2(m_prev - m_new)
        p = jnp.exp2(s - m_new).astype(BF16)
        acc_ref[br] = alpha * acc_ref[br] + _dot(vt_tile, p)
        m_ref[br] = m_new

    def sel_scores(j):
        rows_j = pl.ds(pl.multiple_of(j * tq, tq), tq)
        return _nt(ksa_ref[0, 0, rows_j, :], qa_ref[...])

    def win_scores(j):
        rows_j = pl.ds(pl.multiple_of(j * tq, tq), tq)
        return _nt(kw_ref[0, 0, rows_j, :], qa_ref[:, 0:HEAD_DIM])

    q4 = q_ref[...]
    for e in range(nh):
        qa_ref[e * tq:(e + 1) * tq, 0:HEAD_DIM] = q4[:, e * HEAD_DIM:(e + 1) * HEAD_DIM]

    n_io = lax.broadcasted_iota(I32, (ncp, rows), 0)
    t_io = s0 + lax.broadcasted_iota(I32, (ncp, rows), 1) % tq
    k_io = lax.broadcasted_iota(I32, (tq, rows), 0)
    t_io2 = lax.broadcasted_iota(I32, (tq, rows), 1) % tq
    diag = k_io <= t_io2
    j1 = jnp.maximum(i - 1, 0)
    j2 = jnp.maximum(i - 2, 0)
    s = jnp.where((n_io * CMP_STRIDE + (CMP_LEN - 1)) <= t_io, _nt(kc_ref[0, 0], qa_ref[:, 0:HEAD_DIM]), NEG)
    sc_ref[0, 0:ncp, :] = s
    mc_ref[0] = col_max(s)
    sc_ref[1] = win_scores(i)
    p = jnp.exp2(sc_ref[0, 0:ncp, :] - mc_ref[0]).astype(BF16)
    big = _dot(cat_ref[0, 0], p)
    sc_ref[0] = win_scores(j1)
    t_row = s0 + lax.broadcasted_iota(I32, (1, rows), 1) % tq
    inv = jnp.where(t_row >= CMP_LEN - 1, 1.0 / jnp.maximum(big[HEAD_DIM:HEAD_DIM + 1, :], 1e-30), 0.0)
    oc_ref[...] = big[0:HEAD_DIM, :] * inv
    imp_x = big[CAT_OVL_ROW0:CAT_OVL_ROW0 + SLC_SLOTS, :] * inv
    imp = imp_x[:, 0:tq]
    for e in range(1, nh):
        imp = imp + imp_x[:, e * tq:(e + 1) * tq]

    reset(1)
    consume(1, sc_ref[1], vwt_ref[0, 0, i], diag)
    sc_ref[1] = win_scores(j2)
    consume(1, sc_ref[0], vwt_ref[0, 0, j1], i >= 1)
    consume(1, sc_ref[1], vwt_ref[0, 0, j2], (k_io > t_io2) & (i >= 2))

    j_io = lax.broadcasted_iota(I32, (SLC_SLOTS, tq), 0)
    cur = (s0 + lax.broadcasted_iota(I32, (SLC_SLOTS, tq), 1)) // SLC_BLOCK
    forced = (j_io == 0) | (j_io == cur) | (j_io == cur - 1)
    valid = j_io <= cur
    v = jnp.where(forced, -3e38, jnp.where(valid, imp, -BIG))
    sel = jnp.where(forced, 1.0, 0.0)
    for _ in range(topk - 3):
        mx = jnp.max(v, axis=0, keepdims=True)
        idx = jnp.min(jnp.where(v == mx, j_io, SLC_SLOTS), axis=0, keepdims=True)
        hit = j_io == idx
        sel = jnp.where(hit, 1.0, sel)
        v = jnp.where(hit, -3e38, v)
    selb = jnp.where((sel > 0.5) & valid, 0.0, NEG)
    selb_t = jnp.concatenate([jnp.zeros((SLC_SLOTS, tq), F32), selb], axis=0).T
    selb_t = selb_t.astype(BF16)
    for e in range(nh):
        qa_ref[e * tq:(e + 1) * tq, HEAD_DIM:LANES] = selb_t[:, HEAD_DIM:LANES]

    def produce(slot, j):
        s = sel_scores(j)
        sc_ref[slot] = s
        mc_ref[slot] = col_max(s)

    reset(0)
    produce(0, 0)

    def sel_pair(j):
        produce(1, j + 1)
        consume(0, sc_ref[0], vst_ref[0, 0, j], None, mc_ref[0])
        produce(0, j + 2)
        consume(0, sc_ref[1], vst_ref[0, 0, j + 1], None, mc_ref[1])

    def sel_quad(jj, carry):
        sel_pair(4 * jj)
        sel_pair(4 * jj + 2)
        return carry

    def sel_rest(jj, carry):
        sel_pair((i // 4) * 4 + 2 * jj)
        return carry

    lax.fori_loop(0, i // 4, sel_quad, 0)
    lax.fori_loop(0, (i % 4) // 2, sel_rest, 0)

    @pl.when(i % 2 == 0)
    def _():
        consume(0, sc_ref[0], vst_ref[0, 0, i], diag)

    @pl.when(i % 2 == 1)
    def _():
        sc_ref[1] = sel_scores(i)
        consume(0, sc_ref[0], vst_ref[0, 0, i - 1], None, mc_ref[0])
        consume(0, sc_ref[1], vst_ref[0, 0, i], diag)

    gates_t = jax.nn.sigmoid(sm_ref[...]).T
    ga = ga_ref[...]

    def head_out(e):
        cs = slice(e * tq, (e + 1) * tq)

        def gate(c):
            c0 = GATE_LANE0 + e * 3 + c
            c1 = c0 + nh * 3
            return jnp.where(g == 0, gates_t[c0:c0 + 1, :], gates_t[c1:c1 + 1, :])

        o_s = acc_ref[0, 0:HEAD_DIM, cs] / acc_ref[0, HEAD_DIM:HEAD_DIM + 1, cs]
        o_w = acc_ref[1, 0:HEAD_DIM, cs] / acc_ref[1, HEAD_DIM:HEAD_DIM + 1, cs]
        return gate(0) * oc_ref[:, cs] + gate(1) * o_s + gate(2) * o_w

    for pr in range(nh // 2):
        ls = slice(pr * LANES, (pr + 1) * LANES)
        pair = jnp.concatenate([head_out(2 * pr), head_out(2 * pr + 1)], axis=0).T
        o_ref[:, ls] = (pair * _silu(ga[:, ls])).astype(BF16)


def _nsa(q, kc, cat, ksa, vst, kw, vwt, ga, sm, *, batch, seq, ncp, topk):
    tq = NSA_TILE
    assert WINDOW == 2 * tq
    nq = seq // tq
    gw = HEADS_PER_GROUP * HEAD_DIM
    rows = HEADS_PER_GROUP * tq
    per_bg = lambda *shape: pl.BlockSpec((1, 1) + shape, lambda b, g, i: (b, g) + (0,) * len(shape))
    return pl.pallas_call(
        functools.partial(_nsa_body, tq=tq, ncp=ncp, topk=topk),
        grid=(batch, KV_GROUPS, nq),
        in_specs=[pl.BlockSpec((tq, gw), lambda b, g, i: (b * nq + i, g)),
                  per_bg(ncp, HEAD_DIM), per_bg(CAT_ROWS, ncp),
                  per_bg(seq, LANES), per_bg(nq, V_ROWS, tq), per_bg(seq, HEAD_DIM),
                  per_bg(nq, V_ROWS, tq),
                  pl.BlockSpec((tq, gw), lambda b, g, i: (b * nq + i, g)),
                  pl.BlockSpec((tq, LANES), lambda b, g, i: (b * nq + i, 0))],
        out_specs=pl.BlockSpec((tq, gw), lambda b, g, i: (b * nq + i, g)),
        out_shape=jax.ShapeDtypeStruct((batch * seq, NSA_WIDTH), BF16),
        scratch_shapes=[pltpu.VMEM((rows, LANES), BF16),
                        pltpu.VMEM((2, 1, rows), F32),
                        pltpu.VMEM((2, V_ROWS, rows), F32),
                        pltpu.VMEM((HEAD_DIM, rows), F32),
                        pltpu.VMEM((2, tq, rows), F32), pltpu.VMEM((2, 1, rows), F32)],
        compiler_params=pltpu.CompilerParams(
            dimension_semantics=("parallel", "parallel", "arbitrary"),
            vmem_limit_bytes=VMEM_LIMIT),
        name="nsa",
    )(q, kc, cat, ksa, vst, kw, vwt, ga, sm)


def _ssdout_body(x_ref, oa_ref, qx_ref, gx_ref, k_ref, v_ref, w_ref, g_ref,
                 xbc_ref, z_ref, sm_ref, dtt_ref, cw_ref, cb_ref, dtb_ref, dtbt_ref, al_ref, alt_ref,
                 dsk_ref, gn_ref, eh_ref, ehw_ref, tril_ref, triu_ref,
                 o_ref, acc_ref, mix_ref, ext_ref, xc_ref, y_ref, h_ref, *, ts, tiles_per_seq):
    L = SSD_CHUNK
    N = SSD_STATE
    P = SSD_HEAD_DIM
    E = SSD_HEADS // SSD_GROUPS
    gw = E * P
    pad = 8
    w_b = NSA_WIDTH
    w_c = NSA_WIDTH + SSD_WIDTH

    @pl.when(pl.program_id(0) % tiles_per_seq == 0)
    def _():
        ext_ref[0:pad, :] = jnp.zeros((pad, CONV_DIM), F32)
        h_ref[...] = jnp.zeros(h_ref.shape, F32)

    acc_ref[...] = x_ref[...] + _dot(oa_ref[...], w_ref[0:w_b, :])

    ext_ref[pad:pad + ts, :] = xbc_ref[...]
    ext = ext_ref[...]
    taps = ext * cw_ref[0:1, :]
    for k in range(1, SSD_CONV):
        taps = ext * cw_ref[k:k + 1, :] + pltpu.roll(taps, 1, 0)
    xc_ref[...] = _silu(cb_ref[...] + taps[pad:pad + ts, :])
    ext_ref[0:pad, :] = ext_ref[ts:ts + pad, :]

    a_row = -jnp.exp(al_ref[...])
    a_col = -jnp.exp(alt_ref[...])
    causal = lax.broadcasted_iota(I32, (L, L), 1) <= lax.broadcasted_iota(I32, (L, L), 0)
    low = lax.broadcasted_iota(I32, (ts, LANES), 1) < P
    nck = ts // L
    chunk = lambda c: slice(c * L, (c + 1) * L)

    xs = xc_ref[:, 0:SSD_WIDTH]
    dt = jax.nn.softplus(sm_ref[...] + dtb_ref[...])
    da = dt * a_row
    a_cs = jnp.concatenate([_dot2_r(tril_ref[...], da[chunk(c), :]) for c in range(nck)], axis=0)
    dt_x = _dot2_l(dt, eh_ref[...])
    acs_w = _dot2_l(a_cs, ehw_ref[...])
    acs_x = jnp.concatenate(
        [jnp.where(low, acs_w[:, (2 * n) * LANES:(2 * n + 1) * LANES],
                   acs_w[:, (2 * n + 1) * LANES:(2 * n + 2) * LANES]) for n in range(SSD_HEADS // 2)],
        axis=1)
    a_last = [acs_x[c * L + L - 1:c * L + L, :] for c in range(nck)]
    a_last_x = jnp.concatenate([jnp.broadcast_to(a, (L, SSD_WIDTH)) for a in a_last], axis=0)
    dat = jax.nn.softplus(dtt_ref[0, DT_LANE0:DT_LANE0 + SSD_HEADS, :] + dtbt_ref[...]) * a_col
    acs_t = [_dot2_l(dat[:, chunk(c)], triu_ref[...]) for c in range(nck)]
    xdt = xs * dt_x
    xdo = (xdt * jnp.exp(a_last_x - acs_x)).astype(BF16)
    xdt_b = xdt.astype(BF16)
    y_ref[...] = xs * dsk_ref[...]
    pre = jnp.exp(acs_x)

    def attn_head(h):
        hs = slice(h * XA_HEAD_DIM, (h + 1) * XA_HEAD_DIM)
        s = _nt(qx_ref[:, hs], k_ref[0, :, hs])
        s = s - jnp.max(s, axis=-1, keepdims=True)
        p = jnp.exp(s)
        p = p / jnp.sum(p, axis=-1, keepdims=True)
        oc = _dot(p.astype(BF16), v_ref[0, :, hs]) * _silu(gx_ref[:, hs])
        mix_ref[:, SSD_WIDTH + h * XA_HEAD_DIM:SSD_WIDTH + (h + 1) * XA_HEAD_DIM] = oc.astype(BF16)

    for gi in range(SSD_GROUPS):
        gs = slice(gi * gw, (gi + 1) * gw)
        bm = xc_ref[:, SSD_WIDTH + gi * N:SSD_WIDTH + (gi + 1) * N]
        cm_b = xc_ref[:, SSD_WIDTH + SSD_GROUPS * N + gi * N:SSD_WIDTH + SSD_GROUPS * N + (gi + 1) * N].astype(BF16)
        bm_b = bm.astype(BF16)
        h_c = h_ref[gi]
        h_in = []
        for c in range(nck):
            h_in.append(h_c.astype(BF16))
            h_c = h_c * jnp.exp(a_last[c][:, gs]) + _dot(bm[chunk(c), :].T.astype(BF16), xdo[chunk(c), gs])
        h_ref[gi] = h_c
        for c in range(nck):
            rs = chunk(c)
            cbm = _nt(cm_b[rs, :], bm_b[rs, :])
            y_ref[rs, gs] += _dot(cm_b[rs, :], h_in[c]) * pre[rs, gs]
            for e in range(E):
                h = gi * E + e
                hs = slice(h * P, (h + 1) * P)
                d = acs_w[rs, h * LANES:(h + 1) * LANES] - acs_t[c][h:h + 1, :]
                dec = jnp.exp(jnp.where(causal, d, NEG))
                y_ref[rs, hs] += _dot((cbm * dec).astype(BF16), xdt_b[rs, hs])

    y = y_ref[...] * _silu(z_ref[...])
    ms = jnp.mean(y * y, axis=-1, keepdims=True)
    mix_ref[:, 0:SSD_WIDTH] = (y * lax.rsqrt(ms + EPS) * gn_ref[...]).astype(BF16)
    acc_ref[...] += _dot(mix_ref[:, 0:SSD_WIDTH], w_ref[w_b:w_c, :])

    for h in range(XA_HEADS):
        attn_head(h)
    acc = acc_ref[...] + _dot(mix_ref[:, SSD_WIDTH:], w_ref[w_c:, :])
    ms = jnp.mean(acc * acc, axis=-1, keepdims=True)
    o_ref[...] = acc * lax.rsqrt(ms + EPS) * g_ref[...]


def _ssd_out(x2, oa, qx, gx, km, vm, w_out, g_final, xbc, z, sm, smt, consts, *, batch, seq):
    rows = batch * seq
    ts = ROW_TILE
    tps = seq // ts
    mlen = km.shape[1]
    row = lambda n: pl.BlockSpec((ts, n), lambda r: (r, 0))
    full = lambda a: pl.BlockSpec(a.shape, lambda r: (0,) * a.ndim)
    mem = pl.BlockSpec((1, mlen, XA_WIDTH), lambda r: (r // tps, 0, 0))
    return pl.pallas_call(
        functools.partial(_ssdout_body, ts=ts, tiles_per_seq=tps),
        grid=(rows // ts,),
        in_specs=[row(D_MODEL), row(NSA_WIDTH), row(XA_WIDTH), row(XA_WIDTH), mem, mem,
                  full(w_out), full(g_final),
                  row(CONV_DIM), row(SSD_WIDTH), row(LANES), pl.BlockSpec((1, LANES, ts), lambda r: (r, 0, 0))]
                 + [full(a) for a in consts],
        out_specs=row(D_MODEL),
        out_shape=jax.ShapeDtypeStruct((rows, D_MODEL), F32),
        scratch_shapes=[pltpu.VMEM((ts, D_MODEL), F32), pltpu.VMEM((ts, SSD_WIDTH + XA_WIDTH), BF16),
                        pltpu.VMEM((ts + 8, CONV_DIM), F32), pltpu.VMEM((ts, CONV_DIM), F32),
                        pltpu.VMEM((ts, SSD_WIDTH), F32),
                        pltpu.VMEM((SSD_GROUPS, SSD_STATE, SSD_WIDTH // SSD_GROUPS), F32)],
        compiler_params=pltpu.CompilerParams(dimension_semantics=("arbitrary",),
                                             vmem_limit_bytes=VMEM_LIMIT),
        name="ssd_out",
    )(x2, oa, qx, gx, km, vm, w_out, g_final, xbc, z, sm, smt, *consts)


def _memkv_body(mem_ref, g_ref, w_ref, k_ref, v_ref):
    x = mem_ref[0]
    ms = jnp.mean(x * x, axis=-1, keepdims=True)
    xn = (x * lax.rsqrt(ms + EPS) * g_ref[...]).astype(BF16)
    k_ref[0] = _dot(xn, w_ref[:, 0:XA_WIDTH]).astype(BF16)
    v_ref[0] = _dot(xn, w_ref[:, XA_WIDTH:2 * XA_WIDTH]).astype(BF16)


def _mem_kv(mem, g_mem, w_kv):
    batch, mlen, _ = mem.shape
    full = lambda a: pl.BlockSpec(a.shape, lambda b: (0,) * a.ndim)
    out = pl.BlockSpec((1, mlen, XA_WIDTH), lambda b: (b, 0, 0))
    shp = jax.ShapeDtypeStruct((batch, mlen, XA_WIDTH), BF16)
    return pl.pallas_call(
        _memkv_body,
        grid=(batch,),
        in_specs=[pl.BlockSpec((1, mlen, D_MODEL), lambda b: (b, 0, 0)), full(g_mem), full(w_kv)],
        out_specs=[out, out],
        out_shape=[shp, shp],
        compiler_params=pltpu.CompilerParams(dimension_semantics=("parallel",),
                                             vmem_limit_bytes=VMEM_LIMIT),
        name="mem_kv",
    )(mem, g_mem, w_kv)


def _permute_w_in(w):
    rb = 128
    return pl.pallas_call(
        _wprep_body,
        grid=(w.shape[0] // rb,),
        in_specs=[pl.BlockSpec((rb, w.shape[1]), lambda r: (r, 0))],
        out_specs=pl.BlockSpec((rb, N_PROJ), lambda r: (r, 0)),
        out_shape=jax.ShapeDtypeStruct((w.shape[0], N_PROJ), BF16),
        compiler_params=pltpu.CompilerParams(dimension_semantics=("parallel",),
                                             vmem_limit_bytes=VMEM_LIMIT),
        name="w_prep",
    )(w)


def _wprep_body(w_ref, o_ref):
    sizes = [NSA_WIDTH] + [KV_WIDTH] * 6 + [NSA_HEADS * 3, NSA_WIDTH, SSD_WIDTH, CONV_DIM, SSD_HEADS,
                                          XA_WIDTH, XA_WIDTH]
    offs = [0]
    for s in sizes:
        offs.append(offs[-1] + s)
    w = w_ref[...]

    def move(dst, lo, hi):
        o_ref[:, dst:dst + hi - lo] = w[:, lo:hi].astype(BF16)

    move(OFF_Q, offs[0], offs[7])
    move(OFF_GA, offs[8], offs[11])
    move(OFF_QX, offs[12], offs[14])
    small = jnp.concatenate(
        [w[:, offs[7]:offs[8]], w[:, offs[11]:offs[12]],
         jnp.zeros((w.shape[0], LANES - NSA_HEADS * 3 - SSD_HEADS), w.dtype)], axis=1)
    o_ref[:, OFF_SM:OFF_SM + LANES] = small.astype(BF16)


def _lane_row(vals, lane0):
    return jnp.zeros((1, LANES), F32).at[0, lane0:lane0 + vals.shape[0]].set(vals.astype(F32))


def _forward(x, mem, positions, g_in, w_in, cmp_pos_k, w_cmp1_k, w_cmp2_k, cmp_pos_v, w_cmp1_v,
             w_cmp2_v, conv_w, conv_b, dt_bias, a_log, d_skip, g_ssd_norm, g_mem, w_mem_kv, w_out,
             g_final):
    batch, seq, _ = x.shape
    ncp = seq // CMP_STRIDE
    n_slc = seq // SLC_BLOCK
    assert n_slc <= SLC_SLOTS and seq % ROW_TILE == 0 and ncp <= NSA_TILE
    topk = min(SLC_TOPK, n_slc)
    rows = batch * seq

    inv = ROPE_THETA ** (-jnp.arange(0, ROPE_DIM, 2, dtype=F32) / ROPE_DIM)
    head_inv = jnp.concatenate([inv, inv, jnp.zeros((HEAD_DIM - ROPE_DIM,), F32)])
    invl = jnp.tile(head_inv, LANES // HEAD_DIM)[None, :]
    head_sgn = jnp.concatenate([-jnp.ones((ROPE_HALF,), F32), jnp.ones((ROPE_HALF,), F32),
                                jnp.zeros((HEAD_DIM - ROPE_DIM,), F32)])
    sgn = jnp.tile(head_sgn, LANES // HEAD_DIM)[None, :]
    inv_rows = jnp.concatenate([inv, jnp.zeros((ROPE_FREQ_ROWS - ROPE_HALF,), F32)])[:, None]
    d_ix = jnp.arange(LANES) % HEAD_DIM
    freq_of_lane = jnp.where(d_ix < ROPE_DIM, d_ix % ROPE_HALF, ROPE_HALF)
    spread_cos = (freq_of_lane[:, None] == jnp.arange(ROPE_FREQ_ROWS)[None, :]).astype(F32)
    spread_sin = spread_cos * sgn[0][:, None]

    x2 = x.reshape(rows, D_MODEL)
    pos3 = positions.reshape(rows // ROW_TILE, 1, ROW_TILE)
    h = x2
    for l in range(g_in.shape[0]):
        (q, ck, cv, ksa, vst, kw, vwt, ga, z, xbc, qx, gx, sm, smt) = _in_proj(
            h, pos3, g_in[l][None, :], _permute_w_in(w_in[l]), inv_rows,
            spread_cos.astype(BF16), spread_sin.astype(BF16), batch=batch, seq=seq)

        cmp_end = jnp.minimum(jnp.arange(ncp) * CMP_STRIDE + CMP_LEN - 1, seq - 1)
        posc = positions[:, cmp_end][:, :, None]
        pad_w2 = lambda w: jnp.pad(w, ((0, 0), (0, LANES - HEAD_DIM))).astype(BF16)
        pos_rows = lambda p: jnp.broadcast_to(p.reshape(1, CMP_LEN * HEAD_DIM), (8, CMP_LEN * HEAD_DIM)).astype(BF16)
        kc, vc = _compress(ck, cv, posc,
                           w_cmp1_k[l].astype(BF16), pad_w2(w_cmp2_k[l]), pos_rows(cmp_pos_k[l]),
                           w_cmp1_v[l].astype(BF16), pad_w2(w_cmp2_v[l]), pos_rows(cmp_pos_v[l]),
                           invl, sgn, batch=batch, ncp=ncp)
        n_ix = jnp.arange(ncp)[None, :]
        j_ix = jnp.arange(SLC_SLOTS)[:, None]
        ovl = ((n_ix * CMP_STRIDE < j_ix * SLC_BLOCK + SLC_BLOCK)
               & (n_ix * CMP_STRIDE + CMP_LEN > j_ix * SLC_BLOCK)
               & (n_ix < ncp - (CMP_LEN // CMP_STRIDE - 1))).astype(BF16)
        bg = (batch, KV_GROUPS)
        cat = jnp.concatenate(
            [jnp.swapaxes(vc, 2, 3), jnp.ones(bg + (1, ncp), BF16),
             jnp.zeros(bg + (CAT_OVL_ROW0 - HEAD_DIM - 1, ncp), BF16),
             jnp.broadcast_to(ovl, bg + ovl.shape),
             jnp.zeros(bg + (CAT_ROWS - CAT_OVL_ROW0 - SLC_SLOTS, ncp), BF16)], axis=2)
        o_a = _nsa(q, kc, cat, ksa, vst, kw, vwt, ga, sm,
                   batch=batch, seq=seq, ncp=ncp, topk=topk)

        head_of_lane = jnp.arange(SSD_WIDTH) // SSD_HEAD_DIM
        k_ix = jnp.arange(LANES)[:, None]
        eh = (k_ix == DT_LANE0 + head_of_lane[None, :]).astype(BF16)
        ehw = (k_ix == DT_LANE0 + (jnp.arange(SSD_HEADS * LANES) // LANES)[None, :]).astype(BF16)
        t_ix = jnp.arange(SSD_CHUNK)
        tril = (t_ix[None, :] <= t_ix[:, None]).astype(BF16)
        ssd_consts = [conv_w[l], conv_b[l][None, :],
                      _lane_row(dt_bias[l], DT_LANE0), dt_bias[l].astype(F32)[:, None],
                      _lane_row(a_log[l], DT_LANE0), a_log[l].astype(F32)[:, None],
                      jnp.repeat(d_skip[l].astype(F32), SSD_HEAD_DIM)[None, :], g_ssd_norm[l][None, :],
                      eh, ehw, tril, tril.T]

        km, vm = _mem_kv(mem, g_mem[l][None, :], w_mem_kv[l].astype(BF16))
        assert g_in.shape[0] == 1, "the final RMSNorm is fused into the (single) layer's last kernel"
        h = _ssd_out(h, o_a, qx, gx, km, vm, w_out[l].astype(BF16), g_final[None, :],
                     xbc, z, sm, smt, ssd_consts, batch=batch, seq=seq)
    return h.reshape(batch, seq, D_MODEL)


def kernel(x, mem, positions, g_in, w_in, cmp_pos_k, w_cmp1_k, w_cmp2_k, cmp_pos_v, w_cmp1_v, w_cmp2_v,
           conv_w, conv_b, dt_bias, a_log, d_skip, g_ssd_norm, g_mem, w_mem_kv, w_out, g_final):
    return _forward(x, mem, positions, g_in, w_in, cmp_pos_k, w_cmp1_k, w_cmp2_k, cmp_pos_v, w_cmp1_v,
                    w_cmp2_v, conv_w, conv_b, dt_bias, a_log, d_skip, g_ssd_norm, g_mem, w_mem_kv,
                    w_out, g_final)
```

```python
import functools

import jax
import jax.numpy as jnp
from jax import lax
from jax.experimental import pallas as pl
from jax.experimental.pallas import tpu as pltpu

F32 = jnp.float32
BF16 = jnp.bfloat16
I32 = jnp.int32

D_MODEL = 1024
NSA_HEADS = 8
HEAD_DIM = 64
KV_GROUPS = 2
HEADS_PER_GROUP = NSA_HEADS // KV_GROUPS
NSA_WIDTH = NSA_HEADS * HEAD_DIM
KV_WIDTH = KV_GROUPS * HEAD_DIM
CMP_LEN = 32
CMP_STRIDE = 16
CMP_HIDDEN = 256
SLC_BLOCK = 64
SLC_TOPK = 16
SLC_SLOTS = 64
WINDOW = 512
BIG = 1e9
NEG = -1e30

SSD_HEADS = 8
SSD_HEAD_DIM = 64
SSD_WIDTH = SSD_HEADS * SSD_HEAD_DIM
SSD_GROUPS = 2
SSD_STATE = 128
SSD_CONV = 4
SSD_CHUNK = 128
CONV_DIM = SSD_WIDTH + 2 * SSD_GROUPS * SSD_STATE

XA_HEADS = 4
XA_HEAD_DIM = 128
XA_WIDTH = XA_HEADS * XA_HEAD_DIM
MIX_WIDTH = NSA_WIDTH + SSD_WIDTH + XA_WIDTH

ROPE_THETA = 500000.0
ROPE_DIM = HEAD_DIM // 4
ROPE_HALF = ROPE_DIM // 2
ROPE_FREQ_ROWS = 16
EPS = 1e-6

LANES = 128
GATE_LANE0 = 0
DT_LANE0 = 24

OFF_Q = 0
OFF_KV = OFF_Q + NSA_WIDTH
OFF_GA = OFF_KV + 6 * KV_WIDTH
OFF_Z = OFF_GA + NSA_WIDTH
OFF_XBC = OFF_Z + SSD_WIDTH
OFF_QX = OFF_XBC + CONV_DIM
OFF_GX = OFF_QX + XA_WIDTH
OFF_SM = OFF_GX + XA_WIDTH
N_PROJ = OFF_SM + LANES

ROW_TILE = 512
NSA_TILE = 256
V_ROWS = 80
LOG2E = 1.4426950408889634
CAT_OVL_ROW0 = 80
CAT_ROWS = 144
VMEM_LIMIT = 56 * 1024 * 1024


def _nt(a, b):
    return lax.dot_general(a, b, (((1,), (1,)), ((), ())), preferred_element_type=F32)


def _dot(a, b):
    return jnp.dot(a, b, preferred_element_type=F32)


def _split2(x):
    hi = x.astype(BF16)
    return hi, (x - hi.astype(F32)).astype(BF16)


def _dot2_l(x, w):
    hi, lo = _split2(x)
    return _dot(hi, w) + _dot(lo, w)


def _dot2_r(w, x):
    hi, lo = _split2(x)
    return _dot(w, hi) + _dot(w, lo)


def _silu(x):
    h = 0.5 * x
    return h + h * jnp.tanh(h)


def _rope(a, cs, sn, first):
    r = jnp.where(first, pltpu.roll(a, LANES - ROPE_HALF, 1), pltpu.roll(a, ROPE_HALF, 1))
    return a * cs + r * sn


def _rope_tables(pos_f32, invl, sgn):
    ang = pos_f32 * invl
    return jnp.cos(ang), jnp.sin(ang) * sgn


def _inproj_body(x_ref, pos_ref, g_ref, w_ref, invc_ref, ec_ref, es_ref,
                 q_ref, ck_ref, cv_ref, ksa_ref, vst_ref, kw_ref, vwt_ref,
                 ga_ref, z_ref, xbc_ref, qx_ref, gx_ref, sm_ref, smt_ref, sk_ref, sv_ref, *, tm, tiles_per_seq):
    x = x_ref[...]
    ms = jnp.mean(x * x, axis=-1, keepdims=True)
    xn = (x * lax.rsqrt(ms + EPS) * g_ref[...]).astype(BF16)

    def mm(lo, n):
        return _dot(xn, w_ref[:, lo:lo + n])

    ga_ref[...] = mm(OFF_GA, NSA_WIDTH)
    z_ref[...] = mm(OFF_Z, SSD_WIDTH)

    ang = invc_ref[...] * pos_ref[0].astype(F32)
    cs = _dot2_r(ec_ref[...], jnp.cos(ang)).T
    sn = _dot2_r(es_ref[...], jnp.sin(ang)).T
    lane = lax.broadcasted_iota(I32, (tm, LANES), 1)
    first = (lane % HEAD_DIM) < ROPE_HALF

    qf = mm(OFF_Q, NSA_WIDTH)
    for c in range(NSA_WIDTH // LANES):
        sl = slice(c * LANES, (c + 1) * LANES)
        q_ref[:, sl] = (_rope(qf[:, sl], cs, sn, first) * (HEAD_DIM ** -0.5 * LOG2E)).astype(BF16)

    kv = mm(OFF_KV, 6 * KV_WIDTH)
    k_c, v_c, k_s, v_s, k_w, v_w = [kv[:, n * KV_WIDTH:(n + 1) * KV_WIDTH] for n in range(6)]
    k_s = _rope(k_s, cs, sn, first)
    k_w = _rope(k_w, cs, sn, first)
    s_base = (pl.program_id(0) % tiles_per_seq) * tm
    blk = (s_base + lax.broadcasted_iota(I32, (tm, LANES), 0)) // SLC_BLOCK
    onehot = jnp.where(lane - HEAD_DIM == blk, 1.0, 0.0)
    low = lane < HEAD_DIM
    vs_t = v_s.T
    vw_t = v_w.T
    tail = jnp.where(lax.broadcasted_iota(I32, (V_ROWS - HEAD_DIM, NSA_TILE), 0) == 0, 1.0, 0.0).astype(BF16)
    for g in range(KV_GROUPS):
        gs = slice(g * HEAD_DIM, (g + 1) * HEAD_DIM)
        kw_ref[0, g] = k_w[:, gs].astype(BF16)
        ks_g = k_s if g == 0 else pltpu.roll(k_s, HEAD_DIM, 1)
        ksa_ref[0, g] = jnp.where(low, ks_g, onehot).astype(BF16)
        for c in range(tm // NSA_TILE):
            cols = slice(c * NSA_TILE, (c + 1) * NSA_TILE)
            vst_ref[0, g, c, 0:HEAD_DIM, :] = vs_t[gs, cols].astype(BF16)
            vst_ref[0, g, c, HEAD_DIM:V_ROWS, :] = tail
            vwt_ref[0, g, c, 0:HEAD_DIM, :] = vw_t[gs, cols].astype(BF16)
            vwt_ref[0, g, c, HEAD_DIM:V_ROWS, :] = tail

    sk_ref[...] = k_c
    sv_ref[...] = v_c
    nchunk = tm // CMP_STRIDE
    for t in range(CMP_STRIDE):
        dst = slice(t * HEAD_DIM, (t + 1) * HEAD_DIM)
        kt = sk_ref[pl.ds(t, nchunk, stride=CMP_STRIDE), :].astype(BF16)
        vt = sv_ref[pl.ds(t, nchunk, stride=CMP_STRIDE), :].astype(BF16)
        for g in range(KV_GROUPS):
            gs = slice(g * HEAD_DIM, (g + 1) * HEAD_DIM)
            ck_ref[0, g, :, dst] = kt[:, gs]
            cv_ref[0, g, :, dst] = vt[:, gs]

    xbc_ref[...] = mm(OFF_XBC, CONV_DIM)
    qx_ref[...] = (mm(OFF_QX, XA_WIDTH) * (XA_HEAD_DIM ** -0.5 * LOG2E)).astype(BF16)
    gx_ref[...] = mm(OFF_GX, XA_WIDTH)
    sm = mm(OFF_SM, LANES)
    sm_ref[...] = sm
    smt_ref[0] = sm.T


def _in_proj(x2, pos3, g_in, w_p, invc, ec, es, *, batch, seq):
    rows = batch * seq
    tm = ROW_TILE
    tps = seq // tm
    row = lambda n: pl.BlockSpec((tm, n), lambda r: (r, 0))
    full = lambda a: pl.BlockSpec(a.shape, lambda r: (0,) * a.ndim)
    grp = lambda n: pl.BlockSpec((1, KV_GROUPS, tm, n), lambda r: (r // tps, 0, r % tps, 0))
    grp_shape = lambda n: jax.ShapeDtypeStruct((batch, KV_GROUPS, seq, n), BF16)
    chunk = pl.BlockSpec((1, KV_GROUPS, tm // CMP_STRIDE, CMP_STRIDE * HEAD_DIM),
                         lambda r: (r // tps, 0, r % tps, 0))
    chunk_shape = jax.ShapeDtypeStruct((batch, KV_GROUPS, seq // CMP_STRIDE, CMP_STRIDE * HEAD_DIM), BF16)
    vtile = pl.BlockSpec((1, KV_GROUPS, tm // NSA_TILE, V_ROWS, NSA_TILE), lambda r: (r // tps, 0, r % tps, 0, 0))
    vtile_shape = jax.ShapeDtypeStruct((batch, KV_GROUPS, seq // NSA_TILE, V_ROWS, NSA_TILE), BF16)
    flat = lambda n, dt: jax.ShapeDtypeStruct((rows, n), dt)
    return pl.pallas_call(
        functools.partial(_inproj_body, tm=tm, tiles_per_seq=tps),
        grid=(rows // tm,),
        in_specs=[row(D_MODEL), pl.BlockSpec((1, 1, tm), lambda r: (r, 0, 0)),
                  full(g_in), full(w_p), full(invc), full(ec), full(es)],
        out_specs=[row(NSA_WIDTH), chunk, chunk, grp(LANES), vtile, grp(HEAD_DIM), vtile,
                   row(NSA_WIDTH), row(SSD_WIDTH), row(CONV_DIM),
                   row(XA_WIDTH), row(XA_WIDTH), row(LANES),
                   pl.BlockSpec((1, LANES, tm), lambda r: (r, 0, 0))],
        out_shape=[flat(NSA_WIDTH, BF16), chunk_shape, chunk_shape, grp_shape(LANES), vtile_shape,
                   grp_shape(HEAD_DIM), vtile_shape,
                   flat(NSA_WIDTH, F32), flat(SSD_WIDTH, F32), flat(CONV_DIM, F32),
                   flat(XA_WIDTH, BF16), flat(XA_WIDTH, F32), flat(LANES, F32),
                   jax.ShapeDtypeStruct((rows // tm, LANES, tm), F32)],
        scratch_shapes=[pltpu.VMEM((tm, KV_WIDTH), F32), pltpu.VMEM((tm, KV_WIDTH), F32)],
        compiler_params=pltpu.CompilerParams(dimension_semantics=("parallel",),
                                             vmem_limit_bytes=VMEM_LIMIT),
        name="in_proj",
    )(x2, pos3, g_in, w_p, invc, ec, es)


def _compress_body(ck_ref, cv_ref, posc_ref, w1k_ref, w2k_ref, pk_ref, w1v_ref, w2v_ref, pv_ref,
                   invl_ref, sgn_ref, kc_ref, vc_ref, *, ncp):
    half = CMP_STRIDE * HEAD_DIM

    def mlp(c_ref, w1_ref, w2_ref, p_ref):
        c = c_ref[0, 0]
        a = _dot(c, w1_ref[0:half, :])
        b = _dot(c, w1_ref[half:2 * half, :])
        bias = _dot(p_ref[...], w1_ref[...])[0:1, :]
        h = a + pltpu.roll(b, ncp - 1, 0) + bias
        return _dot(_silu(h).astype(BF16), w2_ref[...])

    kc = mlp(ck_ref, w1k_ref, w2k_ref, pk_ref)
    vc = mlp(cv_ref, w1v_ref, w2v_ref, pv_ref)
    cs, sn = _rope_tables(posc_ref[0].astype(F32), invl_ref[...], sgn_ref[...])
    lane = lax.broadcasted_iota(I32, (ncp, LANES), 1)
    kc = _rope(kc, cs, sn, (lane % HEAD_DIM) < ROPE_HALF)
    kc_ref[0, 0] = kc[:, 0:HEAD_DIM].astype(BF16)
    vc_ref[0, 0] = vc[:, 0:HEAD_DIM].astype(BF16)


def _compress(ck, cv, posc, w1k, w2k, pk, w1v, w2v, pv, invl, sgn, *, batch, ncp):
    chunk = pl.BlockSpec((1, 1, ncp, CMP_STRIDE * HEAD_DIM), lambda b, g: (b, g, 0, 0))
    full = lambda a: pl.BlockSpec(a.shape, lambda b, g: (0,) * a.ndim)
    out = pl.BlockSpec((1, 1, ncp, HEAD_DIM), lambda b, g: (b, g, 0, 0))
    shp = jax.ShapeDtypeStruct((batch, KV_GROUPS, ncp, HEAD_DIM), BF16)
    return pl.pallas_call(
        functools.partial(_compress_body, ncp=ncp),
        grid=(batch, KV_GROUPS),
        in_specs=[chunk, chunk, pl.BlockSpec((1, ncp, 1), lambda b, g: (b, 0, 0)),
                  full(w1k), full(w2k), full(pk), full(w1v), full(w2v), full(pv),
                  full(invl), full(sgn)],
        out_specs=[out, out],
        out_shape=[shp, shp],
        compiler_params=pltpu.CompilerParams(dimension_semantics=("parallel", "parallel"),
                                             vmem_limit_bytes=VMEM_LIMIT),
        name="compress",
    )(ck, cv, posc, w1k, w2k, pk, w1v, w2v, pv, invl, sgn)


def _nsa_body(q_ref, kc_ref, cat_ref, ksa_ref, vst_ref, kw_ref, vwt_ref, ga_ref, sm_ref,
              o_ref, qa_ref, m_ref, acc_ref, oc_ref, sc_ref, mc_ref, gt_ref, gs_ref, *, tq, ncp, topk):
    g = pl.program_id(1)
    i = pl.program_id(2)
    s0 = i * tq
    nh = HEADS_PER_GROUP
    rows = nh * tq

    def col_max(s):
        n = s.shape[0]
        while n > 8:
            n //= 2
            s = jnp.maximum(s[0:n], s[n:2 * n])
        return jnp.max(s, axis=0, keepdims=True)

    def reset(br):
        m_ref[br] = jnp.full((1, rows), NEG, F32)
        acc_ref[br] = jnp.zeros((V_ROWS, rows), F32)

    def consume(br, s, vt_tile, mask, s_max=None):
        if mask is not None:
            s = jnp.where(mask, s, NEG)
        if s_max is None:
            s_max = col_max(s)
        m_prev = m_ref[br]
        m_new = jnp.maximum(m_prev, s_max)
        alpha = jnp.exp2(m_prev - m_new)
        p = jnp.exp2(s - m_new).astype(BF16)
        acc_ref[br] = alpha * acc_ref[br] + _dot(vt_tile, p)
        m_ref[br] = m_new

    def sel_scores(j):
        rows_j = pl.ds(pl.multiple_of(j * tq, tq), tq)
        return _nt(ksa_ref[0, 0, rows_j, :], qa_ref[...])

    def win_scores(j):
        rows_j = pl.ds(pl.multiple_of(j * tq, tq), tq)
        return _nt(kw_ref[0, 0, rows_j, :], qa_ref[:, 0:HEAD_DIM])

    q4 = q_ref[...]
    for e in range(nh):
        qa_ref[e * tq:(e + 1) * tq, 0:HEAD_DIM] = q4[:, e * HEAD_DIM:(e + 1) * HEAD_DIM]

    n_io = lax.broadcasted_iota(I32, (ncp, rows), 0)
    t_io = s0 + lax.broadcasted_iota(I32, (ncp, rows), 1) % tq
    k_io = lax.broadcasted_iota(I32, (tq, rows), 0)
    t_io2 = lax.broadcasted_iota(I32, (tq, rows), 1) % tq
    diag = k_io <= t_io2
    j1 = jnp.maximum(i - 1, 0)
    j2 = jnp.maximum(i - 2, 0)
    s = jnp.where((n_io * CMP_STRIDE + (CMP_LEN - 1)) <= t_io, _nt(kc_ref[0, 0], qa_ref[:, 0:HEAD_DIM]), NEG)
    sc_ref[0, 0:ncp, :] = s
    mc_ref[0] = col_max(s)
    sc_ref[1] = win_scores(i)
    p = jnp.exp2(sc_ref[0, 0:ncp, :] - mc_ref[0]).astype(BF16)
    big = _dot(cat_ref[0, 0], p)
    sc_ref[0] = win_scores(j1)
    t_row = s0 + lax.broadcasted_iota(I32, (1, rows), 1) % tq
    inv = jnp.where(t_row >= CMP_LEN - 1, 1.0 / jnp.maximum(big[HEAD_DIM:HEAD_DIM + 1, :], 1e-30), 0.0)
    oc_ref[...] = big[0:HEAD_DIM, :] * inv
    imp_x = big[CAT_OVL_ROW0:CAT_OVL_ROW0 + SLC_SLOTS, :] * inv
    imp = imp_x[:, 0:tq]
    for e in range(1, nh):
        imp = imp + imp_x[:, e * tq:(e + 1) * tq]

    reset(1)
    consume(1, sc_ref[1], vwt_ref[0, 0, i], diag)
    sc_ref[1] = win_scores(j2)
    consume(1, sc_ref[0], vwt_ref[0, 0, j1], i >= 1)
    consume(1, sc_ref[1], vwt_ref[0, 0, j2], (k_io > t_io2) & (i >= 2))

    j_io = lax.broadcasted_iota(I32, (SLC_SLOTS, tq), 0)
    cur = (s0 + lax.broadcasted_iota(I32, (SLC_SLOTS, tq), 1)) // SLC_BLOCK
    forced = (j_io == 0) | (j_io == cur) | (j_io == cur - 1)
    valid = j_io <= cur
    v = jnp.where(forced, -3e38, jnp.where(valid, imp, -BIG))
    sel = jnp.where(forced, 1.0, 0.0)
    for _ in range(topk - 3):
        mx = jnp.max(v, axis=0, keepdims=True)
        idx = jnp.min(jnp.where(v == mx, j_io, SLC_SLOTS), axis=0, keepdims=True)
        hit = j_io == idx
        sel = jnp.where(hit, 1.0, sel)
        v = jnp.where(hit, -3e38, v)
    selb = jnp.where((sel > 0.5) & valid, 0.0, NEG)
    selb_t = jnp.concatenate([jnp.zeros((SLC_SLOTS, tq), F32), selb], axis=0).T
    selb_t = selb_t.astype(BF16)
    for e in range(nh):
        qa_ref[e * tq:(e + 1) * tq, HEAD_DIM:LANES] = selb_t[:, HEAD_DIM:LANES]

    def produce(slot, j):
        s = sel_scores(j)
        sc_ref[slot] = s
        mc_ref[slot] = col_max(s)

    reset(0)
    produce(0, 0)
    gates_t = jax.nn.sigmoid(sm_ref[...]).T
    gs_ref[...] = _silu(ga_ref[...])

    def gate(e, c):
        c0 = GATE_LANE0 + e * 3 + c
        c1 = c0 + nh * 3
        return jnp.where(g == 0, gates_t[c0:c0 + 1, :], gates_t[c1:c1 + 1, :])

    for e in range(nh):
        cs = slice(e * tq, (e + 1) * tq)
        o_w = acc_ref[1, 0:HEAD_DIM, cs] / acc_ref[1, HEAD_DIM:HEAD_DIM + 1, cs]
        oc_ref[:, cs] = gate(e, 0) * oc_ref[:, cs] + gate(e, 2) * o_w
        gt_ref[e:e + 1, :] = gate(e, 1)

    def sel_pair(j):
        produce(1, j + 1)
        consume(0, sc_ref[0], vst_ref[0, 0, j], None, mc_ref[0])
        produce(0, j + 2)
        consume(0, sc_ref[1], vst_ref[0, 0, j + 1], None, mc_ref[1])

    def sel_quad(jj, carry):
        sel_pair(4 * jj)
        sel_pair(4 * jj + 2)
        return carry

    def sel_rest(jj, carry):
        sel_pair((i // 4) * 4 + 2 * jj)
        return carry

    lax.fori_loop(0, i // 4, sel_quad, 0)
    lax.fori_loop(0, (i % 4) // 2, sel_rest, 0)

    @pl.when(i % 2 == 0)
    def _():
        consume(0, sc_ref[0], vst_ref[0, 0, i], diag)

    @pl.when(i % 2 == 1)
    def _():
        sc_ref[1] = sel_scores(i)
        consume(0, sc_ref[0], vst_ref[0, 0, i - 1], None, mc_ref[0])
        consume(0, sc_ref[1], vst_ref[0, 0, i], diag)

    def head_out(e):
        cs = slice(e * tq, (e + 1) * tq)
        o_s = acc_ref[0, 0:HEAD_DIM, cs] / acc_ref[0, HEAD_DIM:HEAD_DIM + 1, cs]
        return oc_ref[:, cs] + gt_ref[e:e + 1, :] * o_s

    for pr in range(nh // 2):
        ls = slice(pr * LANES, (pr + 1) * LANES)
        pair = jnp.concatenate([head_out(2 * pr), head_out(2 * pr + 1)], axis=0).T
        o_ref[:, ls] = (pair * gs_ref[:, ls]).astype(BF16)


def _nsa(q, kc, cat, ksa, vst, kw, vwt, ga, sm, *, batch, seq, ncp, topk):
    tq = NSA_TILE
    assert WINDOW == 2 * tq
    nq = seq // tq
    gw = HEADS_PER_GROUP * HEAD_DIM
    rows = HEADS_PER_GROUP * tq
    per_bg = lambda *shape: pl.BlockSpec((1, 1) + shape, lambda b, g, i: (b, g) + (0,) * len(shape))
    return pl.pallas_call(
        functools.partial(_nsa_body, tq=tq, ncp=ncp, topk=topk),
        grid=(batch, KV_GROUPS, nq),
        in_specs=[pl.BlockSpec((tq, gw), lambda b, g, i: (b * nq + i, g)),
                  per_bg(ncp, HEAD_DIM), per_bg(CAT_ROWS, ncp),
                  per_bg(seq, LANES), per_bg(nq, V_ROWS, tq), per_bg(seq, HEAD_DIM),
                  per_bg(nq, V_ROWS, tq),
                  pl.BlockSpec((tq, gw), lambda b, g, i: (b * nq + i, g)),
                  pl.BlockSpec((tq, LANES), lambda b, g, i: (b * nq + i, 0))],
        out_specs=pl.BlockSpec((tq, gw), lambda b, g, i: (b * nq + i, g)),
        out_shape=jax.ShapeDtypeStruct((batch * seq, NSA_WIDTH), BF16),
        scratch_shapes=[pltpu.VMEM((rows, LANES), BF16),
                        pltpu.VMEM((2, 1, rows), F32),
                        pltpu.VMEM((2, V_ROWS, rows), F32),
                        pltpu.VMEM((HEAD_DIM, rows), F32),
                        pltpu.VMEM((2, tq, rows), F32), pltpu.VMEM((2, 1, rows), F32),
                        pltpu.VMEM((2 * HEADS_PER_GROUP, tq), F32), pltpu.VMEM((tq, gw), F32)],
        compiler_params=pltpu.CompilerParams(
            dimension_semantics=("parallel", "parallel", "arbitrary"),
            vmem_limit_bytes=VMEM_LIMIT),
        name="nsa",
    )(q, kc, cat, ksa, vst, kw, vwt, ga, sm)


def _ssdout_body(x_ref, oa_ref, qx_ref, gx_ref, k_ref, v_ref, w_ref, g_ref,
                 xbc_ref, z_ref, sm_ref, dtt_ref, cw_ref, cb_ref, dtb_ref, dtbt_ref, al_ref, alt_ref,
                 dsk_ref, gn_ref, eh_ref, ehw_ref, tril_ref, triu_ref,
                 o_ref, acc_ref, mix_ref, ext_ref, xc_ref, y_ref, h_ref, *, ts, tiles_per_seq):
    L = SSD_CHUNK
    N = SSD_STATE
    P = SSD_HEAD_DIM
    E = SSD_HEADS // SSD_GROUPS
    gw = E * P
    pad = 8
    w_b = NSA_WIDTH
    w_c = NSA_WIDTH + SSD_WIDTH

    @pl.when(pl.program_id(0) % tiles_per_seq == 0)
    def _():
        ext_ref[0:pad, :] = jnp.zeros((pad, CONV_DIM), F32)
        h_ref[...] = jnp.zeros(h_ref.shape, F32)

    acc_ref[...] = x_ref[...] + _dot(oa_ref[...], w_ref[0:w_b, :])

    ext_ref[pad:pad + ts, :] = xbc_ref[...]
    assert SSD_CONV == 4
    ext = ext_ref[...]
    ext1 = pltpu.roll(ext, 1, 0)
    near = ext * cw_ref[3:4, :] + ext1 * cw_ref[2:3, :]
    far = ext * cw_ref[1:2, :] + ext1 * cw_ref[0:1, :]
    taps = near + pltpu.roll(far, 2, 0)
    xc_ref[...] = _silu(cb_ref[...] + taps[pad:pad + ts, :])
    ext_ref[0:pad, :] = ext_ref[ts:ts + pad, :]

    a_row = -jnp.exp(al_ref[...]) * LOG2E
    a_col = -jnp.exp(alt_ref[...]) * LOG2E
    causal = lax.broadcasted_iota(I32, (L, L), 1) <= lax.broadcasted_iota(I32, (L, L), 0)
    low = lax.broadcasted_iota(I32, (ts, LANES), 1) < P
    nck = ts // L
    chunk = lambda c: slice(c * L, (c + 1) * L)

    xs = xc_ref[:, 0:SSD_WIDTH]
    dt = jax.nn.softplus(sm_ref[...] + dtb_ref[...])
    da = dt * a_row
    a_cs = jnp.concatenate([_dot2_r(tril_ref[...], da[chunk(c), :]) for c in range(nck)], axis=0)
    dt_x = _dot2_l(dt, eh_ref[...])
    acs_w = _dot2_l(a_cs, ehw_ref[...])
    acs_x = jnp.concatenate(
        [jnp.where(low, acs_w[:, (2 * n) * LANES:(2 * n + 1) * LANES],
                   acs_w[:, (2 * n + 1) * LANES:(2 * n + 2) * LANES]) for n in range(SSD_HEADS // 2)],
        axis=1)
    a_last = [acs_x[c * L + L - 1:c * L + L, :] for c in range(nck)]
    a_last_x = jnp.concatenate([jnp.broadcast_to(a, (L, SSD_WIDTH)) for a in a_last], axis=0)
    dat = jax.nn.softplus(dtt_ref[0, DT_LANE0:DT_LANE0 + SSD_HEADS, :] + dtbt_ref[...]) * a_col
    acs_t = [_dot2_l(dat[:, chunk(c)], triu_ref[...]) for c in range(nck)]
    xdt = xs * dt_x
    xdo = (xdt * jnp.exp2(a_last_x - acs_x)).astype(BF16)
    xdt_b = xdt.astype(BF16)
    y_ref[...] = xs * dsk_ref[...]
    pre = jnp.exp2(acs_x)

    def attn_head(h):
        hs = slice(h * XA_HEAD_DIM, (h + 1) * XA_HEAD_DIM)
        s = _nt(qx_ref[:, hs], k_ref[0, :, hs])
        p = jnp.exp2(s - jnp.max(s, axis=-1, keepdims=True))
        inv = 1.0 / jnp.sum(p, axis=-1, keepdims=True)
        oc = _dot(p.astype(BF16), v_ref[0, :, hs]) * (inv * _silu(gx_ref[:, hs]))
        mix_ref[:, SSD_WIDTH + h * XA_HEAD_DIM:SSD_WIDTH + (h + 1) * XA_HEAD_DIM] = oc.astype(BF16)

    for gi in range(SSD_GROUPS):
        gs = slice(gi * gw, (gi + 1) * gw)
        bm = xc_ref[:, SSD_WIDTH + gi * N:SSD_WIDTH + (gi + 1) * N]
        cm_b = xc_ref[:, SSD_WIDTH + SSD_GROUPS * N + gi * N:SSD_WIDTH + SSD_GROUPS * N + (gi + 1) * N].astype(BF16)
        bm_b = bm.astype(BF16)
        h_c = h_ref[gi]
        h_in = []
        for c in range(nck):
            h_in.append(h_c.astype(BF16))
            h_c = h_c * jnp.exp2(a_last[c][:, gs]) + _dot(bm[chunk(c), :].T.astype(BF16), xdo[chunk(c), gs])
        h_ref[gi] = h_c
        for c in range(nck):
            rs = chunk(c)
            cbm = _nt(cm_b[rs, :], bm_b[rs, :])
            y_ref[rs, gs] += _dot(cm_b[rs, :], h_in[c]) * pre[rs, gs]
            for e in range(E):
                h = gi * E + e
                hs = slice(h * P, (h + 1) * P)
                d = acs_w[rs, h * LANES:(h + 1) * LANES] - acs_t[c][h:h + 1, :]
                dec = jnp.exp2(jnp.where(causal, d, NEG))
                y_ref[rs, hs] += _dot((cbm * dec).astype(BF16), xdt_b[rs, hs])

    y = y_ref[...] * _silu(z_ref[...])
    ms = jnp.mean(y * y, axis=-1, keepdims=True)
    mix_ref[:, 0:SSD_WIDTH] = (y * lax.rsqrt(ms + EPS) * gn_ref[...]).astype(BF16)
    acc_ref[...] += _dot(mix_ref[:, 0:SSD_WIDTH], w_ref[w_b:w_c, :])

    for h in range(XA_HEADS):
        attn_head(h)
    acc = acc_ref[...] + _dot(mix_ref[:, SSD_WIDTH:], w_ref[w_c:, :])
    ms = jnp.mean(acc * acc, axis=-1, keepdims=True)
    o_ref[...] = acc * lax.rsqrt(ms + EPS) * g_ref[...]


def _ssd_out(x2, oa, qx, gx, km, vm, w_out, g_final, xbc, z, sm, smt, consts, *, batch, seq):
    rows = batch * seq
    ts = ROW_TILE
    tps = seq // ts
    mlen = km.shape[1]
    row = lambda n: pl.BlockSpec((ts, n), lambda r: (r, 0))
    full = lambda a: pl.BlockSpec(a.shape, lambda r: (0,) * a.ndim)
    mem = pl.BlockSpec((1, mlen, XA_WIDTH), lambda r: (r // tps, 0, 0))
    return pl.pallas_call(
        functools.partial(_ssdout_body, ts=ts, tiles_per_seq=tps),
        grid=(rows // ts,),
        in_specs=[row(D_MODEL), row(NSA_WIDTH), row(XA_WIDTH), row(XA_WIDTH), mem, mem,
                  full(w_out), full(g_final),
                  row(CONV_DIM), row(SSD_WIDTH), row(LANES), pl.BlockSpec((1, LANES, ts), lambda r: (r, 0, 0))]
                 + [full(a) for a in consts],
        out_specs=row(D_MODEL),
        out_shape=jax.ShapeDtypeStruct((rows, D_MODEL), F32),
        scratch_shapes=[pltpu.VMEM((ts, D_MODEL), F32), pltpu.VMEM((ts, SSD_WIDTH + XA_WIDTH), BF16),
                        pltpu.VMEM((ts + 8, CONV_DIM), F32), pltpu.VMEM((ts, CONV_DIM), F32),
                        pltpu.VMEM((ts, SSD_WIDTH), F32),
                        pltpu.VMEM((SSD_GROUPS, SSD_STATE, SSD_WIDTH // SSD_GROUPS), F32)],
        compiler_params=pltpu.CompilerParams(dimension_semantics=("arbitrary",),
                                             vmem_limit_bytes=VMEM_LIMIT),
        name="ssd_out",
    )(x2, oa, qx, gx, km, vm, w_out, g_final, xbc, z, sm, smt, *consts)


def _memkv_body(mem_ref, g_ref, w_ref, k_ref, v_ref):
    x = mem_ref[0]
    ms = jnp.mean(x * x, axis=-1, keepdims=True)
    xn = (x * lax.rsqrt(ms + EPS) * g_ref[...]).astype(BF16)
    k_ref[0] = _dot(xn, w_ref[:, 0:XA_WIDTH]).astype(BF16)
    v_ref[0] = _dot(xn, w_ref[:, XA_WIDTH:2 * XA_WIDTH]).astype(BF16)


def _mem_kv(mem, g_mem, w_kv):
    batch, mlen, _ = mem.shape
    full = lambda a: pl.BlockSpec(a.shape, lambda b: (0,) * a.ndim)
    out = pl.BlockSpec((1, mlen, XA_WIDTH), lambda b: (b, 0, 0))
    shp = jax.ShapeDtypeStruct((batch, mlen, XA_WIDTH), BF16)
    return pl.pallas_call(
        _memkv_body,
        grid=(batch,),
        in_specs=[pl.BlockSpec((1, mlen, D_MODEL), lambda b: (b, 0, 0)), full(g_mem), full(w_kv)],
        out_specs=[out, out],
        out_shape=[shp, shp],
        compiler_params=pltpu.CompilerParams(dimension_semantics=("parallel",),
                                             vmem_limit_bytes=VMEM_LIMIT),
        name="mem_kv",
    )(mem, g_mem, w_kv)


def _permute_w_in(w, layer):
    rb = 128
    return pl.pallas_call(
        _wprep_body,
        grid=(w.shape[1] // rb,),
        in_specs=[pl.BlockSpec((1, rb, w.shape[2]), lambda r: (layer, r, 0))],
        out_specs=pl.BlockSpec((rb, N_PROJ), lambda r: (r, 0)),
        out_shape=jax.ShapeDtypeStruct((w.shape[1], N_PROJ), BF16),
        compiler_params=pltpu.CompilerParams(dimension_semantics=("parallel",),
                                             vmem_limit_bytes=VMEM_LIMIT),
        name="w_prep",
    )(w)


def _wprep_body(w_ref, o_ref):
    sizes = [NSA_WIDTH] + [KV_WIDTH] * 6 + [NSA_HEADS * 3, NSA_WIDTH, SSD_WIDTH, CONV_DIM, SSD_HEADS,
                                          XA_WIDTH, XA_WIDTH]
    offs = [0]
    for s in sizes:
        offs.append(offs[-1] + s)
    w = w_ref[0]

    def move(dst, lo, hi):
        o_ref[:, dst:dst + hi - lo] = w[:, lo:hi].astype(BF16)

    move(OFF_Q, offs[0], offs[7])
    move(OFF_GA, offs[8], offs[11])
    move(OFF_QX, offs[12], offs[14])
    small = jnp.concatenate(
        [w[:, offs[7]:offs[8]], w[:, offs[11]:offs[12]],
         jnp.zeros((w.shape[0], LANES - NSA_HEADS * 3 - SSD_HEADS), w.dtype)], axis=1)
    o_ref[:, OFF_SM:OFF_SM + LANES] = small.astype(BF16)


def _lane_row(vals, lane0):
    return jnp.zeros((1, LANES), F32).at[0, lane0:lane0 + vals.shape[0]].set(vals.astype(F32))


def _forward(x, mem, positions, g_in, w_in, cmp_pos_k, w_cmp1_k, w_cmp2_k, cmp_pos_v, w_cmp1_v,
             w_cmp2_v, conv_w, conv_b, dt_bias, a_log, d_skip, g_ssd_norm, g_mem, w_mem_kv, w_out,
             g_final):
    batch, seq, _ = x.shape
    ncp = seq // CMP_STRIDE
    n_slc = seq // SLC_BLOCK
    assert n_slc <= SLC_SLOTS and seq % ROW_TILE == 0 and ncp <= NSA_TILE
    topk = min(SLC_TOPK, n_slc)
    rows = batch * seq

    inv = ROPE_THETA ** (-jnp.arange(0, ROPE_DIM, 2, dtype=F32) / ROPE_DIM)
    head_inv = jnp.concatenate([inv, inv, jnp.zeros((HEAD_DIM - ROPE_DIM,), F32)])
    invl = jnp.tile(head_inv, LANES // HEAD_DIM)[None, :]
    head_sgn = jnp.concatenate([-jnp.ones((ROPE_HALF,), F32), jnp.ones((ROPE_HALF,), F32),
                                jnp.zeros((HEAD_DIM - ROPE_DIM,), F32)])
    sgn = jnp.tile(head_sgn, LANES // HEAD_DIM)[None, :]
    inv_rows = jnp.concatenate([inv, jnp.zeros((ROPE_FREQ_ROWS - ROPE_HALF,), F32)])[:, None]
    d_ix = jnp.arange(LANES) % HEAD_DIM
    freq_of_lane = jnp.where(d_ix < ROPE_DIM, d_ix % ROPE_HALF, ROPE_HALF)
    spread_cos = (freq_of_lane[:, None] == jnp.arange(ROPE_FREQ_ROWS)[None, :]).astype(F32)
    spread_sin = spread_cos * sgn[0][:, None]

    x2 = x.reshape(rows, D_MODEL)
    pos3 = positions.reshape(rows // ROW_TILE, 1, ROW_TILE)
    h = x2
    for l in range(g_in.shape[0]):
        (q, ck, cv, ksa, vst, kw, vwt, ga, z, xbc, qx, gx, sm, smt) = _in_proj(
            h, pos3, g_in[l][None, :], _permute_w_in(w_in, l), inv_rows,
            spread_cos.astype(BF16), spread_sin.astype(BF16), batch=batch, seq=seq)

        cmp_end = jnp.minimum(jnp.arange(ncp) * CMP_STRIDE + CMP_LEN - 1, seq - 1)
        posc = positions[:, cmp_end][:, :, None]
        pad_w2 = lambda w: jnp.pad(w, ((0, 0), (0, LANES - HEAD_DIM))).astype(BF16)
        pos_rows = lambda p: jnp.broadcast_to(p.reshape(1, CMP_LEN * HEAD_DIM), (8, CMP_LEN * HEAD_DIM)).astype(BF16)
        kc, vc = _compress(ck, cv, posc,
                           w_cmp1_k[l].astype(BF16), pad_w2(w_cmp2_k[l]), pos_rows(cmp_pos_k[l]),
                           w_cmp1_v[l].astype(BF16), pad_w2(w_cmp2_v[l]), pos_rows(cmp_pos_v[l]),
                           invl, sgn, batch=batch, ncp=ncp)
        n_ix = jnp.arange(ncp)[None, :]
        j_ix = jnp.arange(SLC_SLOTS)[:, None]
        ovl = ((n_ix * CMP_STRIDE < j_ix * SLC_BLOCK + SLC_BLOCK)
               & (n_ix * CMP_STRIDE + CMP_LEN > j_ix * SLC_BLOCK)
               & (n_ix < ncp - (CMP_LEN // CMP_STRIDE - 1))).astype(BF16)
        bg = (batch, KV_GROUPS)
        cat = jnp.concatenate(
            [jnp.swapaxes(vc, 2, 3), jnp.ones(bg + (1, ncp), BF16),
             jnp.zeros(bg + (CAT_OVL_ROW0 - HEAD_DIM - 1, ncp), BF16),
             jnp.broadcast_to(ovl, bg + ovl.shape),
             jnp.zeros(bg + (CAT_ROWS - CAT_OVL_ROW0 - SLC_SLOTS, ncp), BF16)], axis=2)
        o_a = _nsa(q, kc, cat, ksa, vst, kw, vwt, ga, sm,
                   batch=batch, seq=seq, ncp=ncp, topk=topk)

        head_of_lane = jnp.arange(SSD_WIDTH) // SSD_HEAD_DIM
        k_ix = jnp.arange(LANES)[:, None]
        eh = (k_ix == DT_LANE0 + head_of_lane[None, :]).astype(BF16)
        ehw = (k_ix == DT_LANE0 + (jnp.arange(SSD_HEADS * LANES) // LANES)[None, :]).astype(BF16)
        t_ix = jnp.arange(SSD_CHUNK)
        tril = (t_ix[None, :] <= t_ix[:, None]).astype(BF16)
        ssd_consts = [conv_w[l], conv_b[l][None, :],
                      _lane_row(dt_bias[l], DT_LANE0), dt_bias[l].astype(F32)[:, None],
                      _lane_row(a_log[l], DT_LANE0), a_log[l].astype(F32)[:, None],
                      jnp.repeat(d_skip[l].astype(F32), SSD_HEAD_DIM)[None, :], g_ssd_norm[l][None, :],
                      eh, ehw, tril, tril.T]

        km, vm = _mem_kv(mem, g_mem[l][None, :], w_mem_kv[l].astype(BF16))
        assert g_in.shape[0] == 1, "the final RMSNorm is fused into the (single) layer's last kernel"
        h = _ssd_out(h, o_a, qx, gx, km, vm, w_out[l].astype(BF16), g_final[None, :],
                     xbc, z, sm, smt, ssd_consts, batch=batch, seq=seq)
    return h.reshape(batch, seq, D_MODEL)


def kernel(x, mem, positions, g_in, w_in, cmp_pos_k, w_cmp1_k, w_cmp2_k, cmp_pos_v, w_cmp1_v, w_cmp2_v,
           conv_w, conv_b, dt_bias, a_log, d_skip, g_ssd_norm, g_mem, w_mem_kv, w_out, g_final):
    return _forward(x, mem, positions, g_in, w_in, cmp_pos_k, w_cmp1_k, w_cmp2_k, cmp_pos_v, w_cmp1_v,
                    w_cmp2_v, conv_w, conv_b, dt_bias, a_log, d_skip, g_ssd_norm, g_mem, w_mem_kv,
                    w_out, g_final)
```

```python
import functools

import jax
import jax.numpy as jnp
from jax import lax
from jax.experimental import pallas as pl
from jax.experimental.pallas import tpu as pltpu

F32 = jnp.float32
BF16 = jnp.bfloat16
I32 = jnp.int32

D_MODEL = 1024
NSA_HEADS = 8
HEAD_DIM = 64
KV_GROUPS = 2
HEADS_PER_GROUP = NSA_HEADS // KV_GROUPS
NSA_WIDTH = NSA_HEADS * HEAD_DIM
KV_WIDTH = KV_GROUPS * HEAD_DIM
CMP_LEN = 32
CMP_STRIDE = 16
CMP_HIDDEN = 256
SLC_BLOCK = 64
SLC_TOPK = 16
SLC_SLOTS = 64
WINDOW = 512
BIG = 1e9
NEG = -1e30

SSD_HEADS = 8
SSD_HEAD_DIM = 64
SSD_WIDTH = SSD_HEADS * SSD_HEAD_DIM
SSD_GROUPS = 2
SSD_STATE = 128
SSD_CONV = 4
SSD_CHUNK = 128
CONV_DIM = SSD_WIDTH + 2 * SSD_GROUPS * SSD_STATE

XA_HEADS = 4
XA_HEAD_DIM = 128
XA_WIDTH = XA_HEADS * XA_HEAD_DIM
MIX_WIDTH = NSA_WIDTH + SSD_WIDTH + XA_WIDTH

ROPE_THETA = 500000.0
ROPE_DIM = HEAD_DIM // 4
ROPE_HALF = ROPE_DIM // 2
ROPE_FREQ_ROWS = 16
EPS = 1e-6

LANES = 128
GATE_LANE0 = 0
DT_LANE0 = 24

OFF_Q = 0
OFF_KV = OFF_Q + NSA_WIDTH
OFF_GA = OFF_KV + 6 * KV_WIDTH
OFF_Z = OFF_GA + NSA_WIDTH
OFF_XBC = OFF_Z + SSD_WIDTH
OFF_QX = OFF_XBC + CONV_DIM
OFF_GX = OFF_QX + XA_WIDTH
OFF_SM = OFF_GX + XA_WIDTH
N_PROJ = OFF_SM + LANES

ROW_TILE = 512
NSA_TILE = 256
V_ROWS = 80
LOG2E = 1.4426950408889634
CAT_OVL_ROW0 = 80
CAT_ROWS = 144
VMEM_LIMIT = 56 * 1024 * 1024


def _nt(a, b):
    return lax.dot_general(a, b, (((1,), (1,)), ((), ())), preferred_element_type=F32)


def _dot(a, b):
    return jnp.dot(a, b, preferred_element_type=F32)


def _split2(x):
    hi = x.astype(BF16)
    return hi, (x - hi.astype(F32)).astype(BF16)


def _dot2_l(x, w):
    hi, lo = _split2(x)
    return _dot(hi, w) + _dot(lo, w)


def _dot2_r(w, x):
    hi, lo = _split2(x)
    return _dot(w, hi) + _dot(w, lo)


def _silu(x):
    h = 0.5 * x
    return h + h * jnp.tanh(h)


def _rope(a, cs, sn, first):
    r = jnp.where(first, pltpu.roll(a, LANES - ROPE_HALF, 1), pltpu.roll(a, ROPE_HALF, 1))
    return a * cs + r * sn


def _rope_tables(pos_f32, invl, sgn):
    ang = pos_f32 * invl
    return jnp.cos(ang), jnp.sin(ang) * sgn


def _inproj_body(x_ref, pos_ref, g_ref, w_ref, invc_ref, ec_ref, es_ref,
                 q_ref, ck_ref, cv_ref, ksa_ref, vst_ref, kw_ref, vwt_ref,
                 ga_ref, z_ref, xbc_ref, qx_ref, gx_ref, sm_ref, smt_ref, sk_ref, sv_ref, *, tm, tiles_per_seq):
    x = x_ref[...]
    ms = jnp.mean(x * x, axis=-1, keepdims=True)
    xn = (x * lax.rsqrt(ms + EPS) * g_ref[...]).astype(BF16)

    def mm(lo, n):
        return _dot(xn, w_ref[:, lo:lo + n])

    ga_ref[...] = mm(OFF_GA, NSA_WIDTH)
    z_ref[...] = mm(OFF_Z, SSD_WIDTH)

    ang = invc_ref[...] * pos_ref[0].astype(F32)
    cs = _dot2_r(ec_ref[...], jnp.cos(ang)).T
    sn = _dot2_r(es_ref[...], jnp.sin(ang)).T
    lane = lax.broadcasted_iota(I32, (tm, LANES), 1)
    first = (lane % HEAD_DIM) < ROPE_HALF

    qf = mm(OFF_Q, NSA_WIDTH)
    for c in range(NSA_WIDTH // LANES):
        sl = slice(c * LANES, (c + 1) * LANES)
        q_ref[:, sl] = (_rope(qf[:, sl], cs, sn, first) * (HEAD_DIM ** -0.5 * LOG2E)).astype(BF16)

    kv = mm(OFF_KV, 6 * KV_WIDTH)
    k_c, v_c, k_s, v_s, k_w, v_w = [kv[:, n * KV_WIDTH:(n + 1) * KV_WIDTH] for n in range(6)]
    k_s = _rope(k_s, cs, sn, first)
    k_w = _rope(k_w, cs, sn, first)
    s_base = (pl.program_id(0) % tiles_per_seq) * tm
    blk = (s_base + lax.broadcasted_iota(I32, (tm, LANES), 0)) // SLC_BLOCK
    onehot = jnp.where(lane - HEAD_DIM == blk, 1.0, 0.0)
    low = lane < HEAD_DIM
    vs_t = v_s.T
    vw_t = v_w.T
    tail = jnp.where(lax.broadcasted_iota(I32, (V_ROWS - HEAD_DIM, NSA_TILE), 0) == 0, 1.0, 0.0).astype(BF16)
    for g in range(KV_GROUPS):
        gs = slice(g * HEAD_DIM, (g + 1) * HEAD_DIM)
        kw_ref[0, g] = k_w[:, gs].astype(BF16)
        ks_g = k_s if g == 0 else pltpu.roll(k_s, HEAD_DIM, 1)
        ksa_ref[0, g] = jnp.where(low, ks_g, onehot).astype(BF16)
        for c in range(tm // NSA_TILE):
            cols = slice(c * NSA_TILE, (c + 1) * NSA_TILE)
            vst_ref[0, g, c, 0:HEAD_DIM, :] = vs_t[gs, cols].astype(BF16)
            vst_ref[0, g, c, HEAD_DIM:V_ROWS, :] = tail
            vwt_ref[0, g, c, 0:HEAD_DIM, :] = vw_t[gs, cols].astype(BF16)
            vwt_ref[0, g, c, HEAD_DIM:V_ROWS, :] = tail

    sk_ref[...] = k_c
    sv_ref[...] = v_c
    nchunk = tm // CMP_STRIDE
    for t in range(CMP_STRIDE):
        dst = slice(t * HEAD_DIM, (t + 1) * HEAD_DIM)
        kt = sk_ref[pl.ds(t, nchunk, stride=CMP_STRIDE), :].astype(BF16)
        vt = sv_ref[pl.ds(t, nchunk, stride=CMP_STRIDE), :].astype(BF16)
        for g in range(KV_GROUPS):
            gs = slice(g * HEAD_DIM, (g + 1) * HEAD_DIM)
            ck_ref[0, g, :, dst] = kt[:, gs]
            cv_ref[0, g, :, dst] = vt[:, gs]

    xbc_ref[...] = mm(OFF_XBC, CONV_DIM)
    qx_ref[...] = (mm(OFF_QX, XA_WIDTH) * (XA_HEAD_DIM ** -0.5 * LOG2E)).astype(BF16)
    gx_ref[...] = mm(OFF_GX, XA_WIDTH)
    sm = mm(OFF_SM, LANES)
    sm_ref[...] = sm
    smt_ref[0] = sm.T


def _in_proj(x2, pos3, g_in, w_p, invc, ec, es, *, batch, seq):
    rows = batch * seq
    tm = ROW_TILE
    tps = seq // tm
    row = lambda n: pl.BlockSpec((tm, n), lambda r: (r, 0))
    full = lambda a: pl.BlockSpec(a.shape, lambda r: (0,) * a.ndim)
    grp = lambda n: pl.BlockSpec((1, KV_GROUPS, tm, n), lambda r: (r // tps, 0, r % tps, 0))
    grp_shape = lambda n: jax.ShapeDtypeStruct((batch, KV_GROUPS, seq, n), BF16)
    chunk = pl.BlockSpec((1, KV_GROUPS, tm // CMP_STRIDE, CMP_STRIDE * HEAD_DIM),
                         lambda r: (r // tps, 0, r % tps, 0))
    chunk_shape = jax.ShapeDtypeStruct((batch, KV_GROUPS, seq // CMP_STRIDE, CMP_STRIDE * HEAD_DIM), BF16)
    vtile = pl.BlockSpec((1, KV_GROUPS, tm // NSA_TILE, V_ROWS, NSA_TILE), lambda r: (r // tps, 0, r % tps, 0, 0))
    vtile_shape = jax.ShapeDtypeStruct((batch, KV_GROUPS, seq // NSA_TILE, V_ROWS, NSA_TILE), BF16)
    flat = lambda n, dt: jax.ShapeDtypeStruct((rows, n), dt)
    return pl.pallas_call(
        functools.partial(_inproj_body, tm=tm, tiles_per_seq=tps),
        grid=(rows // tm,),
        in_specs=[row(D_MODEL), pl.BlockSpec((1, 1, tm), lambda r: (r, 0, 0)),
                  full(g_in), full(w_p), full(invc), full(ec), full(es)],
        out_specs=[row(NSA_WIDTH), chunk, chunk, grp(LANES), vtile, grp(HEAD_DIM), vtile,
                   row(NSA_WIDTH), row(SSD_WIDTH), row(CONV_DIM),
                   row(XA_WIDTH), row(XA_WIDTH), row(LANES),
                   pl.BlockSpec((1, LANES, tm), lambda r: (r, 0, 0))],
        out_shape=[flat(NSA_WIDTH, BF16), chunk_shape, chunk_shape, grp_shape(LANES), vtile_shape,
                   grp_shape(HEAD_DIM), vtile_shape,
                   flat(NSA_WIDTH, F32), flat(SSD_WIDTH, F32), flat(CONV_DIM, F32),
                   flat(XA_WIDTH, BF16), flat(XA_WIDTH, F32), flat(LANES, F32),
                   jax.ShapeDtypeStruct((rows // tm, LANES, tm), F32)],
        scratch_shapes=[pltpu.VMEM((tm, KV_WIDTH), F32), pltpu.VMEM((tm, KV_WIDTH), F32)],
        compiler_params=pltpu.CompilerParams(dimension_semantics=("parallel",),
                                             vmem_limit_bytes=VMEM_LIMIT),
        name="in_proj",
    )(x2, pos3, g_in, w_p, invc, ec, es)


def _compress_body(ck_ref, cv_ref, posc_ref, w1k_ref, w2k_ref, pk_ref, w1v_ref, w2v_ref, pv_ref,
                   invl_ref, sgn_ref, kc_ref, vc_ref, *, ncp):
    half = CMP_STRIDE * HEAD_DIM

    def mlp(c_ref, w1_ref, w2_ref, p_ref):
        c = c_ref[0, 0]
        a = _dot(c, w1_ref[0:half, :])
        b = _dot(c, w1_ref[half:2 * half, :])
        bias = _dot(p_ref[...], w1_ref[...])[0:1, :]
        h = a + pltpu.roll(b, ncp - 1, 0) + bias
        return _dot(_silu(h).astype(BF16), w2_ref[...])

    kc = mlp(ck_ref, w1k_ref, w2k_ref, pk_ref)
    vc = mlp(cv_ref, w1v_ref, w2v_ref, pv_ref)
    cs, sn = _rope_tables(posc_ref[0].astype(F32), invl_ref[...], sgn_ref[...])
    lane = lax.broadcasted_iota(I32, (ncp, LANES), 1)
    kc = _rope(kc, cs, sn, (lane % HEAD_DIM) < ROPE_HALF)
    kc_ref[0, 0] = kc[:, 0:HEAD_DIM].astype(BF16)
    vc_ref[0, 0] = vc[:, 0:HEAD_DIM].astype(BF16)


def _compress(ck, cv, posc, w1k, w2k, pk, w1v, w2v, pv, invl, sgn, *, batch, ncp):
    chunk = pl.BlockSpec((1, 1, ncp, CMP_STRIDE * HEAD_DIM), lambda b, g: (b, g, 0, 0))
    full = lambda a: pl.BlockSpec(a.shape, lambda b, g: (0,) * a.ndim)
    out = pl.BlockSpec((1, 1, ncp, HEAD_DIM), lambda b, g: (b, g, 0, 0))
    shp = jax.ShapeDtypeStruct((batch, KV_GROUPS, ncp, HEAD_DIM), BF16)
    return pl.pallas_call(
        functools.partial(_compress_body, ncp=ncp),
        grid=(batch, KV_GROUPS),
        in_specs=[chunk, chunk, pl.BlockSpec((1, ncp, 1), lambda b, g: (b, 0, 0)),
                  full(w1k), full(w2k), full(pk), full(w1v), full(w2v), full(pv),
                  full(invl), full(sgn)],
        out_specs=[out, out],
        out_shape=[shp, shp],
        compiler_params=pltpu.CompilerParams(dimension_semantics=("parallel", "parallel"),
                                             vmem_limit_bytes=VMEM_LIMIT),
        name="compress",
    )(ck, cv, posc, w1k, w2k, pk, w1v, w2v, pv, invl, sgn)


def _nsa_body(q_ref, kc_ref, cat_ref, ksa_ref, vst_ref, kw_ref, vwt_ref, ga_ref, sm_ref,
              o_ref, qa_ref, m_ref, acc_ref, oc_ref, sc_ref, mc_ref, gt_ref, gs_ref, *, tq, ncp, topk):
    g = pl.program_id(1)
    i = pl.program_id(2)
    s0 = i * tq
    nh = HEADS_PER_GROUP
    rows = nh * tq

    def col_max(s):
        n = s.shape[0]
        while n > 8:
            n //= 2
            s = jnp.maximum(s[0:n], s[n:2 * n])
        return jnp.max(s, axis=0, keepdims=True)

    def reset(br):
        m_ref[br] = jnp.full((1, rows), NEG, F32)
        acc_ref[br] = jnp.zeros((V_ROWS, rows), F32)

    def consume(br, s, vt_tile, mask, s_max=None):
        if mask is not None:
            s = jnp.where(mask, s, NEG)
        if s_max is None:
            s_max = col_max(s)
        m_prev = m_ref[br]
        m_new = jnp.maximum(m_prev, s_max)
        alpha = jnp.exp2(m_prev - m_new)
        p = jnp.exp2(s - m_new).astype(BF16)
        acc_ref[br] = alpha * acc_ref[br] + _dot(vt_tile, p)
        m_ref[br] = m_new

    def sel_scores(j):
        rows_j = pl.ds(pl.multiple_of(j * tq, tq), tq)
        return _nt(ksa_ref[0, 0, rows_j, :], qa_ref[...])

    def win_scores(j):
        rows_j = pl.ds(pl.multiple_of(j * tq, tq), tq)
        return _nt(kw_ref[0, 0, rows_j, :], qa_ref[:, 0:HEAD_DIM])

    q4 = q_ref[...]
    for e in range(nh):
        qa_ref[e * tq:(e + 1) * tq, 0:HEAD_DIM] = q4[:, e * HEAD_DIM:(e + 1) * HEAD_DIM]

    n_io = lax.broadcasted_iota(I32, (ncp, rows), 0)
    t_io = s0 + lax.broadcasted_iota(I32, (ncp, rows), 1) % tq
    k_io = lax.broadcasted_iota(I32, (tq, rows), 0)
    t_io2 = lax.broadcasted_iota(I32, (tq, rows), 1) % tq
    diag = k_io <= t_io2
    j1 = jnp.maximum(i - 1, 0)
    j2 = jnp.maximum(i - 2, 0)
    s = jnp.where((n_io * CMP_STRIDE + (CMP_LEN - 1)) <= t_io, _nt(kc_ref[0, 0], qa_ref[:, 0:HEAD_DIM]), NEG)
    sc_ref[0, 0:ncp, :] = s
    mc_ref[0] = col_max(s)
    sc_ref[1] = win_scores(i)
    p = jnp.exp2(sc_ref[0, 0:ncp, :] - mc_ref[0]).astype(BF16)
    big = _dot(cat_ref[0, 0], p)
    sc_ref[0] = win_scores(j1)
    t_row = s0 + lax.broadcasted_iota(I32, (1, rows), 1) % tq
    inv = jnp.where(t_row >= CMP_LEN - 1, 1.0 / jnp.maximum(big[HEAD_DIM:HEAD_DIM + 1, :], 1e-30), 0.0)
    oc_ref[...] = big[0:HEAD_DIM, :] * inv
    imp_x = big[CAT_OVL_ROW0:CAT_OVL_ROW0 + SLC_SLOTS, :] * inv
    imp = imp_x[:, 0:tq]
    for e in range(1, nh):
        imp = imp + imp_x[:, e * tq:(e + 1) * tq]

    j_io = lax.broadcasted_iota(I32, (SLC_SLOTS, tq), 0)
    cur = (s0 + lax.broadcasted_iota(I32, (SLC_SLOTS, tq), 1)) // SLC_BLOCK
    forced = (j_io == 0) | (j_io == cur) | (j_io == cur - 1)
    valid = j_io <= cur
    TAKEN = -3e38
    TAKEN_BELOW = -1e37
    j_f = j_io.astype(F32)
    v = jnp.where(forced, TAKEN, jnp.where(valid, imp, -BIG))
    for _ in range(topk - 3):
        mx = jnp.max(v, axis=0, keepdims=True)
        idx = jnp.min(jnp.where(v == mx, j_f, float(SLC_SLOTS)), axis=0, keepdims=True)
        v = jnp.where(j_f == idx, TAKEN, v)
    selb = jnp.where((v < TAKEN_BELOW) & valid, 0.0, NEG)
    selb_t = jnp.concatenate([jnp.zeros((SLC_SLOTS, tq), F32), selb], axis=0).T
    selb_t = selb_t.astype(BF16)

    reset(1)
    consume(1, sc_ref[1], vwt_ref[0, 0, i], diag)
    sc_ref[1] = win_scores(j2)
    consume(1, sc_ref[0], vwt_ref[0, 0, j1], i >= 1)
    consume(1, sc_ref[1], vwt_ref[0, 0, j2], (k_io > t_io2) & (i >= 2))
    for e in range(nh):
        qa_ref[e * tq:(e + 1) * tq, HEAD_DIM:LANES] = selb_t[:, HEAD_DIM:LANES]

    def produce(slot, j):
        s = sel_scores(j)
        sc_ref[slot] = s
        mc_ref[slot] = col_max(s)

    reset(0)
    produce(0, 0)
    gates_t = jax.nn.sigmoid(sm_ref[...]).T
    gs_ref[...] = _silu(ga_ref[...])

    def gate(e, c):
        c0 = GATE_LANE0 + e * 3 + c
        c1 = c0 + nh * 3
        return jnp.where(g == 0, gates_t[c0:c0 + 1, :], gates_t[c1:c1 + 1, :])

    for e in range(nh):
        cs = slice(e * tq, (e + 1) * tq)
        o_w = acc_ref[1, 0:HEAD_DIM, cs] / acc_ref[1, HEAD_DIM:HEAD_DIM + 1, cs]
        oc_ref[:, cs] = gate(e, 0) * oc_ref[:, cs] + gate(e, 2) * o_w
        gt_ref[e:e + 1, :] = gate(e, 1)

    def sel_pair(j):
        produce(1, j + 1)
        consume(0, sc_ref[0], vst_ref[0, 0, j], None, mc_ref[0])
        produce(0, j + 2)
        consume(0, sc_ref[1], vst_ref[0, 0, j + 1], None, mc_ref[1])

    done = 0
    for width in (8, 4, 2):
        start = done

        def block(jj, carry, width=width, start=start):
            for u in range(0, width, 2):
                sel_pair(start + width * jj + u)
            return carry

        trips = (i - start) // width
        lax.fori_loop(0, trips, block, 0)
        done = start + trips * width

    @pl.when(i % 2 == 0)
    def _():
        consume(0, sc_ref[0], vst_ref[0, 0, i], diag)

    @pl.when(i % 2 == 1)
    def _():
        sc_ref[1] = sel_scores(i)
        consume(0, sc_ref[0], vst_ref[0, 0, i - 1], None, mc_ref[0])
        consume(0, sc_ref[1], vst_ref[0, 0, i], diag)

    def head_out(e):
        cs = slice(e * tq, (e + 1) * tq)
        o_s = acc_ref[0, 0:HEAD_DIM, cs] / acc_ref[0, HEAD_DIM:HEAD_DIM + 1, cs]
        return oc_ref[:, cs] + gt_ref[e:e + 1, :] * o_s

    for pr in range(nh // 2):
        ls = slice(pr * LANES, (pr + 1) * LANES)
        pair = jnp.concatenate([head_out(2 * pr), head_out(2 * pr + 1)], axis=0).T
        o_ref[:, ls] = (pair * gs_ref[:, ls]).astype(BF16)


def _nsa(q, kc, cat, ksa, vst, kw, vwt, ga, sm, *, batch, seq, ncp, topk):
    tq = NSA_TILE
    assert WINDOW == 2 * tq
    nq = seq // tq
    gw = HEADS_PER_GROUP * HEAD_DIM
    rows = HEADS_PER_GROUP * tq
    per_bg = lambda *shape: pl.BlockSpec((1, 1) + shape, lambda b, g, i: (b, g) + (0,) * len(shape))
    return pl.pallas_call(
        functools.partial(_nsa_body, tq=tq, ncp=ncp, topk=topk),
        grid=(batch, KV_GROUPS, nq),
        in_specs=[pl.BlockSpec((tq, gw), lambda b, g, i: (b * nq + i, g)),
                  per_bg(ncp, HEAD_DIM), per_bg(CAT_ROWS, ncp),
                  per_bg(seq, LANES), per_bg(nq, V_ROWS, tq), per_bg(seq, HEAD_DIM),
                  per_bg(nq, V_ROWS, tq),
                  pl.BlockSpec((tq, gw), lambda b, g, i: (b * nq + i, g)),
                  pl.BlockSpec((tq, LANES), lambda b, g, i: (b * nq + i, 0))],
        out_specs=pl.BlockSpec((tq, gw), lambda b, g, i: (b * nq + i, g)),
        out_shape=jax.ShapeDtypeStruct((batch * seq, NSA_WIDTH), BF16),
        scratch_shapes=[pltpu.VMEM((rows, LANES), BF16),
                        pltpu.VMEM((2, 1, rows), F32),
                        pltpu.VMEM((2, V_ROWS, rows), F32),
                        pltpu.VMEM((HEAD_DIM, rows), F32),
                        pltpu.VMEM((2, tq, rows), F32), pltpu.VMEM((2, 1, rows), F32),
                        pltpu.VMEM((2 * HEADS_PER_GROUP, tq), F32), pltpu.VMEM((tq, gw), F32)],
        compiler_params=pltpu.CompilerParams(
            dimension_semantics=("parallel", "parallel", "arbitrary"),
            vmem_limit_bytes=VMEM_LIMIT),
        name="nsa",
    )(q, kc, cat, ksa, vst, kw, vwt, ga, sm)


def _ssdout_body(x_ref, oa_ref, qx_ref, gx_ref, k_ref, v_ref, w_ref, g_ref,
                 xbc_ref, z_ref, sm_ref, dtt_ref, cw_ref, cb_ref, dtb_ref, dtbt_ref, al_ref, alt_ref,
                 dsk_ref, gn_ref, eh_ref, ehw_ref, tril_ref, triu_ref,
                 o_ref, acc_ref, mix_ref, ext_ref, xc_ref, y_ref, h_ref, *, ts, tiles_per_seq):
    L = SSD_CHUNK
    N = SSD_STATE
    P = SSD_HEAD_DIM
    E = SSD_HEADS // SSD_GROUPS
    gw = E * P
    pad = 8
    w_b = NSA_WIDTH
    w_c = NSA_WIDTH + SSD_WIDTH

    @pl.when(pl.program_id(0) % tiles_per_seq == 0)
    def _():
        ext_ref[0:pad, :] = jnp.zeros((pad, CONV_DIM), F32)
        h_ref[...] = jnp.zeros(h_ref.shape, F32)

    def attn_head(h):
        hs = slice(h * XA_HEAD_DIM, (h + 1) * XA_HEAD_DIM)
        s = _nt(qx_ref[:, hs], k_ref[0, :, hs])
        p = jnp.exp2(s - jnp.max(s, axis=-1, keepdims=True))
        inv = 1.0 / jnp.sum(p, axis=-1, keepdims=True)
        oc = _dot(p.astype(BF16), v_ref[0, :, hs]) * (inv * _silu(gx_ref[:, hs]))
        mix_ref[:, SSD_WIDTH + h * XA_HEAD_DIM:SSD_WIDTH + (h + 1) * XA_HEAD_DIM] = oc.astype(BF16)

    ext_ref[pad:pad + ts, :] = xbc_ref[...]
    assert SSD_CONV == 4
    ext = ext_ref[...]
    ext1 = pltpu.roll(ext, 1, 0)
    near = ext * cw_ref[3:4, :] + ext1 * cw_ref[2:3, :]
    far = ext * cw_ref[1:2, :] + ext1 * cw_ref[0:1, :]
    taps = near + pltpu.roll(far, 2, 0)
    xc_ref[...] = _silu(cb_ref[...] + taps[pad:pad + ts, :])
    ext_ref[0:pad, :] = ext_ref[ts:ts + pad, :]

    for h in range(XA_HEADS):
        attn_head(h)
    acc_ref[...] = (x_ref[...] + _dot(oa_ref[...], w_ref[0:w_b, :])
                    + _dot(mix_ref[:, SSD_WIDTH:], w_ref[w_c:, :]))

    a_row = -jnp.exp(al_ref[...]) * LOG2E
    a_col = -jnp.exp(alt_ref[...]) * LOG2E
    causal = lax.broadcasted_iota(I32, (L, L), 1) <= lax.broadcasted_iota(I32, (L, L), 0)
    low = lax.broadcasted_iota(I32, (ts, LANES), 1) < P
    nck = ts // L
    chunk = lambda c: slice(c * L, (c + 1) * L)

    xs = xc_ref[:, 0:SSD_WIDTH]
    dt = jax.nn.softplus(sm_ref[...] + dtb_ref[...])
    da = dt * a_row
    a_cs = jnp.concatenate([_dot2_r(tril_ref[...], da[chunk(c), :]) for c in range(nck)], axis=0)
    dt_x = _dot2_l(dt, eh_ref[...])
    acs_w = _dot2_l(a_cs, ehw_ref[...])
    acs_x = jnp.concatenate(
        [jnp.where(low, acs_w[:, (2 * n) * LANES:(2 * n + 1) * LANES],
                   acs_w[:, (2 * n + 1) * LANES:(2 * n + 2) * LANES]) for n in range(SSD_HEADS // 2)],
        axis=1)
    a_last = [acs_x[c * L + L - 1:c * L + L, :] for c in range(nck)]
    a_last_x = jnp.concatenate([jnp.broadcast_to(a, (L, SSD_WIDTH)) for a in a_last], axis=0)
    dat = jax.nn.softplus(dtt_ref[0, DT_LANE0:DT_LANE0 + SSD_HEADS, :] + dtbt_ref[...]) * a_col
    acs_t = [_dot2_l(dat[:, chunk(c)], triu_ref[...]) for c in range(nck)]
    xdt = xs * dt_x
    xdo = (xdt * jnp.exp2(a_last_x - acs_x)).astype(BF16)
    xdt_b = xdt.astype(BF16)
    y_ref[...] = xs * dsk_ref[...]
    pre = jnp.exp2(acs_x)

    for gi in range(SSD_GROUPS):
        gs = slice(gi * gw, (gi + 1) * gw)
        bm = xc_ref[:, SSD_WIDTH + gi * N:SSD_WIDTH + (gi + 1) * N]
        cm_b = xc_ref[:, SSD_WIDTH + SSD_GROUPS * N + gi * N:SSD_WIDTH + SSD_GROUPS * N + (gi + 1) * N].astype(BF16)
        bm_b = bm.astype(BF16)
        h_c = h_ref[gi]
        h_in = []
        for c in range(nck):
            h_in.append(h_c.astype(BF16))
            h_c = h_c * jnp.exp2(a_last[c][:, gs]) + _dot(bm[chunk(c), :].T.astype(BF16), xdo[chunk(c), gs])
        h_ref[gi] = h_c
        for c in range(nck):
            rs = chunk(c)
            cbm = _nt(cm_b[rs, :], bm_b[rs, :])
            y_ref[rs, gs] += _dot(cm_b[rs, :], h_in[c]) * pre[rs, gs]
            for e in range(E):
                h = gi * E + e
                hs = slice(h * P, (h + 1) * P)
                d = acs_w[rs, h * LANES:(h + 1) * LANES] - acs_t[c][h:h + 1, :]
                dec = jnp.exp2(jnp.where(causal, d, NEG))
                y_ref[rs, hs] += _dot((cbm * dec).astype(BF16), xdt_b[rs, hs])

    y = y_ref[...] * _silu(z_ref[...])
    ms = jnp.mean(y * y, axis=-1, keepdims=True)
    mix_ref[:, 0:SSD_WIDTH] = (y * lax.rsqrt(ms + EPS) * gn_ref[...]).astype(BF16)
    acc = acc_ref[...] + _dot(mix_ref[:, 0:SSD_WIDTH], w_ref[w_b:w_c, :])
    ms = jnp.mean(acc * acc, axis=-1, keepdims=True)
    o_ref[...] = acc * lax.rsqrt(ms + EPS) * g_ref[...]


def _ssd_out(x2, oa, qx, gx, km, vm, w_out, g_final, xbc, z, sm, smt, consts, *, batch, seq):
    rows = batch * seq
    ts = ROW_TILE
    tps = seq // ts
    mlen = km.shape[1]
    row = lambda n: pl.BlockSpec((ts, n), lambda r: (r, 0))
    full = lambda a: pl.BlockSpec(a.shape, lambda r: (0,) * a.ndim)
    mem = pl.BlockSpec((1, mlen, XA_WIDTH), lambda r: (r // tps, 0, 0))
    return pl.pallas_call(
        functools.partial(_ssdout_body, ts=ts, tiles_per_seq=tps),
        grid=(rows // ts,),
        in_specs=[row(D_MODEL), row(NSA_WIDTH), row(XA_WIDTH), row(XA_WIDTH), mem, mem,
                  full(w_out), full(g_final),
                  row(CONV_DIM), row(SSD_WIDTH), row(LANES), pl.BlockSpec((1, LANES, ts), lambda r: (r, 0, 0))]
                 + [full(a) for a in consts],
        out_specs=row(D_MODEL),
        out_shape=jax.ShapeDtypeStruct((rows, D_MODEL), F32),
        scratch_shapes=[pltpu.VMEM((ts, D_MODEL), F32), pltpu.VMEM((ts, SSD_WIDTH + XA_WIDTH), BF16),
                        pltpu.VMEM((ts + 8, CONV_DIM), F32), pltpu.VMEM((ts, CONV_DIM), F32),
                        pltpu.VMEM((ts, SSD_WIDTH), F32),
                        pltpu.VMEM((SSD_GROUPS, SSD_STATE, SSD_WIDTH // SSD_GROUPS), F32)],
        compiler_params=pltpu.CompilerParams(dimension_semantics=("arbitrary",),
                                             vmem_limit_bytes=VMEM_LIMIT),
        name="ssd_out",
    )(x2, oa, qx, gx, km, vm, w_out, g_final, xbc, z, sm, smt, *consts)


def _memkv_body(mem_ref, g_ref, w_ref, k_ref, v_ref):
    x = mem_ref[0]
    ms = jnp.mean(x * x, axis=-1, keepdims=True)
    xn = (x * lax.rsqrt(ms + EPS) * g_ref[...]).astype(BF16)
    k_ref[0] = _dot(xn, w_ref[:, 0:XA_WIDTH]).astype(BF16)
    v_ref[0] = _dot(xn, w_ref[:, XA_WIDTH:2 * XA_WIDTH]).astype(BF16)


def _mem_kv(mem, g_mem, w_kv):
    batch, mlen, _ = mem.shape
    full = lambda a: pl.BlockSpec(a.shape, lambda b: (0,) * a.ndim)
    out = pl.BlockSpec((1, mlen, XA_WIDTH), lambda b: (b, 0, 0))
    shp = jax.ShapeDtypeStruct((batch, mlen, XA_WIDTH), BF16)
    return pl.pallas_call(
        _memkv_body,
        grid=(batch,),
        in_specs=[pl.BlockSpec((1, mlen, D_MODEL), lambda b: (b, 0, 0)), full(g_mem), full(w_kv)],
        out_specs=[out, out],
        out_shape=[shp, shp],
        compiler_params=pltpu.CompilerParams(dimension_semantics=("parallel",),
                                             vmem_limit_bytes=VMEM_LIMIT),
        name="mem_kv",
    )(mem, g_mem, w_kv)


def _permute_w_in(w, layer):
    rb = 128
    return pl.pallas_call(
        _wprep_body,
        grid=(w.shape[1] // rb,),
        in_specs=[pl.BlockSpec((1, rb, w.shape[2]), lambda r: (layer, r, 0))],
        out_specs=pl.BlockSpec((rb, N_PROJ), lambda r: (r, 0)),
        out_shape=jax.ShapeDtypeStruct((w.shape[1], N_PROJ), BF16),
        compiler_params=pltpu.CompilerParams(dimension_semantics=("parallel",),
                                             vmem_limit_bytes=VMEM_LIMIT),
        name="w_prep",
    )(w)


def _wprep_body(w_ref, o_ref):
    sizes = [NSA_WIDTH] + [KV_WIDTH] * 6 + [NSA_HEADS * 3, NSA_WIDTH, SSD_WIDTH, CONV_DIM, SSD_HEADS,
                                          XA_WIDTH, XA_WIDTH]
    offs = [0]
    for s in sizes:
        offs.append(offs[-1] + s)
    w = w_ref[0]

    def move(dst, lo, hi):
        o_ref[:, dst:dst + hi - lo] = w[:, lo:hi].astype(BF16)

    move(OFF_Q, offs[0], offs[7])
    move(OFF_GA, offs[8], offs[11])
    move(OFF_QX, offs[12], offs[14])
    small = jnp.concatenate(
        [w[:, offs[7]:offs[8]], w[:, offs[11]:offs[12]],
         jnp.zeros((w.shape[0], LANES - NSA_HEADS * 3 - SSD_HEADS), w.dtype)], axis=1)
    o_ref[:, OFF_SM:OFF_SM + LANES] = small.astype(BF16)


def _lane_row(vals, lane0):
    return jnp.zeros((1, LANES), F32).at[0, lane0:lane0 + vals.shape[0]].set(vals.astype(F32))


def _forward(x, mem, positions, g_in, w_in, cmp_pos_k, w_cmp1_k, w_cmp2_k, cmp_pos_v, w_cmp1_v,
             w_cmp2_v, conv_w, conv_b, dt_bias, a_log, d_skip, g_ssd_norm, g_mem, w_mem_kv, w_out,
             g_final):
    batch, seq, _ = x.shape
    ncp = seq // CMP_STRIDE
    n_slc = seq // SLC_BLOCK
    assert n_slc <= SLC_SLOTS and seq % ROW_TILE == 0 and ncp <= NSA_TILE
    topk = min(SLC_TOPK, n_slc)
    rows = batch * seq

    inv = ROPE_THETA ** (-jnp.arange(0, ROPE_DIM, 2, dtype=F32) / ROPE_DIM)
    head_inv = jnp.concatenate([inv, inv, jnp.zeros((HEAD_DIM - ROPE_DIM,), F32)])
    invl = jnp.tile(head_inv, LANES // HEAD_DIM)[None, :]
    head_sgn = jnp.concatenate([-jnp.ones((ROPE_HALF,), F32), jnp.ones((ROPE_HALF,), F32),
                                jnp.zeros((HEAD_DIM - ROPE_DIM,), F32)])
    sgn = jnp.tile(head_sgn, LANES // HEAD_DIM)[None, :]
    inv_rows = jnp.concatenate([inv, jnp.zeros((ROPE_FREQ_ROWS - ROPE_HALF,), F32)])[:, None]
    d_ix = jnp.arange(LANES) % HEAD_DIM
    freq_of_lane = jnp.where(d_ix < ROPE_DIM, d_ix % ROPE_HALF, ROPE_HALF)
    spread_cos = (freq_of_lane[:, None] == jnp.arange(ROPE_FREQ_ROWS)[None, :]).astype(F32)
    spread_sin = spread_cos * sgn[0][:, None]

    x2 = x.reshape(rows, D_MODEL)
    pos3 = positions.reshape(rows // ROW_TILE, 1, ROW_TILE)
    h = x2
    for l in range(g_in.shape[0]):
        (q, ck, cv, ksa, vst, kw, vwt, ga, z, xbc, qx, gx, sm, smt) = _in_proj(
            h, pos3, g_in[l][None, :], _permute_w_in(w_in, l), inv_rows,
            spread_cos.astype(BF16), spread_sin.astype(BF16), batch=batch, seq=seq)

        cmp_end = jnp.minimum(jnp.arange(ncp) * CMP_STRIDE + CMP_LEN - 1, seq - 1)
        posc = positions[:, cmp_end][:, :, None]
        pad_w2 = lambda w: jnp.pad(w, ((0, 0), (0, LANES - HEAD_DIM))).astype(BF16)
        pos_rows = lambda p: jnp.broadcast_to(p.reshape(1, CMP_LEN * HEAD_DIM), (8, CMP_LEN * HEAD_DIM)).astype(BF16)
        kc, vc = _compress(ck, cv, posc,
                           w_cmp1_k[l].astype(BF16), pad_w2(w_cmp2_k[l]), pos_rows(cmp_pos_k[l]),
                           w_cmp1_v[l].astype(BF16), pad_w2(w_cmp2_v[l]), pos_rows(cmp_pos_v[l]),
                           invl, sgn, batch=batch, ncp=ncp)
        n_ix = jnp.arange(ncp)[None, :]
        j_ix = jnp.arange(SLC_SLOTS)[:, None]
        ovl = ((n_ix * CMP_STRIDE < j_ix * SLC_BLOCK + SLC_BLOCK)
               & (n_ix * CMP_STRIDE + CMP_LEN > j_ix * SLC_BLOCK)
               & (n_ix < ncp - (CMP_LEN // CMP_STRIDE - 1))).astype(BF16)
        bg = (batch, KV_GROUPS)
        cat = jnp.concatenate(
            [jnp.swapaxes(vc, 2, 3), jnp.ones(bg + (1, ncp), BF16),
             jnp.zeros(bg + (CAT_OVL_ROW0 - HEAD_DIM - 1, ncp), BF16),
             jnp.broadcast_to(ovl, bg + ovl.shape),
             jnp.zeros(bg + (CAT_ROWS - CAT_OVL_ROW0 - SLC_SLOTS, ncp), BF16)], axis=2)
        o_a = _nsa(q, kc, cat, ksa, vst, kw, vwt, ga, sm,
                   batch=batch, seq=seq, ncp=ncp, topk=topk)

        head_of_lane = jnp.arange(SSD_WIDTH) // SSD_HEAD_DIM
        k_ix = jnp.arange(LANES)[:, None]
        eh = (k_ix == DT_LANE0 + head_of_lane[None, :]).astype(BF16)
        ehw = (k_ix == DT_LANE0 + (jnp.arange(SSD_HEADS * LANES) // LANES)[None, :]).astype(BF16)
        t_ix = jnp.arange(SSD_CHUNK)
        tril = (t_ix[None, :] <= t_ix[:, None]).astype(BF16)
        ssd_consts = [conv_w[l], conv_b[l][None, :],
                      _lane_row(dt_bias[l], DT_LANE0), dt_bias[l].astype(F32)[:, None],
                      _lane_row(a_log[l], DT_LANE0), a_log[l].astype(F32)[:, None],
                      jnp.repeat(d_skip[l].astype(F32), SSD_HEAD_DIM)[None, :], g_ssd_norm[l][None, :],
                      eh, ehw, tril, tril.T]

        km, vm = _mem_kv(mem, g_mem[l][None, :], w_mem_kv[l].astype(BF16))
        assert g_in.shape[0] == 1, "the final RMSNorm is fused into the (single) layer's last kernel"
        h = _ssd_out(h, o_a, qx, gx, km, vm, w_out[l].astype(BF16), g_final[None, :],
                     xbc, z, sm, smt, ssd_consts, batch=batch, seq=seq)
    return h.reshape(batch, seq, D_MODEL)


def kernel(x, mem, positions, g_in, w_in, cmp_pos_k, w_cmp1_k, w_cmp2_k, cmp_pos_v, w_cmp1_v, w_cmp2_v,
           conv_w, conv_b, dt_bias, a_log, d_skip, g_ssd_norm, g_mem, w_mem_kv, w_out, g_final):
    return _forward(x, mem, positions, g_in, w_in, cmp_pos_k, w_cmp1_k, w_cmp2_k, cmp_pos_v, w_cmp1_v,
                    w_cmp2_v, conv_w, conv_b, dt_bias, a_log, d_skip, g_ssd_norm, g_mem, w_mem_kv,
                    w_out, g_final)
```

```python
import functools

import jax
import jax.numpy as jnp
from jax import lax
from jax.experimental import pallas as pl
from jax.experimental.pallas import tpu as pltpu

F32 = jnp.float32
BF16 = jnp.bfloat16
I32 = jnp.int32

D_MODEL = 1024
NSA_HEADS = 8
HEAD_DIM = 64
KV_GROUPS = 2
HEADS_PER_GROUP = NSA_HEADS // KV_GROUPS
NSA_WIDTH = NSA_HEADS * HEAD_DIM
KV_WIDTH = KV_GROUPS * HEAD_DIM
CMP_LEN = 32
CMP_STRIDE = 16
CMP_HIDDEN = 256
SLC_BLOCK = 64
SLC_TOPK = 16
SLC_SLOTS = 64
WINDOW = 512
BIG = 1e9
NEG = -1e30

SSD_HEADS = 8
SSD_HEAD_DIM = 64
SSD_WIDTH = SSD_HEADS * SSD_HEAD_DIM
SSD_GROUPS = 2
SSD_STATE = 128
SSD_CONV = 4
SSD_CHUNK = 128
CONV_DIM = SSD_WIDTH + 2 * SSD_GROUPS * SSD_STATE

XA_HEADS = 4
XA_HEAD_DIM = 128
XA_WIDTH = XA_HEADS * XA_HEAD_DIM
MIX_WIDTH = NSA_WIDTH + SSD_WIDTH + XA_WIDTH

ROPE_THETA = 500000.0
ROPE_DIM = HEAD_DIM // 4
ROPE_HALF = ROPE_DIM // 2
ROPE_FREQ_ROWS = 16
EPS = 1e-6

LANES = 128
GATE_LANE0 = 0
DT_LANE0 = 24

OFF_Q = 0
OFF_KV = OFF_Q + NSA_WIDTH
OFF_GA = OFF_KV + 6 * KV_WIDTH
OFF_Z = OFF_GA + NSA_WIDTH
OFF_XBC = OFF_Z + SSD_WIDTH
OFF_QX = OFF_XBC + CONV_DIM
OFF_GX = OFF_QX + XA_WIDTH
OFF_SM = OFF_GX + XA_WIDTH
N_PROJ = OFF_SM + LANES

ROW_TILE = 512
NSA_TILE = 256
V_ROWS = 80
LOG2E = 1.4426950408889634
CAT_OVL_ROW0 = 80
CAT_ROWS = 144
VMEM_LIMIT = 56 * 1024 * 1024


def _nt(a, b):
    return lax.dot_general(a, b, (((1,), (1,)), ((), ())), preferred_element_type=F32)


def _dot(a, b):
    return jnp.dot(a, b, preferred_element_type=F32)


def _split2(x):
    hi = x.astype(BF16)
    return hi, (x - hi.astype(F32)).astype(BF16)


def _dot2_l(x, w):
    hi, lo = _split2(x)
    return _dot(hi, w) + _dot(lo, w)


def _dot2_r(w, x):
    hi, lo = _split2(x)
    return _dot(w, hi) + _dot(w, lo)


def _silu(x):
    h = 0.5 * x
    return h + h * jnp.tanh(h)


def _rope(a, cs, sn, first):
    r = jnp.where(first, pltpu.roll(a, LANES - ROPE_HALF, 1), pltpu.roll(a, ROPE_HALF, 1))
    return a * cs + r * sn


def _rope_tables(pos_f32, invl, sgn):
    ang = pos_f32 * invl
    return jnp.cos(ang), jnp.sin(ang) * sgn


def _inproj_body(x_ref, pos_ref, g_ref, w_ref, invc_ref, ec_ref, es_ref,
                 q_ref, ck_ref, cv_ref, ksa_ref, vst_ref, kw_ref, vwt_ref,
                 ga_ref, z_ref, xbc_ref, qx_ref, gx_ref, sm_ref, smt_ref, sk_ref, sv_ref, *, tm, tiles_per_seq):
    x = x_ref[...]
    ms = jnp.mean(x * x, axis=-1, keepdims=True)
    xn = (x * lax.rsqrt(ms + EPS) * g_ref[...]).astype(BF16)

    def mm(lo, n):
        return _dot(xn, w_ref[:, lo:lo + n])

    ga_ref[...] = mm(OFF_GA, NSA_WIDTH)
    z_ref[...] = mm(OFF_Z, SSD_WIDTH)

    ang = invc_ref[...] * pos_ref[0].astype(F32)
    cs = _dot2_r(ec_ref[...], jnp.cos(ang)).T
    sn = _dot2_r(es_ref[...], jnp.sin(ang)).T
    lane = lax.broadcasted_iota(I32, (tm, LANES), 1)
    first = (lane % HEAD_DIM) < ROPE_HALF

    qf = mm(OFF_Q, NSA_WIDTH)
    for c in range(NSA_WIDTH // LANES):
        sl = slice(c * LANES, (c + 1) * LANES)
        q_ref[:, sl] = (_rope(qf[:, sl], cs, sn, first) * (HEAD_DIM ** -0.5 * LOG2E)).astype(BF16)

    kv = mm(OFF_KV, 6 * KV_WIDTH)
    k_c, v_c, k_s, v_s, k_w, v_w = [kv[:, n * KV_WIDTH:(n + 1) * KV_WIDTH] for n in range(6)]
    k_s = _rope(k_s, cs, sn, first)
    k_w = _rope(k_w, cs, sn, first)
    s_base = (pl.program_id(0) % tiles_per_seq) * tm
    blk = (s_base + lax.broadcasted_iota(I32, (tm, LANES), 0)) // SLC_BLOCK
    onehot = jnp.where(lane - HEAD_DIM == blk, 1.0, 0.0)
    low = lane < HEAD_DIM
    vs_t = v_s.T
    vw_t = v_w.T
    tail = jnp.where(lax.broadcasted_iota(I32, (V_ROWS - HEAD_DIM, NSA_TILE), 0) == 0, 1.0, 0.0).astype(BF16)
    for g in range(KV_GROUPS):
        gs = slice(g * HEAD_DIM, (g + 1) * HEAD_DIM)
        kw_ref[0, g] = k_w[:, gs].astype(BF16)
        ks_g = k_s if g == 0 else pltpu.roll(k_s, HEAD_DIM, 1)
        ksa_ref[0, g] = jnp.where(low, ks_g, onehot).astype(BF16)
        for c in range(tm // NSA_TILE):
            cols = slice(c * NSA_TILE, (c + 1) * NSA_TILE)
            vst_ref[0, g, c, 0:HEAD_DIM, :] = vs_t[gs, cols].astype(BF16)
            vst_ref[0, g, c, HEAD_DIM:V_ROWS, :] = tail
            vwt_ref[0, g, c, 0:HEAD_DIM, :] = vw_t[gs, cols].astype(BF16)
            vwt_ref[0, g, c, HEAD_DIM:V_ROWS, :] = tail

    sk_ref[...] = k_c
    sv_ref[...] = v_c
    nchunk = tm // CMP_STRIDE
    for t in range(CMP_STRIDE):
        dst = slice(t * HEAD_DIM, (t + 1) * HEAD_DIM)
        kt = sk_ref[pl.ds(t, nchunk, stride=CMP_STRIDE), :].astype(BF16)
        vt = sv_ref[pl.ds(t, nchunk, stride=CMP_STRIDE), :].astype(BF16)
        for g in range(KV_GROUPS):
            gs = slice(g * HEAD_DIM, (g + 1) * HEAD_DIM)
            ck_ref[0, g, :, dst] = kt[:, gs]
            cv_ref[0, g, :, dst] = vt[:, gs]

    xbc_ref[...] = mm(OFF_XBC, CONV_DIM)
    qx_ref[...] = (mm(OFF_QX, XA_WIDTH) * (XA_HEAD_DIM ** -0.5 * LOG2E)).astype(BF16)
    gx_ref[...] = mm(OFF_GX, XA_WIDTH)
    sm = mm(OFF_SM, LANES)
    sm_ref[...] = sm
    smt_ref[0] = sm.T


def _in_proj(x2, pos3, g_in, w_p, invc, ec, es, *, batch, seq):
    rows = batch * seq
    tm = ROW_TILE
    tps = seq // tm
    row = lambda n: pl.BlockSpec((tm, n), lambda r: (r, 0))
    full = lambda a: pl.BlockSpec(a.shape, lambda r: (0,) * a.ndim)
    grp = lambda n: pl.BlockSpec((1, KV_GROUPS, tm, n), lambda r: (r // tps, 0, r % tps, 0))
    grp_shape = lambda n: jax.ShapeDtypeStruct((batch, KV_GROUPS, seq, n), BF16)
    chunk = pl.BlockSpec((1, KV_GROUPS, tm // CMP_STRIDE, CMP_STRIDE * HEAD_DIM),
                         lambda r: (r // tps, 0, r % tps, 0))
    chunk_shape = jax.ShapeDtypeStruct((batch, KV_GROUPS, seq // CMP_STRIDE, CMP_STRIDE * HEAD_DIM), BF16)
    vtile = pl.BlockSpec((1, KV_GROUPS, tm // NSA_TILE, V_ROWS, NSA_TILE), lambda r: (r // tps, 0, r % tps, 0, 0))
    vtile_shape = jax.ShapeDtypeStruct((batch, KV_GROUPS, seq // NSA_TILE, V_ROWS, NSA_TILE), BF16)
    flat = lambda n, dt: jax.ShapeDtypeStruct((rows, n), dt)
    return pl.pallas_call(
        functools.partial(_inproj_body, tm=tm, tiles_per_seq=tps),
        grid=(rows // tm,),
        in_specs=[row(D_MODEL), pl.BlockSpec((1, 1, tm), lambda r: (r, 0, 0)),
                  full(g_in), full(w_p), full(invc), full(ec), full(es)],
        out_specs=[row(NSA_WIDTH), chunk, chunk, grp(LANES), vtile, grp(HEAD_DIM), vtile,
                   row(NSA_WIDTH), row(SSD_WIDTH), row(CONV_DIM),
                   row(XA_WIDTH), row(XA_WIDTH), row(LANES),
                   pl.BlockSpec((1, LANES, tm), lambda r: (r, 0, 0))],
        out_shape=[flat(NSA_WIDTH, BF16), chunk_shape, chunk_shape, grp_shape(LANES), vtile_shape,
                   grp_shape(HEAD_DIM), vtile_shape,
                   flat(NSA_WIDTH, F32), flat(SSD_WIDTH, F32), flat(CONV_DIM, F32),
                   flat(XA_WIDTH, BF16), flat(XA_WIDTH, F32), flat(LANES, F32),
                   jax.ShapeDtypeStruct((rows // tm, LANES, tm), F32)],
        scratch_shapes=[pltpu.VMEM((tm, KV_WIDTH), F32), pltpu.VMEM((tm, KV_WIDTH), F32)],
        compiler_params=pltpu.CompilerParams(dimension_semantics=("parallel",),
                                             vmem_limit_bytes=VMEM_LIMIT),
        name="in_proj",
    )(x2, pos3, g_in, w_p, invc, ec, es)


def _compress_body(ck_ref, cv_ref, posc_ref, w1k_ref, w2k_ref, pk_ref, w1v_ref, w2v_ref, pv_ref,
                   invl_ref, sgn_ref, kc_ref, vc_ref, *, ncp):
    half = CMP_STRIDE * HEAD_DIM

    def mlp(c_ref, w1_ref, w2_ref, p_ref):
        c = c_ref[0, 0]
        a = _dot(c, w1_ref[0:half, :])
        b = _dot(c, w1_ref[half:2 * half, :])
        bias = _dot(p_ref[...], w1_ref[...])[0:1, :]
        h = a + pltpu.roll(b, ncp - 1, 0) + bias
        return _dot(_silu(h).astype(BF16), w2_ref[...])

    kc = mlp(ck_ref, w1k_ref, w2k_ref, pk_ref)
    vc = mlp(cv_ref, w1v_ref, w2v_ref, pv_ref)
    cs, sn = _rope_tables(posc_ref[0].astype(F32), invl_ref[...], sgn_ref[...])
    lane = lax.broadcasted_iota(I32, (ncp, LANES), 1)
    kc = _rope(kc, cs, sn, (lane % HEAD_DIM) < ROPE_HALF)
    kc_ref[0, 0] = kc[:, 0:HEAD_DIM].astype(BF16)
    vc_ref[0, 0] = vc[:, 0:HEAD_DIM].astype(BF16)


def _compress(ck, cv, posc, w1k, w2k, pk, w1v, w2v, pv, invl, sgn, *, batch, ncp):
    chunk = pl.BlockSpec((1, 1, ncp, CMP_STRIDE * HEAD_DIM), lambda b, g: (b, g, 0, 0))
    full = lambda a: pl.BlockSpec(a.shape, lambda b, g: (0,) * a.ndim)
    out = pl.BlockSpec((1, 1, ncp, HEAD_DIM), lambda b, g: (b, g, 0, 0))
    shp = jax.ShapeDtypeStruct((batch, KV_GROUPS, ncp, HEAD_DIM), BF16)
    return pl.pallas_call(
        functools.partial(_compress_body, ncp=ncp),
        grid=(batch, KV_GROUPS),
        in_specs=[chunk, chunk, pl.BlockSpec((1, ncp, 1), lambda b, g: (b, 0, 0)),
                  full(w1k), full(w2k), full(pk), full(w1v), full(w2v), full(pv),
                  full(invl), full(sgn)],
        out_specs=[out, out],
        out_shape=[shp, shp],
        compiler_params=pltpu.CompilerParams(dimension_semantics=("parallel", "parallel"),
                                             vmem_limit_bytes=VMEM_LIMIT),
        name="compress",
    )(ck, cv, posc, w1k, w2k, pk, w1v, w2v, pv, invl, sgn)


def _nsa_body(q_ref, kc_ref, cat_ref, ksa_ref, vst_ref, kw_ref, vwt_ref, ga_ref, sm_ref,
              o_ref, qa_ref, m_ref, acc_ref, oc_ref, sc_ref, mc_ref, gt_ref, gs_ref, *, tq, ncp, topk):
    g = pl.program_id(1)
    i = pl.program_id(2)
    s0 = i * tq
    nh = HEADS_PER_GROUP
    rows = nh * tq

    def col_max(s):
        n = s.shape[0]
        while n > 8:
            n //= 2
            s = jnp.maximum(s[0:n], s[n:2 * n])
        return jnp.max(s, axis=0, keepdims=True)

    def reset(br):
        m_ref[br] = jnp.full((1, rows), NEG, F32)
        acc_ref[br] = jnp.zeros((V_ROWS, rows), F32)

    def consume(br, s, vt_tile, mask, s_max=None):
        if mask is not None:
            s = jnp.where(mask, s, NEG)
        if s_max is None:
            s_max = col_max(s)
        m_prev = m_ref[br]
        m_new = jnp.maximum(m_prev, s_max)
        alpha = jnp.exp2(m_prev - m_new)
        p = jnp.exp2(s - m_new).astype(BF16)
        acc_ref[br] = alpha * acc_ref[br] + _dot(vt_tile, p)
        m_ref[br] = m_new

    def sel_scores(j):
        rows_j = pl.ds(pl.multiple_of(j * tq, tq), tq)
        return _nt(ksa_ref[0, 0, rows_j, :], qa_ref[...])

    def win_scores(j):
        rows_j = pl.ds(pl.multiple_of(j * tq, tq), tq)
        return _nt(kw_ref[0, 0, rows_j, :], qa_ref[:, 0:HEAD_DIM])

    q4 = q_ref[...]
    for e in range(nh):
        qa_ref[e * tq:(e + 1) * tq, 0:HEAD_DIM] = q4[:, e * HEAD_DIM:(e + 1) * HEAD_DIM]

    k_io = lax.broadcasted_iota(I32, (tq, rows), 0)
    t_io2 = lax.broadcasted_iota(I32, (tq, rows), 1) % tq
    diag = k_io <= t_io2
    j1 = jnp.maximum(i - 1, 0)
    j2 = jnp.maximum(i - 2, 0)
    cmp_per_blk = SLC_BLOCK // CMP_STRIDE

    def produce(slot, j):
        s = sel_scores(j)
        sc_ref[slot] = s
        mc_ref[slot] = col_max(s)

    def first_tile_and_gates():
        reset(0)
        produce(0, 0)
        gates_t = jax.nn.sigmoid(sm_ref[...]).T
        gs_ref[...] = _silu(ga_ref[...])

        def gate(e, c):
            c0 = GATE_LANE0 + e * 3 + c
            c1 = c0 + nh * 3
            return jnp.where(g == 0, gates_t[c0:c0 + 1, :], gates_t[c1:c1 + 1, :])

        for e in range(nh):
            cs = slice(e * tq, (e + 1) * tq)
            o_w = acc_ref[1, 0:HEAD_DIM, cs] / acc_ref[1, HEAD_DIM:HEAD_DIM + 1, cs]
            oc_ref[:, cs] = gate(e, 0) * oc_ref[:, cs] + gate(e, 2) * o_w
            gt_ref[e:e + 1, :] = gate(e, 1)

    def prologue(cap):
        nc = min(ncp, cap * cmp_per_blk)
        n_io = lax.broadcasted_iota(I32, (nc, rows), 0)
        t_io = s0 + lax.broadcasted_iota(I32, (nc, rows), 1) % tq
        s = jnp.where((n_io * CMP_STRIDE + (CMP_LEN - 1)) <= t_io,
                      _nt(kc_ref[0, 0, 0:nc, :], qa_ref[:, 0:HEAD_DIM]), NEG)
        sc_ref[0, 0:nc, :] = s
        mc_ref[0] = col_max(s)
        sc_ref[1] = win_scores(i)
        p = jnp.exp2(sc_ref[0, 0:nc, :] - mc_ref[0]).astype(BF16)
        big = _dot(cat_ref[0, 0, :, 0:nc], p)
        sc_ref[0] = win_scores(j1)
        t_row = s0 + lax.broadcasted_iota(I32, (1, rows), 1) % tq
        inv = jnp.where(t_row >= CMP_LEN - 1, 1.0 / jnp.maximum(big[HEAD_DIM:HEAD_DIM + 1, :], 1e-30), 0.0)
        oc_ref[...] = big[0:HEAD_DIM, :] * inv
        imp_x = big[CAT_OVL_ROW0:CAT_OVL_ROW0 + cap, :] * inv
        imp = imp_x[:, 0:tq]
        for e in range(1, nh):
            imp = imp + imp_x[:, e * tq:(e + 1) * tq]

        j_io = lax.broadcasted_iota(I32, (cap, tq), 0)
        cur = (s0 + lax.broadcasted_iota(I32, (cap, tq), 1)) // SLC_BLOCK
        valid = j_io <= cur
        if cap > topk:
            TAKEN = -3e38
            TAKEN_BELOW = -1e37
            forced = (j_io == 0) | (j_io == cur) | (j_io == cur - 1)
            j_f = j_io.astype(F32)
            v = jnp.where(forced, TAKEN, jnp.where(valid, imp, -BIG))
            for _ in range(topk - 3):
                mx = jnp.max(v, axis=0, keepdims=True)
                idx = jnp.min(jnp.where(v == mx, j_f, float(SLC_SLOTS)), axis=0, keepdims=True)
                v = jnp.where(j_f == idx, TAKEN, v)
            valid = valid & (v < TAKEN_BELOW)
        selb = jnp.where(valid, 0.0, NEG)
        fill = [jnp.full((SLC_SLOTS - cap, tq), NEG, F32)] if cap < SLC_SLOTS else []
        selb_t = jnp.concatenate([jnp.zeros((SLC_SLOTS, tq), F32), selb] + fill, axis=0).T
        selb_t = selb_t.astype(BF16)

        reset(1)
        consume(1, sc_ref[1], vwt_ref[0, 0, i], diag)
        sc_ref[1] = win_scores(j2)
        consume(1, sc_ref[0], vwt_ref[0, 0, j1], i >= 1)
        consume(1, sc_ref[1], vwt_ref[0, 0, j2], (k_io > t_io2) & (i >= 2))
        for e in range(nh):
            qa_ref[e * tq:(e + 1) * tq, HEAD_DIM:LANES] = selb_t[:, HEAD_DIM:LANES]
        first_tile_and_gates()

    n_vis = (i + 1) * (tq // SLC_BLOCK)
    lo = 0
    for cap in sorted({min(topk, SLC_SLOTS), SLC_SLOTS // 2, SLC_SLOTS}):
        pl.when((n_vis > lo) & (n_vis <= cap))(functools.partial(prologue, cap))
        lo = cap

    def sel_pair(j):
        produce(1, j + 1)
        consume(0, sc_ref[0], vst_ref[0, 0, j], None, mc_ref[0])
        produce(0, j + 2)
        consume(0, sc_ref[1], vst_ref[0, 0, j + 1], None, mc_ref[1])

    done = 0
    for width in (8, 4, 2):
        start = done

        def block(jj, carry, width=width, start=start):
            for u in range(0, width, 2):
                sel_pair(start + width * jj + u)
            return carry

        trips = (i - start) // width
        lax.fori_loop(0, trips, block, 0)
        done = start + trips * width

    @pl.when(i % 2 == 0)
    def _():
        consume(0, sc_ref[0], vst_ref[0, 0, i], diag)

    @pl.when(i % 2 == 1)
    def _():
        sc_ref[1] = sel_scores(i)
        consume(0, sc_ref[0], vst_ref[0, 0, i - 1], None, mc_ref[0])
        consume(0, sc_ref[1], vst_ref[0, 0, i], diag)

    def head_out(e):
        cs = slice(e * tq, (e + 1) * tq)
        o_s = acc_ref[0, 0:HEAD_DIM, cs] / acc_ref[0, HEAD_DIM:HEAD_DIM + 1, cs]
        return oc_ref[:, cs] + gt_ref[e:e + 1, :] * o_s

    for pr in range(nh // 2):
        ls = slice(pr * LANES, (pr + 1) * LANES)
        pair = jnp.concatenate([head_out(2 * pr), head_out(2 * pr + 1)], axis=0).T
        o_ref[:, ls] = (pair * gs_ref[:, ls]).astype(BF16)


def _nsa(q, kc, cat, ksa, vst, kw, vwt, ga, sm, *, batch, seq, ncp, topk):
    tq = NSA_TILE
    assert WINDOW == 2 * tq
    nq = seq // tq
    gw = HEADS_PER_GROUP * HEAD_DIM
    rows = HEADS_PER_GROUP * tq
    per_bg = lambda *shape: pl.BlockSpec((1, 1) + shape, lambda b, g, i: (b, g) + (0,) * len(shape))
    return pl.pallas_call(
        functools.partial(_nsa_body, tq=tq, ncp=ncp, topk=topk),
        grid=(batch, KV_GROUPS, nq),
        in_specs=[pl.BlockSpec((tq, gw), lambda b, g, i: (b * nq + i, g)),
                  per_bg(ncp, HEAD_DIM), per_bg(CAT_ROWS, ncp),
                  per_bg(seq, LANES), per_bg(nq, V_ROWS, tq), per_bg(seq, HEAD_DIM),
                  per_bg(nq, V_ROWS, tq),
                  pl.BlockSpec((tq, gw), lambda b, g, i: (b * nq + i, g)),
                  pl.BlockSpec((tq, LANES), lambda b, g, i: (b * nq + i, 0))],
        out_specs=pl.BlockSpec((tq, gw), lambda b, g, i: (b * nq + i, g)),
        out_shape=jax.ShapeDtypeStruct((batch * seq, NSA_WIDTH), BF16),
        scratch_shapes=[pltpu.VMEM((rows, LANES), BF16),
                        pltpu.VMEM((2, 1, rows), F32),
                        pltpu.VMEM((2, V_ROWS, rows), F32),
                        pltpu.VMEM((HEAD_DIM, rows), F32),
                        pltpu.VMEM((2, tq, rows), F32), pltpu.VMEM((2, 1, rows), F32),
                        pltpu.VMEM((2 * HEADS_PER_GROUP, tq), F32), pltpu.VMEM((tq, gw), F32)],
        compiler_params=pltpu.CompilerParams(
            dimension_semantics=("parallel", "parallel", "arbitrary"),
            vmem_limit_bytes=VMEM_LIMIT),
        name="nsa",
    )(q, kc, cat, ksa, vst, kw, vwt, ga, sm)


def _ssdout_body(x_ref, oa_ref, qx_ref, gx_ref, k_ref, v_ref, w_ref, g_ref,
                 xbc_ref, z_ref, sm_ref, dtt_ref, cw_ref, cb_ref, dtb_ref, dtbt_ref, al_ref, alt_ref,
                 dsk_ref, gn_ref, eh_ref, ehw_ref, tril_ref, triu_ref,
                 o_ref, acc_ref, mix_ref, ext_ref, xc_ref, y_ref, h_ref, *, ts, tiles_per_seq):
    L = SSD_CHUNK
    N = SSD_STATE
    P = SSD_HEAD_DIM
    E = SSD_HEADS // SSD_GROUPS
    gw = E * P
    pad = 8
    w_b = NSA_WIDTH
    w_c = NSA_WIDTH + SSD_WIDTH

    @pl.when(pl.program_id(0) % tiles_per_seq == 0)
    def _():
        ext_ref[0:pad, :] = jnp.zeros((pad, CONV_DIM), F32)
        h_ref[...] = jnp.zeros(h_ref.shape, F32)

    def attn_head(h):
        hs = slice(h * XA_HEAD_DIM, (h + 1) * XA_HEAD_DIM)
        s = _nt(qx_ref[:, hs], k_ref[0, :, hs])
        p = jnp.exp2(s - jnp.max(s, axis=-1, keepdims=True))
        inv = 1.0 / jnp.sum(p, axis=-1, keepdims=True)
        oc = _dot(p.astype(BF16), v_ref[0, :, hs]) * (inv * _silu(gx_ref[:, hs]))
        mix_ref[:, SSD_WIDTH + h * XA_HEAD_DIM:SSD_WIDTH + (h + 1) * XA_HEAD_DIM] = oc.astype(BF16)

    ext_ref[pad:pad + ts, :] = xbc_ref[...]
    assert SSD_CONV == 4
    ext = ext_ref[...]
    ext1 = pltpu.roll(ext, 1, 0)
    near = ext * cw_ref[3:4, :] + ext1 * cw_ref[2:3, :]
    far = ext * cw_ref[1:2, :] + ext1 * cw_ref[0:1, :]
    taps = near + pltpu.roll(far, 2, 0)
    xc_ref[...] = _silu(cb_ref[...] + taps[pad:pad + ts, :])
    ext_ref[0:pad, :] = ext_ref[ts:ts + pad, :]

    for h in range(XA_HEADS):
        attn_head(h)
    acc_ref[...] = (x_ref[...] + _dot(oa_ref[...], w_ref[0:w_b, :])
                    + _dot(mix_ref[:, SSD_WIDTH:], w_ref[w_c:, :]))

    a_row = -jnp.exp(al_ref[...]) * LOG2E
    a_col = -jnp.exp(alt_ref[...]) * LOG2E
    causal = lax.broadcasted_iota(I32, (L, L), 1) <= lax.broadcasted_iota(I32, (L, L), 0)
    low = lax.broadcasted_iota(I32, (ts, LANES), 1) < P
    nck = ts // L
    chunk = lambda c: slice(c * L, (c + 1) * L)

    xs = xc_ref[:, 0:SSD_WIDTH]
    dt = jax.nn.softplus(sm_ref[...] + dtb_ref[...])
    da = dt * a_row
    a_cs = jnp.concatenate([_dot2_r(tril_ref[...], da[chunk(c), :]) for c in range(nck)], axis=0)
    dt_x = _dot2_l(dt, eh_ref[...])
    acs_w = _dot2_l(a_cs, ehw_ref[...])
    acs_x = jnp.concatenate(
        [jnp.where(low, acs_w[:, (2 * n) * LANES:(2 * n + 1) * LANES],
                   acs_w[:, (2 * n + 1) * LANES:(2 * n + 2) * LANES]) for n in range(SSD_HEADS // 2)],
        axis=1)
    a_last = [acs_x[c * L + L - 1:c * L + L, :] for c in range(nck)]
    a_last_x = jnp.concatenate([jnp.broadcast_to(a, (L, SSD_WIDTH)) for a in a_last], axis=0)
    dat = jax.nn.softplus(dtt_ref[0, DT_LANE0:DT_LANE0 + SSD_HEADS, :] + dtbt_ref[...]) * a_col
    acs_t = [_dot2_l(dat[:, chunk(c)], triu_ref[...]) for c in range(nck)]
    xdt = xs * dt_x
    xdo = (xdt * jnp.exp2(a_last_x - acs_x)).astype(BF16)
    xdt_b = xdt.astype(BF16)
    y_ref[...] = xs * dsk_ref[...]
    pre = jnp.exp2(acs_x)

    for gi in range(SSD_GROUPS):
        gs = slice(gi * gw, (gi + 1) * gw)
        bm = xc_ref[:, SSD_WIDTH + gi * N:SSD_WIDTH + (gi + 1) * N]
        cm_b = xc_ref[:, SSD_WIDTH + SSD_GROUPS * N + gi * N:SSD_WIDTH + SSD_GROUPS * N + (gi + 1) * N].astype(BF16)
        bm_b = bm.astype(BF16)
        h_c = h_ref[gi]
        h_in = []
        for c in range(nck):
            h_in.append(h_c.astype(BF16))
            h_c = h_c * jnp.exp2(a_last[c][:, gs]) + _dot(bm[chunk(c), :].T.astype(BF16), xdo[chunk(c), gs])
        h_ref[gi] = h_c
        for c in range(nck):
            rs = chunk(c)
            cbm = _nt(cm_b[rs, :], bm_b[rs, :])
            y_ref[rs, gs] += _dot(cm_b[rs, :], h_in[c]) * pre[rs, gs]
            for e in range(E):
                h = gi * E + e
                hs = slice(h * P, (h + 1) * P)
                d = acs_w[rs, h * LANES:(h + 1) * LANES] - acs_t[c][h:h + 1, :]
                dec = jnp.exp2(jnp.where(causal, d, NEG))
                y_ref[rs, hs] += _dot((cbm * dec).astype(BF16), xdt_b[rs, hs])

    y = y_ref[...] * _silu(z_ref[...])
    ms = jnp.mean(y * y, axis=-1, keepdims=True)
    mix_ref[:, 0:SSD_WIDTH] = (y * lax.rsqrt(ms + EPS) * gn_ref[...]).astype(BF16)
    acc = acc_ref[...] + _dot(mix_ref[:, 0:SSD_WIDTH], w_ref[w_b:w_c, :])
    ms = jnp.mean(acc * acc, axis=-1, keepdims=True)
    o_ref[...] = acc * lax.rsqrt(ms + EPS) * g_ref[...]


def _ssd_out(x2, oa, qx, gx, km, vm, w_out, g_final, xbc, z, sm, smt, consts, *, batch, seq):
    rows = batch * seq
    ts = ROW_TILE
    tps = seq // ts
    mlen = km.shape[1]
    row = lambda n: pl.BlockSpec((ts, n), lambda r: (r, 0))
    full = lambda a: pl.BlockSpec(a.shape, lambda r: (0,) * a.ndim)
    mem = pl.BlockSpec((1, mlen, XA_WIDTH), lambda r: (r // tps, 0, 0))
    return pl.pallas_call(
        functools.partial(_ssdout_body, ts=ts, tiles_per_seq=tps),
        grid=(rows // ts,),
        in_specs=[row(D_MODEL), row(NSA_WIDTH), row(XA_WIDTH), row(XA_WIDTH), mem, mem,
                  full(w_out), full(g_final),
                  row(CONV_DIM), row(SSD_WIDTH), row(LANES), pl.BlockSpec((1, LANES, ts), lambda r: (r, 0, 0))]
                 + [full(a) for a in consts],
        out_specs=row(D_MODEL),
        out_shape=jax.ShapeDtypeStruct((rows, D_MODEL), F32),
        scratch_shapes=[pltpu.VMEM((ts, D_MODEL), F32), pltpu.VMEM((ts, SSD_WIDTH + XA_WIDTH), BF16),
                        pltpu.VMEM((ts + 8, CONV_DIM), F32), pltpu.VMEM((ts, CONV_DIM), F32),
                        pltpu.VMEM((ts, SSD_WIDTH), F32),
                        pltpu.VMEM((SSD_GROUPS, SSD_STATE, SSD_WIDTH // SSD_GROUPS), F32)],
        compiler_params=pltpu.CompilerParams(dimension_semantics=("arbitrary",),
                                             vmem_limit_bytes=VMEM_LIMIT),
        name="ssd_out",
    )(x2, oa, qx, gx, km, vm, w_out, g_final, xbc, z, sm, smt, *consts)


def _memkv_body(mem_ref, g_ref, w_ref, k_ref, v_ref):
    x = mem_ref[0]
    ms = jnp.mean(x * x, axis=-1, keepdims=True)
    xn = (x * lax.rsqrt(ms + EPS) * g_ref[...]).astype(BF16)
    k_ref[0] = _dot(xn, w_ref[:, 0:XA_WIDTH]).astype(BF16)
    v_ref[0] = _dot(xn, w_ref[:, XA_WIDTH:2 * XA_WIDTH]).astype(BF16)


def _mem_kv(mem, g_mem, w_kv):
    batch, mlen, _ = mem.shape
    full = lambda a: pl.BlockSpec(a.shape, lambda b: (0,) * a.ndim)
    out = pl.BlockSpec((1, mlen, XA_WIDTH), lambda b: (b, 0, 0))
    shp = jax.ShapeDtypeStruct((batch, mlen, XA_WIDTH), BF16)
    return pl.pallas_call(
        _memkv_body,
        grid=(batch,),
        in_specs=[pl.BlockSpec((1, mlen, D_MODEL), lambda b: (b, 0, 0)), full(g_mem), full(w_kv)],
        out_specs=[out, out],
        out_shape=[shp, shp],
        compiler_params=pltpu.CompilerParams(dimension_semantics=("parallel",),
                                             vmem_limit_bytes=VMEM_LIMIT),
        name="mem_kv",
    )(mem, g_mem, w_kv)


def _permute_w_in(w, layer):
    rb = 128
    return pl.pallas_call(
        _wprep_body,
        grid=(w.shape[1] // rb,),
        in_specs=[pl.BlockSpec((1, rb, w.shape[2]), lambda r: (layer, r, 0))],
        out_specs=pl.BlockSpec((rb, N_PROJ), lambda r: (r, 0)),
        out_shape=jax.ShapeDtypeStruct((w.shape[1], N_PROJ), BF16),
        compiler_params=pltpu.CompilerParams(dimension_semantics=("parallel",),
                                             vmem_limit_bytes=VMEM_LIMIT),
        name="w_prep",
    )(w)


def _wprep_body(w_ref, o_ref):
    sizes = [NSA_WIDTH] + [KV_WIDTH] * 6 + [NSA_HEADS * 3, NSA_WIDTH, SSD_WIDTH, CONV_DIM, SSD_HEADS,
                                          XA_WIDTH, XA_WIDTH]
    offs = [0]
    for s in sizes:
        offs.append(offs[-1] + s)
    w = w_ref[0]

    def move(dst, lo, hi):
        o_ref[:, dst:dst + hi - lo] = w[:, lo:hi].astype(BF16)

    move(OFF_Q, offs[0], offs[7])
    move(OFF_GA, offs[8], offs[11])
    move(OFF_QX, offs[12], offs[14])
    small = jnp.concatenate(
        [w[:, offs[7]:offs[8]], w[:, offs[11]:offs[12]],
         jnp.zeros((w.shape[0], LANES - NSA_HEADS * 3 - SSD_HEADS), w.dtype)], axis=1)
    o_ref[:, OFF_SM:OFF_SM + LANES] = small.astype(BF16)


def _lane_row(vals, lane0):
    return jnp.zeros((1, LANES), F32).at[0, lane0:lane0 + vals.shape[0]].set(vals.astype(F32))


def _forward(x, mem, positions, g_in, w_in, cmp_pos_k, w_cmp1_k, w_cmp2_k, cmp_pos_v, w_cmp1_v,
             w_cmp2_v, conv_w, conv_b, dt_bias, a_log, d_skip, g_ssd_norm, g_mem, w_mem_kv, w_out,
             g_final):
    batch, seq, _ = x.shape
    ncp = seq // CMP_STRIDE
    n_slc = seq // SLC_BLOCK
    assert n_slc <= SLC_SLOTS and seq % ROW_TILE == 0 and ncp <= NSA_TILE
    topk = min(SLC_TOPK, n_slc)
    rows = batch * seq

    inv = ROPE_THETA ** (-jnp.arange(0, ROPE_DIM, 2, dtype=F32) / ROPE_DIM)
    head_inv = jnp.concatenate([inv, inv, jnp.zeros((HEAD_DIM - ROPE_DIM,), F32)])
    invl = jnp.tile(head_inv, LANES // HEAD_DIM)[None, :]
    head_sgn = jnp.concatenate([-jnp.ones((ROPE_HALF,), F32), jnp.ones((ROPE_HALF,), F32),
                                jnp.zeros((HEAD_DIM - ROPE_DIM,), F32)])
    sgn = jnp.tile(head_sgn, LANES // HEAD_DIM)[None, :]
    inv_rows = jnp.concatenate([inv, jnp.zeros((ROPE_FREQ_ROWS - ROPE_HALF,), F32)])[:, None]
    d_ix = jnp.arange(LANES) % HEAD_DIM
    freq_of_lane = jnp.where(d_ix < ROPE_DIM, d_ix % ROPE_HALF, ROPE_HALF)
    spread_cos = (freq_of_lane[:, None] == jnp.arange(ROPE_FREQ_ROWS)[None, :]).astype(F32)
    spread_sin = spread_cos * sgn[0][:, None]

    x2 = x.reshape(rows, D_MODEL)
    pos3 = positions.reshape(rows // ROW_TILE, 1, ROW_TILE)
    h = x2
    for l in range(g_in.shape[0]):
        (q, ck, cv, ksa, vst, kw, vwt, ga, z, xbc, qx, gx, sm, smt) = _in_proj(
            h, pos3, g_in[l][None, :], _permute_w_in(w_in, l), inv_rows,
            spread_cos.astype(BF16), spread_sin.astype(BF16), batch=batch, seq=seq)

        cmp_end = jnp.minimum(jnp.arange(ncp) * CMP_STRIDE + CMP_LEN - 1, seq - 1)
        posc = positions[:, cmp_end][:, :, None]
        pad_w2 = lambda w: jnp.pad(w, ((0, 0), (0, LANES - HEAD_DIM))).astype(BF16)
        pos_rows = lambda p: jnp.broadcast_to(p.reshape(1, CMP_LEN * HEAD_DIM), (8, CMP_LEN * HEAD_DIM)).astype(BF16)
        kc, vc = _compress(ck, cv, posc,
                           w_cmp1_k[l].astype(BF16), pad_w2(w_cmp2_k[l]), pos_rows(cmp_pos_k[l]),
                           w_cmp1_v[l].astype(BF16), pad_w2(w_cmp2_v[l]), pos_rows(cmp_pos_v[l]),
                           invl, sgn, batch=batch, ncp=ncp)
        n_ix = jnp.arange(ncp)[None, :]
        j_ix = jnp.arange(SLC_SLOTS)[:, None]
        ovl = ((n_ix * CMP_STRIDE < j_ix * SLC_BLOCK + SLC_BLOCK)
               & (n_ix * CMP_STRIDE + CMP_LEN > j_ix * SLC_BLOCK)
               & (n_ix < ncp - (CMP_LEN // CMP_STRIDE - 1))).astype(BF16)
        bg = (batch, KV_GROUPS)
        cat = jnp.concatenate(
            [jnp.swapaxes(vc, 2, 3), jnp.ones(bg + (1, ncp), BF16),
             jnp.zeros(bg + (CAT_OVL_ROW0 - HEAD_DIM - 1, ncp), BF16),
             jnp.broadcast_to(ovl, bg + ovl.shape),
             jnp.zeros(bg + (CAT_ROWS - CAT_OVL_ROW0 - SLC_SLOTS, ncp), BF16)], axis=2)
        o_a = _nsa(q, kc, cat, ksa, vst, kw, vwt, ga, sm,
                   batch=batch, seq=seq, ncp=ncp, topk=topk)

        head_of_lane = jnp.arange(SSD_WIDTH) // SSD_HEAD_DIM
        k_ix = jnp.arange(LANES)[:, None]
        eh = (k_ix == DT_LANE0 + head_of_lane[None, :]).astype(BF16)
        ehw = (k_ix == DT_LANE0 + (jnp.arange(SSD_HEADS * LANES) // LANES)[None, :]).astype(BF16)
        t_ix = jnp.arange(SSD_CHUNK)
        tril = (t_ix[None, :] <= t_ix[:, None]).astype(BF16)
        ssd_consts = [conv_w[l], conv_b[l][None, :],
                      _lane_row(dt_bias[l], DT_LANE0), dt_bias[l].astype(F32)[:, None],
                      _lane_row(a_log[l], DT_LANE0), a_log[l].astype(F32)[:, None],
                      jnp.repeat(d_skip[l].astype(F32), SSD_HEAD_DIM)[None, :], g_ssd_norm[l][None, :],
                      eh, ehw, tril, tril.T]

        km, vm = _mem_kv(mem, g_mem[l][None, :], w_mem_kv[l].astype(BF16))
        assert g_in.shape[0] == 1, "the final RMSNorm is fused into the (single) layer's last kernel"
        h = _ssd_out(h, o_a, qx, gx, km, vm, w_out[l].astype(BF16), g_final[None, :],
                     xbc, z, sm, smt, ssd_consts, batch=batch, seq=seq)
    return h.reshape(batch, seq, D_MODEL)


def kernel(x, mem, positions, g_in, w_in, cmp_pos_k, w_cmp1_k, w_cmp2_k, cmp_pos_v, w_cmp1_v, w_cmp2_v,
           conv_w, conv_b, dt_bias, a_log, d_skip, g_ssd_norm, g_mem, w_mem_kv, w_out, g_final):
    return _forward(x, mem, positions, g_in, w_in, cmp_pos_k, w_cmp1_k, w_cmp2_k, cmp_pos_v, w_cmp1_v,
                    w_cmp2_v, conv_w, conv_b, dt_bias, a_log, d_skip, g_ssd_norm, g_mem, w_mem_kv,
                    w_out, g_final)
```

```python
import functools

import jax
import jax.numpy as jnp
from jax import lax
from jax.experimental import pallas as pl
from jax.experimental.pallas import tpu as pltpu

F32 = jnp.float32
BF16 = jnp.bfloat16
ACT = jnp.bfloat16
I32 = jnp.int32

D_MODEL = 1024
NSA_HEADS = 8
HEAD_DIM = 64
KV_GROUPS = 2
HEADS_PER_GROUP = NSA_HEADS // KV_GROUPS
NSA_WIDTH = NSA_HEADS * HEAD_DIM
KV_WIDTH = KV_GROUPS * HEAD_DIM
CMP_LEN = 32
CMP_STRIDE = 16
CMP_HIDDEN = 256
SLC_BLOCK = 64
SLC_TOPK = 16
SLC_SLOTS = 64
WINDOW = 512
BIG = 1e9
NEG = -1e30

SSD_HEADS = 8
SSD_HEAD_DIM = 64
SSD_WIDTH = SSD_HEADS * SSD_HEAD_DIM
SSD_GROUPS = 2
SSD_STATE = 128
SSD_CONV = 4
SSD_CHUNK = 128
CONV_DIM = SSD_WIDTH + 2 * SSD_GROUPS * SSD_STATE

XA_HEADS = 4
XA_HEAD_DIM = 128
XA_WIDTH = XA_HEADS * XA_HEAD_DIM
MIX_WIDTH = NSA_WIDTH + SSD_WIDTH + XA_WIDTH

ROPE_THETA = 500000.0
ROPE_DIM = HEAD_DIM // 4
ROPE_HALF = ROPE_DIM // 2
ROPE_FREQ_ROWS = 16
EPS = 1e-6

LANES = 128
GATE_LANE0 = 0
DT_LANE0 = 24

OFF_Q = 0
OFF_KV = OFF_Q + NSA_WIDTH
OFF_GA = OFF_KV + 6 * KV_WIDTH
OFF_Z = OFF_GA + NSA_WIDTH
OFF_XBC = OFF_Z + SSD_WIDTH
OFF_QX = OFF_XBC + CONV_DIM
OFF_GX = OFF_QX + XA_WIDTH
OFF_SM = OFF_GX + XA_WIDTH
N_PROJ = OFF_SM + LANES

ROW_TILE = 512
NSA_TILE = 256
V_ROWS = 80
LOG2E = 1.4426950408889634
CAT_OVL_ROW0 = 80
CAT_ROWS = 144
VMEM_LIMIT = 56 * 1024 * 1024


def _nt(a, b):
    return lax.dot_general(a, b, (((1,), (1,)), ((), ())), preferred_element_type=F32)


def _dot(a, b):
    return jnp.dot(a, b, preferred_element_type=F32)


def _split2(x):
    hi = x.astype(BF16)
    return hi, (x - hi.astype(F32)).astype(BF16)


def _dot2_l(x, w):
    hi, lo = _split2(x)
    return _dot(hi, w) + _dot(lo, w)


def _dot2_r(w, x):
    hi, lo = _split2(x)
    return _dot(w, hi) + _dot(w, lo)


def _silu(x):
    h = 0.5 * x
    return h + h * jnp.tanh(h)


def _rope(a, cs, sn, first):
    r = jnp.where(first, pltpu.roll(a, LANES - ROPE_HALF, 1), pltpu.roll(a, ROPE_HALF, 1))
    return a * cs + r * sn


def _rope_tables(pos_f32, invl, sgn):
    ang = pos_f32 * invl
    return jnp.cos(ang), jnp.sin(ang) * sgn


def _inproj_body(x_ref, pos_ref, g_ref, w_ref, invc_ref, ec_ref, es_ref,
                 q_ref, ck_ref, cv_ref, ksa_ref, vst_ref, kw_ref, vwt_ref,
                 ga_ref, z_ref, xbc_ref, qx_ref, gx_ref, sm_ref, smt_ref, sk_ref, sv_ref, *, tm, tiles_per_seq):
    x = x_ref[...]
    ms = jnp.mean(x * x, axis=-1, keepdims=True)
    xn = (x * lax.rsqrt(ms + EPS) * g_ref[...]).astype(BF16)

    def mm(lo, n):
        return _dot(xn, w_ref[:, lo:lo + n])

    ga_ref[...] = mm(OFF_GA, NSA_WIDTH).astype(ACT)
    z_ref[...] = mm(OFF_Z, SSD_WIDTH).astype(ACT)

    ang = invc_ref[...] * pos_ref[0].astype(F32)
    cs = _dot2_r(ec_ref[...], jnp.cos(ang)).T
    sn = _dot2_r(es_ref[...], jnp.sin(ang)).T
    lane = lax.broadcasted_iota(I32, (tm, LANES), 1)
    first = (lane % HEAD_DIM) < ROPE_HALF

    qf = mm(OFF_Q, NSA_WIDTH)
    for c in range(NSA_WIDTH // LANES):
        sl = slice(c * LANES, (c + 1) * LANES)
        q_ref[:, sl] = (_rope(qf[:, sl], cs, sn, first) * (HEAD_DIM ** -0.5 * LOG2E)).astype(BF16)

    kv = mm(OFF_KV, 6 * KV_WIDTH)
    k_c, v_c, k_s, v_s, k_w, v_w = [kv[:, n * KV_WIDTH:(n + 1) * KV_WIDTH] for n in range(6)]
    k_s = _rope(k_s, cs, sn, first)
    k_w = _rope(k_w, cs, sn, first)
    s_base = (pl.program_id(0) % tiles_per_seq) * tm
    blk = (s_base + lax.broadcasted_iota(I32, (tm, LANES), 0)) // SLC_BLOCK
    onehot = jnp.where(lane - HEAD_DIM == blk, 1.0, 0.0)
    low = lane < HEAD_DIM
    vs_t = v_s.T
    vw_t = v_w.T
    tail = jnp.where(lax.broadcasted_iota(I32, (V_ROWS - HEAD_DIM, NSA_TILE), 0) == 0, 1.0, 0.0).astype(BF16)
    for g in range(KV_GROUPS):
        gs = slice(g * HEAD_DIM, (g + 1) * HEAD_DIM)
        kw_ref[0, g] = k_w[:, gs].astype(BF16)
        ks_g = k_s if g == 0 else pltpu.roll(k_s, HEAD_DIM, 1)
        ksa_ref[0, g] = jnp.where(low, ks_g, onehot).astype(BF16)
        for c in range(tm // NSA_TILE):
            cols = slice(c * NSA_TILE, (c + 1) * NSA_TILE)
            vst_ref[0, g, c, 0:HEAD_DIM, :] = vs_t[gs, cols].astype(BF16)
            vst_ref[0, g, c, HEAD_DIM:V_ROWS, :] = tail
            vwt_ref[0, g, c, 0:HEAD_DIM, :] = vw_t[gs, cols].astype(BF16)
            vwt_ref[0, g, c, HEAD_DIM:V_ROWS, :] = tail

    sk_ref[...] = k_c
    sv_ref[...] = v_c
    nchunk = tm // CMP_STRIDE
    for t in range(CMP_STRIDE):
        dst = slice(t * HEAD_DIM, (t + 1) * HEAD_DIM)
        kt = sk_ref[pl.ds(t, nchunk, stride=CMP_STRIDE), :].astype(BF16)
        vt = sv_ref[pl.ds(t, nchunk, stride=CMP_STRIDE), :].astype(BF16)
        for g in range(KV_GROUPS):
            gs = slice(g * HEAD_DIM, (g + 1) * HEAD_DIM)
            ck_ref[0, g, :, dst] = kt[:, gs]
            cv_ref[0, g, :, dst] = vt[:, gs]

    xbc_ref[...] = mm(OFF_XBC, CONV_DIM).astype(ACT)
    qx_ref[...] = (mm(OFF_QX, XA_WIDTH) * (XA_HEAD_DIM ** -0.5 * LOG2E)).astype(BF16)
    gx_ref[...] = mm(OFF_GX, XA_WIDTH).astype(ACT)
    sm = mm(OFF_SM, LANES)
    sm_ref[...] = sm
    smt_ref[0] = sm.T


def _in_proj(x2, pos3, g_in, w_p, invc, ec, es, *, batch, seq):
    rows = batch * seq
    tm = ROW_TILE
    tps = seq // tm
    row = lambda n: pl.BlockSpec((tm, n), lambda r: (r, 0))
    full = lambda a: pl.BlockSpec(a.shape, lambda r: (0,) * a.ndim)
    grp = lambda n: pl.BlockSpec((1, KV_GROUPS, tm, n), lambda r: (r // tps, 0, r % tps, 0))
    grp_shape = lambda n: jax.ShapeDtypeStruct((batch, KV_GROUPS, seq, n), BF16)
    chunk = pl.BlockSpec((1, KV_GROUPS, tm // CMP_STRIDE, CMP_STRIDE * HEAD_DIM),
                         lambda r: (r // tps, 0, r % tps, 0))
    chunk_shape = jax.ShapeDtypeStruct((batch, KV_GROUPS, seq // CMP_STRIDE, CMP_STRIDE * HEAD_DIM), BF16)
    vtile = pl.BlockSpec((1, KV_GROUPS, tm // NSA_TILE, V_ROWS, NSA_TILE), lambda r: (r // tps, 0, r % tps, 0, 0))
    vtile_shape = jax.ShapeDtypeStruct((batch, KV_GROUPS, seq // NSA_TILE, V_ROWS, NSA_TILE), BF16)
    flat = lambda n, dt: jax.ShapeDtypeStruct((rows, n), dt)
    return pl.pallas_call(
        functools.partial(_inproj_body, tm=tm, tiles_per_seq=tps),
        grid=(rows // tm,),
        in_specs=[row(D_MODEL), pl.BlockSpec((1, 1, tm), lambda r: (r, 0, 0)),
                  full(g_in), full(w_p), full(invc), full(ec), full(es)],
        out_specs=[row(NSA_WIDTH), chunk, chunk, grp(LANES), vtile, grp(HEAD_DIM), vtile,
                   row(NSA_WIDTH), row(SSD_WIDTH), row(CONV_DIM),
                   row(XA_WIDTH), row(XA_WIDTH), row(LANES),
                   pl.BlockSpec((1, LANES, tm), lambda r: (r, 0, 0))],
        out_shape=[flat(NSA_WIDTH, BF16), chunk_shape, chunk_shape, grp_shape(LANES), vtile_shape,
                   grp_shape(HEAD_DIM), vtile_shape,
                   flat(NSA_WIDTH, ACT), flat(SSD_WIDTH, ACT), flat(CONV_DIM, ACT),
                   flat(XA_WIDTH, BF16), flat(XA_WIDTH, ACT), flat(LANES, F32),
                   jax.ShapeDtypeStruct((rows // tm, LANES, tm), F32)],
        scratch_shapes=[pltpu.VMEM((tm, KV_WIDTH), F32), pltpu.VMEM((tm, KV_WIDTH), F32)],
        compiler_params=pltpu.CompilerParams(dimension_semantics=("parallel",),
                                             vmem_limit_bytes=VMEM_LIMIT),
        name="in_proj",
    )(x2, pos3, g_in, w_p, invc, ec, es)


def _compress_body(ck_ref, cv_ref, posc_ref, w1k_ref, w2k_ref, pk_ref, w1v_ref, w2v_ref, pv_ref,
                   invl_ref, sgn_ref, kc_ref, vc_ref, *, ncp):
    half = CMP_STRIDE * HEAD_DIM

    def mlp(c_ref, w1_ref, w2_ref, p_ref):
        c = c_ref[0, 0]
        a = _dot(c, w1_ref[0:half, :])
        b = _dot(c, w1_ref[half:2 * half, :])
        bias = _dot(p_ref[...], w1_ref[...])[0:1, :]
        h = a + pltpu.roll(b, ncp - 1, 0) + bias
        return _dot(_silu(h).astype(BF16), w2_ref[...])

    kc = mlp(ck_ref, w1k_ref, w2k_ref, pk_ref)
    vc = mlp(cv_ref, w1v_ref, w2v_ref, pv_ref)
    cs, sn = _rope_tables(posc_ref[0].astype(F32), invl_ref[...], sgn_ref[...])
    lane = lax.broadcasted_iota(I32, (ncp, LANES), 1)
    kc = _rope(kc, cs, sn, (lane % HEAD_DIM) < ROPE_HALF)
    kc_ref[0, 0] = kc[:, 0:HEAD_DIM].astype(BF16)
    vc_ref[0, 0] = vc[:, 0:HEAD_DIM].astype(BF16)


def _compress(ck, cv, posc, w1k, w2k, pk, w1v, w2v, pv, invl, sgn, *, batch, ncp):
    chunk = pl.BlockSpec((1, 1, ncp, CMP_STRIDE * HEAD_DIM), lambda b, g: (b, g, 0, 0))
    full = lambda a: pl.BlockSpec(a.shape, lambda b, g: (0,) * a.ndim)
    out = pl.BlockSpec((1, 1, ncp, HEAD_DIM), lambda b, g: (b, g, 0, 0))
    shp = jax.ShapeDtypeStruct((batch, KV_GROUPS, ncp, HEAD_DIM), BF16)
    return pl.pallas_call(
        functools.partial(_compress_body, ncp=ncp),
        grid=(batch, KV_GROUPS),
        in_specs=[chunk, chunk, pl.BlockSpec((1, ncp, 1), lambda b, g: (b, 0, 0)),
                  full(w1k), full(w2k), full(pk), full(w1v), full(w2v), full(pv),
                  full(invl), full(sgn)],
        out_specs=[out, out],
        out_shape=[shp, shp],
        compiler_params=pltpu.CompilerParams(dimension_semantics=("parallel", "parallel"),
                                             vmem_limit_bytes=VMEM_LIMIT),
        name="compress",
    )(ck, cv, posc, w1k, w2k, pk, w1v, w2v, pv, invl, sgn)


def _nsa_body(q_ref, kc_ref, cat_ref, ksa_ref, vst_ref, kw_ref, vwt_ref, ga_ref, sm_ref,
              o_ref, qa_ref, m_ref, acc_ref, oc_ref, sc_ref, mc_ref, gt_ref, gs_ref, *, tq, ncp, topk):
    g = pl.program_id(1)
    i = pl.program_id(2)
    s0 = i * tq
    nh = HEADS_PER_GROUP
    rows = nh * tq

    def col_max(s):
        n = s.shape[0]
        while n > 8:
            n //= 2
            s = jnp.maximum(s[0:n], s[n:2 * n])
        return jnp.max(s, axis=0, keepdims=True)

    def reset(br):
        m_ref[br] = jnp.full((1, rows), NEG, F32)
        acc_ref[br] = jnp.zeros((V_ROWS, rows), F32)

    def consume(br, s, vt_tile, mask, s_max=None):
        if mask is not None:
            s = jnp.where(mask, s, NEG)
        if s_max is None:
            s_max = col_max(s)
        m_prev = m_ref[br]
        m_new = jnp.maximum(m_prev, s_max)
        alpha = jnp.exp2(m_prev - m_new)
        p = jnp.exp2(s - m_new).astype(BF16)
        acc_ref[br] = alpha * acc_ref[br] + _dot(vt_tile, p)
        m_ref[br] = m_new

    def sel_scores(j):
        rows_j = pl.ds(pl.multiple_of(j * tq, tq), tq)
        return _nt(ksa_ref[0, 0, rows_j, :], qa_ref[...])

    def win_scores(j):
        rows_j = pl.ds(pl.multiple_of(j * tq, tq), tq)
        return _nt(kw_ref[0, 0, rows_j, :], qa_ref[:, 0:HEAD_DIM])

    q4 = q_ref[...]
    for e in range(nh):
        qa_ref[e * tq:(e + 1) * tq, 0:HEAD_DIM] = q4[:, e * HEAD_DIM:(e + 1) * HEAD_DIM]

    k_io = lax.broadcasted_iota(I32, (tq, rows), 0)
    t_io2 = lax.broadcasted_iota(I32, (tq, rows), 1) % tq
    diag = k_io <= t_io2
    j1 = jnp.maximum(i - 1, 0)
    j2 = jnp.maximum(i - 2, 0)
    cmp_per_blk = SLC_BLOCK // CMP_STRIDE

    def produce(slot, j):
        s = sel_scores(j)
        sc_ref[slot] = s
        mc_ref[slot] = col_max(s)

    def first_tile_and_gates():
        reset(0)
        produce(0, 0)
        gates_t = jax.nn.sigmoid(sm_ref[...]).T
        gs_ref[...] = _silu(ga_ref[...].astype(F32))

        def gate(e, c):
            c0 = GATE_LANE0 + e * 3 + c
            c1 = c0 + nh * 3
            return jnp.where(g == 0, gates_t[c0:c0 + 1, :], gates_t[c1:c1 + 1, :])

        for e in range(nh):
            cs = slice(e * tq, (e + 1) * tq)
            o_w = acc_ref[1, 0:HEAD_DIM, cs] / acc_ref[1, HEAD_DIM:HEAD_DIM + 1, cs]
            oc_ref[:, cs] = gate(e, 0) * oc_ref[:, cs] + gate(e, 2) * o_w
            gt_ref[e:e + 1, :] = gate(e, 1)

    def prologue(cap):
        nc = min(ncp, cap * cmp_per_blk)
        n_io = lax.broadcasted_iota(I32, (nc, rows), 0)
        t_io = s0 + lax.broadcasted_iota(I32, (nc, rows), 1) % tq
        s = jnp.where((n_io * CMP_STRIDE + (CMP_LEN - 1)) <= t_io,
                      _nt(kc_ref[0, 0, 0:nc, :], qa_ref[:, 0:HEAD_DIM]), NEG)
        sc_ref[0, 0:nc, :] = s
        mc_ref[0] = col_max(s)
        sc_ref[1] = win_scores(i)
        p = jnp.exp2(sc_ref[0, 0:nc, :] - mc_ref[0]).astype(BF16)
        big = _dot(cat_ref[0, 0, :, 0:nc], p)
        sc_ref[0] = win_scores(j1)
        t_row = s0 + lax.broadcasted_iota(I32, (1, rows), 1) % tq
        inv = jnp.where(t_row >= CMP_LEN - 1, 1.0 / jnp.maximum(big[HEAD_DIM:HEAD_DIM + 1, :], 1e-30), 0.0)
        oc_ref[...] = big[0:HEAD_DIM, :] * inv
        imp_x = big[CAT_OVL_ROW0:CAT_OVL_ROW0 + cap, :] * inv
        imp = imp_x[:, 0:tq]
        for e in range(1, nh):
            imp = imp + imp_x[:, e * tq:(e + 1) * tq]

        j_io = lax.broadcasted_iota(I32, (cap, tq), 0)
        cur = (s0 + lax.broadcasted_iota(I32, (cap, tq), 1)) // SLC_BLOCK
        valid = j_io <= cur
        if cap > topk:
            TAKEN = -3e38
            TAKEN_BELOW = -1e37
            forced = (j_io == 0) | (j_io == cur) | (j_io == cur - 1)
            j_f = j_io.astype(F32)
            v = jnp.where(forced, TAKEN, jnp.where(valid, imp, -BIG))
            for _ in range(topk - 3):
                mx = jnp.max(v, axis=0, keepdims=True)
                idx = jnp.min(jnp.where(v == mx, j_f, float(SLC_SLOTS)), axis=0, keepdims=True)
                v = jnp.where(j_f == idx, TAKEN, v)
            valid = valid & (v < TAKEN_BELOW)
        selb = jnp.where(valid, 0.0, NEG)
        fill = [jnp.full((SLC_SLOTS - cap, tq), NEG, F32)] if cap < SLC_SLOTS else []
        selb_t = jnp.concatenate([jnp.zeros((SLC_SLOTS, tq), F32), selb] + fill, axis=0).T
        selb_t = selb_t.astype(BF16)

        reset(1)
        consume(1, sc_ref[1], vwt_ref[0, 0, i], diag)
        sc_ref[1] = win_scores(j2)
        consume(1, sc_ref[0], vwt_ref[0, 0, j1], i >= 1)
        consume(1, sc_ref[1], vwt_ref[0, 0, j2], (k_io > t_io2) & (i >= 2))
        for e in range(nh):
            qa_ref[e * tq:(e + 1) * tq, HEAD_DIM:LANES] = selb_t[:, HEAD_DIM:LANES]
        first_tile_and_gates()

    n_vis = (i + 1) * (tq // SLC_BLOCK)
    lo = 0
    for cap in sorted({min(topk, SLC_SLOTS), SLC_SLOTS // 2, SLC_SLOTS}):
        pl.when((n_vis > lo) & (n_vis <= cap))(functools.partial(prologue, cap))
        lo = cap

    def sel_pair(j):
        produce(1, j + 1)
        consume(0, sc_ref[0], vst_ref[0, 0, j], None, mc_ref[0])
        produce(0, j + 2)
        consume(0, sc_ref[1], vst_ref[0, 0, j + 1], None, mc_ref[1])

    done = 0
    for width in (8, 4, 2):
        start = done

        def block(jj, carry, width=width, start=start):
            for u in range(0, width, 2):
                sel_pair(start + width * jj + u)
            return carry

        trips = (i - start) // width
        lax.fori_loop(0, trips, block, 0)
        done = start + trips * width

    @pl.when(i % 2 == 0)
    def _():
        consume(0, sc_ref[0], vst_ref[0, 0, i], diag)

    @pl.when(i % 2 == 1)
    def _():
        sc_ref[1] = sel_scores(i)
        consume(0, sc_ref[0], vst_ref[0, 0, i - 1], None, mc_ref[0])
        consume(0, sc_ref[1], vst_ref[0, 0, i], diag)

    def head_out(e):
        cs = slice(e * tq, (e + 1) * tq)
        o_s = acc_ref[0, 0:HEAD_DIM, cs] / acc_ref[0, HEAD_DIM:HEAD_DIM + 1, cs]
        return oc_ref[:, cs] + gt_ref[e:e + 1, :] * o_s

    for pr in range(nh // 2):
        ls = slice(pr * LANES, (pr + 1) * LANES)
        pair = jnp.concatenate([head_out(2 * pr), head_out(2 * pr + 1)], axis=0).T
        o_ref[:, ls] = (pair * gs_ref[:, ls]).astype(BF16)


def _nsa(q, kc, cat, ksa, vst, kw, vwt, ga, sm, *, batch, seq, ncp, topk):
    tq = NSA_TILE
    assert WINDOW == 2 * tq
    nq = seq // tq
    gw = HEADS_PER_GROUP * HEAD_DIM
    rows = HEADS_PER_GROUP * tq
    per_bg = lambda *shape: pl.BlockSpec((1, 1) + shape, lambda b, g, i: (b, g) + (0,) * len(shape))
    return pl.pallas_call(
        functools.partial(_nsa_body, tq=tq, ncp=ncp, topk=topk),
        grid=(batch, KV_GROUPS, nq),
        in_specs=[pl.BlockSpec((tq, gw), lambda b, g, i: (b * nq + i, g)),
                  per_bg(ncp, HEAD_DIM), per_bg(CAT_ROWS, ncp),
                  per_bg(seq, LANES), per_bg(nq, V_ROWS, tq), per_bg(seq, HEAD_DIM),
                  per_bg(nq, V_ROWS, tq),
                  pl.BlockSpec((tq, gw), lambda b, g, i: (b * nq + i, g)),
                  pl.BlockSpec((tq, LANES), lambda b, g, i: (b * nq + i, 0))],
        out_specs=pl.BlockSpec((tq, gw), lambda b, g, i: (b * nq + i, g)),
        out_shape=jax.ShapeDtypeStruct((batch * seq, NSA_WIDTH), BF16),
        scratch_shapes=[pltpu.VMEM((rows, LANES), BF16),
                        pltpu.VMEM((2, 1, rows), F32),
                        pltpu.VMEM((2, V_ROWS, rows), F32),
                        pltpu.VMEM((HEAD_DIM, rows), F32),
                        pltpu.VMEM((2, tq, rows), F32), pltpu.VMEM((2, 1, rows), F32),
                        pltpu.VMEM((2 * HEADS_PER_GROUP, tq), F32), pltpu.VMEM((tq, gw), F32)],
        compiler_params=pltpu.CompilerParams(
            dimension_semantics=("parallel", "parallel", "arbitrary"),
            vmem_limit_bytes=VMEM_LIMIT),
        name="nsa",
    )(q, kc, cat, ksa, vst, kw, vwt, ga, sm)


def _ssdout_body(x_ref, oa_ref, qx_ref, gx_ref, k_ref, v_ref, w_ref, g_ref,
                 xbc_ref, z_ref, sm_ref, dtt_ref, cw_ref, cb_ref, dtb_ref, dtbt_ref, al_ref, alt_ref,
                 dsk_ref, gn_ref, eh_ref, ehw_ref, tril_ref, triu_ref,
                 o_ref, acc_ref, mix_ref, ext_ref, xc_ref, y_ref, h_ref, *, ts, tiles_per_seq):
    L = SSD_CHUNK
    N = SSD_STATE
    P = SSD_HEAD_DIM
    E = SSD_HEADS // SSD_GROUPS
    gw = E * P
    pad = 8
    w_b = NSA_WIDTH
    w_c = NSA_WIDTH + SSD_WIDTH

    @pl.when(pl.program_id(0) % tiles_per_seq == 0)
    def _():
        ext_ref[0:pad, :] = jnp.zeros((pad, CONV_DIM), F32)
        h_ref[...] = jnp.zeros(h_ref.shape, F32)

    def attn_head(h):
        hs = slice(h * XA_HEAD_DIM, (h + 1) * XA_HEAD_DIM)
        s = _nt(qx_ref[:, hs], k_ref[0, :, hs])
        p = jnp.exp2(s - jnp.max(s, axis=-1, keepdims=True))
        inv = 1.0 / jnp.sum(p, axis=-1, keepdims=True)
        oc = _dot(p.astype(BF16), v_ref[0, :, hs]) * (inv * _silu(gx_ref[:, hs].astype(F32)))
        mix_ref[:, SSD_WIDTH + h * XA_HEAD_DIM:SSD_WIDTH + (h + 1) * XA_HEAD_DIM] = oc.astype(BF16)

    ext_ref[pad:pad + ts, :] = xbc_ref[...].astype(F32)
    assert SSD_CONV == 4
    ext = ext_ref[...]
    ext1 = pltpu.roll(ext, 1, 0)
    near = ext * cw_ref[3:4, :] + ext1 * cw_ref[2:3, :]
    far = ext * cw_ref[1:2, :] + ext1 * cw_ref[0:1, :]
    taps = near + pltpu.roll(far, 2, 0)
    xc_ref[...] = _silu(cb_ref[...] + taps[pad:pad + ts, :])
    ext_ref[0:pad, :] = ext_ref[ts:ts + pad, :]

    for h in range(XA_HEADS):
        attn_head(h)
    acc_ref[...] = (x_ref[...] + _dot(oa_ref[...], w_ref[0:w_b, :])
                    + _dot(mix_ref[:, SSD_WIDTH:], w_ref[w_c:, :]))

    a_row = -jnp.exp(al_ref[...]) * LOG2E
    a_col = -jnp.exp(alt_ref[...]) * LOG2E
    causal = lax.broadcasted_iota(I32, (L, L), 1) <= lax.broadcasted_iota(I32, (L, L), 0)
    low = lax.broadcasted_iota(I32, (ts, LANES), 1) < P
    nck = ts // L
    chunk = lambda c: slice(c * L, (c + 1) * L)

    xs = xc_ref[:, 0:SSD_WIDTH]
    dt = jax.nn.softplus(sm_ref[...] + dtb_ref[...])
    da = dt * a_row
    a_cs = jnp.concatenate([_dot2_r(tril_ref[...], da[chunk(c), :]) for c in range(nck)], axis=0)
    dt_x = _dot2_l(dt, eh_ref[...])
    acs_w = _dot2_l(a_cs, ehw_ref[...])
    acs_x = jnp.concatenate(
        [jnp.where(low, acs_w[:, (2 * n) * LANES:(2 * n + 1) * LANES],
                   acs_w[:, (2 * n + 1) * LANES:(2 * n + 2) * LANES]) for n in range(SSD_HEADS // 2)],
        axis=1)
    a_last = [acs_x[c * L + L - 1:c * L + L, :] for c in range(nck)]
    a_last_x = jnp.concatenate([jnp.broadcast_to(a, (L, SSD_WIDTH)) for a in a_last], axis=0)
    dat = jax.nn.softplus(dtt_ref[0, DT_LANE0:DT_LANE0 + SSD_HEADS, :] + dtbt_ref[...]) * a_col
    acs_t = [_dot2_l(dat[:, chunk(c)], triu_ref[...]) for c in range(nck)]
    xdt = xs * dt_x
    xdo = (xdt * jnp.exp2(a_last_x - acs_x)).astype(BF16)
    xdt_b = xdt.astype(BF16)
    y_ref[...] = xs * dsk_ref[...]
    pre = jnp.exp2(acs_x)

    for gi in range(SSD_GROUPS):
        gs = slice(gi * gw, (gi + 1) * gw)
        bm = xc_ref[:, SSD_WIDTH + gi * N:SSD_WIDTH + (gi + 1) * N]
        cm_b = xc_ref[:, SSD_WIDTH + SSD_GROUPS * N + gi * N:SSD_WIDTH + SSD_GROUPS * N + (gi + 1) * N].astype(BF16)
        bm_b = bm.astype(BF16)
        h_c = h_ref[gi]
        h_in = []
        for c in range(nck):
            h_in.append(h_c.astype(BF16))
            h_c = h_c * jnp.exp2(a_last[c][:, gs]) + _dot(bm[chunk(c), :].T.astype(BF16), xdo[chunk(c), gs])
        h_ref[gi] = h_c
        for c in range(nck):
            rs = chunk(c)
            cbm = _nt(cm_b[rs, :], bm_b[rs, :])
            y_ref[rs, gs] += _dot(cm_b[rs, :], h_in[c]) * pre[rs, gs]
            for e in range(E):
                h = gi * E + e
                hs = slice(h * P, (h + 1) * P)
                d = acs_w[rs, h * LANES:(h + 1) * LANES] - acs_t[c][h:h + 1, :]
                dec = jnp.exp2(jnp.where(causal, d, NEG))
                y_ref[rs, hs] += _dot((cbm * dec).astype(BF16), xdt_b[rs, hs])

    y = y_ref[...] * _silu(z_ref[...].astype(F32))
    ms = jnp.mean(y * y, axis=-1, keepdims=True)
    mix_ref[:, 0:SSD_WIDTH] = (y * lax.rsqrt(ms + EPS) * gn_ref[...]).astype(BF16)
    acc = acc_ref[...] + _dot(mix_ref[:, 0:SSD_WIDTH], w_ref[w_b:w_c, :])
    ms = jnp.mean(acc * acc, axis=-1, keepdims=True)
    o_ref[...] = acc * lax.rsqrt(ms + EPS) * g_ref[...]


def _ssd_out(x2, oa, qx, gx, km, vm, w_out, g_final, xbc, z, sm, smt, consts, *, batch, seq):
    rows = batch * seq
    ts = ROW_TILE
    tps = seq // ts
    mlen = km.shape[1]
    row = lambda n: pl.BlockSpec((ts, n), lambda r: (r, 0))
    full = lambda a: pl.BlockSpec(a.shape, lambda r: (0,) * a.ndim)
    mem = pl.BlockSpec((1, mlen, XA_WIDTH), lambda r: (r // tps, 0, 0))
    return pl.pallas_call(
        functools.partial(_ssdout_body, ts=ts, tiles_per_seq=tps),
        grid=(rows // ts,),
        in_specs=[row(D_MODEL), row(NSA_WIDTH), row(XA_WIDTH), row(XA_WIDTH), mem, mem,
                  full(w_out), full(g_final),
                  row(CONV_DIM), row(SSD_WIDTH), row(LANES), pl.BlockSpec((1, LANES, ts), lambda r: (r, 0, 0))]
                 + [full(a) for a in consts],
        out_specs=row(D_MODEL),
        out_shape=jax.ShapeDtypeStruct((rows, D_MODEL), F32),
        scratch_shapes=[pltpu.VMEM((ts, D_MODEL), F32), pltpu.VMEM((ts, SSD_WIDTH + XA_WIDTH), BF16),
                        pltpu.VMEM((ts + 8, CONV_DIM), F32), pltpu.VMEM((ts, CONV_DIM), F32),
                        pltpu.VMEM((ts, SSD_WIDTH), F32),
                        pltpu.VMEM((SSD_GROUPS, SSD_STATE, SSD_WIDTH // SSD_GROUPS), F32)],
        compiler_params=pltpu.CompilerParams(dimension_semantics=("arbitrary",),
                                             vmem_limit_bytes=VMEM_LIMIT),
        name="ssd_out",
    )(x2, oa, qx, gx, km, vm, w_out, g_final, xbc, z, sm, smt, *consts)


def _memkv_body(mem_ref, g_ref, w_ref, k_ref, v_ref):
    x = mem_ref[0]
    ms = jnp.mean(x * x, axis=-1, keepdims=True)
    xn = (x * lax.rsqrt(ms + EPS) * g_ref[...]).astype(BF16)
    k_ref[0] = _dot(xn, w_ref[:, 0:XA_WIDTH]).astype(BF16)
    v_ref[0] = _dot(xn, w_ref[:, XA_WIDTH:2 * XA_WIDTH]).astype(BF16)


def _mem_kv(mem, g_mem, w_kv):
    batch, mlen, _ = mem.shape
    full = lambda a: pl.BlockSpec(a.shape, lambda b: (0,) * a.ndim)
    out = pl.BlockSpec((1, mlen, XA_WIDTH), lambda b: (b, 0, 0))
    shp = jax.ShapeDtypeStruct((batch, mlen, XA_WIDTH), BF16)
    return pl.pallas_call(
        _memkv_body,
        grid=(batch,),
        in_specs=[pl.BlockSpec((1, mlen, D_MODEL), lambda b: (b, 0, 0)), full(g_mem), full(w_kv)],
        out_specs=[out, out],
        out_shape=[shp, shp],
        compiler_params=pltpu.CompilerParams(dimension_semantics=("parallel",),
                                             vmem_limit_bytes=VMEM_LIMIT),
        name="mem_kv",
    )(mem, g_mem, w_kv)


def _permute_w_in(w, layer):
    rb = 128
    return pl.pallas_call(
        _wprep_body,
        grid=(w.shape[1] // rb,),
        in_specs=[pl.BlockSpec((1, rb, w.shape[2]), lambda r: (layer, r, 0))],
        out_specs=pl.BlockSpec((rb, N_PROJ), lambda r: (r, 0)),
        out_shape=jax.ShapeDtypeStruct((w.shape[1], N_PROJ), BF16),
        compiler_params=pltpu.CompilerParams(dimension_semantics=("parallel",),
                                             vmem_limit_bytes=VMEM_LIMIT),
        name="w_prep",
    )(w)


def _wprep_body(w_ref, o_ref):
    sizes = [NSA_WIDTH] + [KV_WIDTH] * 6 + [NSA_HEADS * 3, NSA_WIDTH, SSD_WIDTH, CONV_DIM, SSD_HEADS,
                                          XA_WIDTH, XA_WIDTH]
    offs = [0]
    for s in sizes:
        offs.append(offs[-1] + s)
    w = w_ref[0]

    def move(dst, lo, hi):
        o_ref[:, dst:dst + hi - lo] = w[:, lo:hi].astype(BF16)

    move(OFF_Q, offs[0], offs[7])
    move(OFF_GA, offs[8], offs[11])
    move(OFF_QX, offs[12], offs[14])
    small = jnp.concatenate(
        [w[:, offs[7]:offs[8]], w[:, offs[11]:offs[12]],
         jnp.zeros((w.shape[0], LANES - NSA_HEADS * 3 - SSD_HEADS), w.dtype)], axis=1)
    o_ref[:, OFF_SM:OFF_SM + LANES] = small.astype(BF16)


def _lane_row(vals, lane0):
    return jnp.zeros((1, LANES), F32).at[0, lane0:lane0 + vals.shape[0]].set(vals.astype(F32))


def _forward(x, mem, positions, g_in, w_in, cmp_pos_k, w_cmp1_k, w_cmp2_k, cmp_pos_v, w_cmp1_v,
             w_cmp2_v, conv_w, conv_b, dt_bias, a_log, d_skip, g_ssd_norm, g_mem, w_mem_kv, w_out,
             g_final):
    batch, seq, _ = x.shape
    ncp = seq // CMP_STRIDE
    n_slc = seq // SLC_BLOCK
    assert n_slc <= SLC_SLOTS and seq % ROW_TILE == 0 and ncp <= NSA_TILE
    topk = min(SLC_TOPK, n_slc)
    rows = batch * seq

    inv = ROPE_THETA ** (-jnp.arange(0, ROPE_DIM, 2, dtype=F32) / ROPE_DIM)
    head_inv = jnp.concatenate([inv, inv, jnp.zeros((HEAD_DIM - ROPE_DIM,), F32)])
    invl = jnp.tile(head_inv, LANES // HEAD_DIM)[None, :]
    head_sgn = jnp.concatenate([-jnp.ones((ROPE_HALF,), F32), jnp.ones((ROPE_HALF,), F32),
                                jnp.zeros((HEAD_DIM - ROPE_DIM,), F32)])
    sgn = jnp.tile(head_sgn, LANES // HEAD_DIM)[None, :]
    inv_rows = jnp.concatenate([inv, jnp.zeros((ROPE_FREQ_ROWS - ROPE_HALF,), F32)])[:, None]
    d_ix = jnp.arange(LANES) % HEAD_DIM
    freq_of_lane = jnp.where(d_ix < ROPE_DIM, d_ix % ROPE_HALF, ROPE_HALF)
    spread_cos = (freq_of_lane[:, None] == jnp.arange(ROPE_FREQ_ROWS)[None, :]).astype(F32)
    spread_sin = spread_cos * sgn[0][:, None]

    x2 = x.reshape(rows, D_MODEL)
    pos3 = positions.reshape(rows // ROW_TILE, 1, ROW_TILE)
    h = x2
    for l in range(g_in.shape[0]):
        (q, ck, cv, ksa, vst, kw, vwt, ga, z, xbc, qx, gx, sm, smt) = _in_proj(
            h, pos3, g_in[l][None, :], _permute_w_in(w_in, l), inv_rows,
            spread_cos.astype(BF16), spread_sin.astype(BF16), batch=batch, seq=seq)

        cmp_end = jnp.minimum(jnp.arange(ncp) * CMP_STRIDE + CMP_LEN - 1, seq - 1)
        posc = positions[:, cmp_end][:, :, None]
        pad_w2 = lambda w: jnp.pad(w, ((0, 0), (0, LANES - HEAD_DIM))).astype(BF16)
        pos_rows = lambda p: jnp.broadcast_to(p.reshape(1, CMP_LEN * HEAD_DIM), (8, CMP_LEN * HEAD_DIM)).astype(BF16)
        kc, vc = _compress(ck, cv, posc,
                           w_cmp1_k[l].astype(BF16), pad_w2(w_cmp2_k[l]), pos_rows(cmp_pos_k[l]),
                           w_cmp1_v[l].astype(BF16), pad_w2(w_cmp2_v[l]), pos_rows(cmp_pos_v[l]),
                           invl, sgn, batch=batch, ncp=ncp)
        n_ix = jnp.arange(ncp)[None, :]
        j_ix = jnp.arange(SLC_SLOTS)[:, None]
        ovl = ((n_ix * CMP_STRIDE < j_ix * SLC_BLOCK + SLC_BLOCK)
               & (n_ix * CMP_STRIDE + CMP_LEN > j_ix * SLC_BLOCK)
               & (n_ix < ncp - (CMP_LEN // CMP_STRIDE - 1))).astype(BF16)
        bg = (batch, KV_GROUPS)
        cat = jnp.concatenate(
            [jnp.swapaxes(vc, 2, 3), jnp.ones(bg + (1, ncp), BF16),
             jnp.zeros(bg + (CAT_OVL_ROW0 - HEAD_DIM - 1, ncp), BF16),
             jnp.broadcast_to(ovl, bg + ovl.shape),
             jnp.zeros(bg + (CAT_ROWS - CAT_OVL_ROW0 - SLC_SLOTS, ncp), BF16)], axis=2)
        o_a = _nsa(q, kc, cat, ksa, vst, kw, vwt, ga, sm,
                   batch=batch, seq=seq, ncp=ncp, topk=topk)

        head_of_lane = jnp.arange(SSD_WIDTH) // SSD_HEAD_DIM
        k_ix = jnp.arange(LANES)[:, None]
        eh = (k_ix == DT_LANE0 + head_of_lane[None, :]).astype(BF16)
        ehw = (k_ix == DT_LANE0 + (jnp.arange(SSD_HEADS * LANES) // LANES)[None, :]).astype(BF16)
        t_ix = jnp.arange(SSD_CHUNK)
        tril = (t_ix[None, :] <= t_ix[:, None]).astype(BF16)
        ssd_consts = [conv_w[l], conv_b[l][None, :],
                      _lane_row(dt_bias[l], DT_LANE0), dt_bias[l].astype(F32)[:, None],
                      _lane_row(a_log[l], DT_LANE0), a_log[l].astype(F32)[:, None],
                      jnp.repeat(d_skip[l].astype(F32), SSD_HEAD_DIM)[None, :], g_ssd_norm[l][None, :],
                      eh, ehw, tril, tril.T]

        km, vm = _mem_kv(mem, g_mem[l][None, :], w_mem_kv[l].astype(BF16))
        assert g_in.shape[0] == 1, "the final RMSNorm is fused into the (single) layer's last kernel"
        h = _ssd_out(h, o_a, qx, gx, km, vm, w_out[l].astype(BF16), g_final[None, :],
                     xbc, z, sm, smt, ssd_consts, batch=batch, seq=seq)
    return h.reshape(batch, seq, D_MODEL)


def kernel(x, mem, positions, g_in, w_in, cmp_pos_k, w_cmp1_k, w_cmp2_k, cmp_pos_v, w_cmp1_v, w_cmp2_v,
           conv_w, conv_b, dt_bias, a_log, d_skip, g_ssd_norm, g_mem, w_mem_kv, w_out, g_final):
    return _forward(x, mem, positions, g_in, w_in, cmp_pos_k, w_cmp1_k, w_cmp2_k, cmp_pos_v, w_cmp1_v,
                    w_cmp2_v, conv_w, conv_b, dt_bias, a_log, d_skip, g_ssd_norm, g_mem, w_mem_kv,
                    w_out, g_final)
```

```python
import functools

import jax
import jax.numpy as jnp
from jax import lax
from jax.experimental import pallas as pl
from jax.experimental.pallas import tpu as pltpu

F32 = jnp.float32
BF16 = jnp.bfloat16
I32 = jnp.int32

D_MODEL = 1024
NSA_HEADS = 8
HEAD_DIM = 64
KV_GROUPS = 2
HEADS_PER_GROUP = NSA_HEADS // KV_GROUPS
NSA_WIDTH = NSA_HEADS * HEAD_DIM
KV_WIDTH = KV_GROUPS * HEAD_DIM
CMP_LEN = 32
CMP_STRIDE = 16
CMP_HIDDEN = 256
SLC_BLOCK = 64
SLC_TOPK = 16
SLC_SLOTS = 64
WINDOW = 512
BIG = 1e9
NEG = -1e30

SSD_HEADS = 8
SSD_HEAD_DIM = 64
SSD_WIDTH = SSD_HEADS * SSD_HEAD_DIM
SSD_GROUPS = 2
SSD_STATE = 128
SSD_CONV = 4
SSD_CHUNK = 128
CONV_DIM = SSD_WIDTH + 2 * SSD_GROUPS * SSD_STATE

XA_HEADS = 4
XA_HEAD_DIM = 128
XA_WIDTH = XA_HEADS * XA_HEAD_DIM
MIX_WIDTH = NSA_WIDTH + SSD_WIDTH + XA_WIDTH

ROPE_THETA = 500000.0
ROPE_DIM = HEAD_DIM // 4
ROPE_HALF = ROPE_DIM // 2
ROPE_FREQ_ROWS = 16
EPS = 1e-6

LANES = 128
GATE_LANE0 = 0
DT_LANE0 = 24

OFF_Q = 0
OFF_KV = OFF_Q + NSA_WIDTH
OFF_GA = OFF_KV + 6 * KV_WIDTH
OFF_Z = OFF_GA + NSA_WIDTH
OFF_XBC = OFF_Z + SSD_WIDTH
OFF_QX = OFF_XBC + CONV_DIM
OFF_GX = OFF_QX + XA_WIDTH
OFF_SM = OFF_GX + XA_WIDTH
N_PROJ = OFF_SM + LANES

ROW_TILE = 512
NSA_TILE = 256
V_ROWS = 80
LOG2E = 1.4426950408889634
CAT_OVL_ROW0 = 80
CAT_ROWS = 144
VMEM_LIMIT = 56 * 1024 * 1024


def _nt(a, b):
    return lax.dot_general(a, b, (((1,), (1,)), ((), ())), preferred_element_type=F32)


def _dot(a, b):
    return jnp.dot(a, b, preferred_element_type=F32)


def _split2(x):
    hi = x.astype(BF16)
    return hi, (x - hi.astype(F32)).astype(BF16)


def _dot2_l(x, w):
    hi, lo = _split2(x)
    return _dot(hi, w) + _dot(lo, w)


def _dot2_r(w, x):
    hi, lo = _split2(x)
    return _dot(w, hi) + _dot(w, lo)


def _silu(x):
    h = 0.5 * x
    return h + h * jnp.tanh(h)


def _rope(a, cs, sn, first):
    r = jnp.where(first, pltpu.roll(a, LANES - ROPE_HALF, 1), pltpu.roll(a, ROPE_HALF, 1))
    return a * cs + r * sn


def _rope_tables(pos_f32, invl, sgn):
    ang = pos_f32 * invl
    return jnp.cos(ang), jnp.sin(ang) * sgn


def _inproj_body(x_ref, pos_ref, g_ref, w_ref, invc_ref, ec_ref, es_ref,
                 q_ref, ck_ref, cv_ref, ksa_ref, vst_ref, kw_ref, vwt_ref,
                 ga_ref, z_ref, xbc_ref, qx_ref, gx_ref, sm_ref, smt_ref, sk_ref, sv_ref, *, tm, tiles_per_seq):
    x = x_ref[...]
    ms = jnp.mean(x * x, axis=-1, keepdims=True)
    xn = (x * lax.rsqrt(ms + EPS) * g_ref[...]).astype(BF16)

    def mm(lo, n):
        return _dot(xn, w_ref[:, lo:lo + n])

    ga_ref[...] = mm(OFF_GA, NSA_WIDTH)
    z_ref[...] = mm(OFF_Z, SSD_WIDTH)

    ang = invc_ref[...] * pos_ref[0].astype(F32)
    cs = _dot2_r(ec_ref[...], jnp.cos(ang)).T
    sn = _dot2_r(es_ref[...], jnp.sin(ang)).T
    lane = lax.broadcasted_iota(I32, (tm, LANES), 1)
    first = (lane % HEAD_DIM) < ROPE_HALF

    qf = mm(OFF_Q, NSA_WIDTH)
    for c in range(NSA_WIDTH // LANES):
        sl = slice(c * LANES, (c + 1) * LANES)
        q_ref[:, sl] = (_rope(qf[:, sl], cs, sn, first) * (HEAD_DIM ** -0.5 * LOG2E)).astype(BF16)

    kv = mm(OFF_KV, 6 * KV_WIDTH)
    k_c, v_c, k_s, v_s, k_w, v_w = [kv[:, n * KV_WIDTH:(n + 1) * KV_WIDTH] for n in range(6)]
    k_s = _rope(k_s, cs, sn, first)
    k_w = _rope(k_w, cs, sn, first)
    s_base = (pl.program_id(0) % tiles_per_seq) * tm
    blk = (s_base + lax.broadcasted_iota(I32, (tm, LANES), 0)) // SLC_BLOCK
    onehot = jnp.where(lane - HEAD_DIM == blk, 1.0, 0.0)
    low = lane < HEAD_DIM
    vs_t = v_s.T
    vw_t = v_w.T
    tail = jnp.where(lax.broadcasted_iota(I32, (V_ROWS - HEAD_DIM, NSA_TILE), 0) == 0, 1.0, 0.0).astype(BF16)
    for g in range(KV_GROUPS):
        gs = slice(g * HEAD_DIM, (g + 1) * HEAD_DIM)
        kw_ref[0, g] = k_w[:, gs].astype(BF16)
        ks_g = k_s if g == 0 else pltpu.roll(k_s, HEAD_DIM, 1)
        ksa_ref[0, g] = jnp.where(low, ks_g, onehot).astype(BF16)
        for c in range(tm // NSA_TILE):
            cols = slice(c * NSA_TILE, (c + 1) * NSA_TILE)
            vst_ref[0, g, c, 0:HEAD_DIM, :] = vs_t[gs, cols].astype(BF16)
            vst_ref[0, g, c, HEAD_DIM:V_ROWS, :] = tail
            vwt_ref[0, g, c, 0:HEAD_DIM, :] = vw_t[gs, cols].astype(BF16)
            vwt_ref[0, g, c, HEAD_DIM:V_ROWS, :] = tail

    sk_ref[...] = k_c
    sv_ref[...] = v_c
    nchunk = tm // CMP_STRIDE
    for t in range(CMP_STRIDE):
        dst = slice(t * HEAD_DIM, (t + 1) * HEAD_DIM)
        kt = sk_ref[pl.ds(t, nchunk, stride=CMP_STRIDE), :].astype(BF16)
        vt = sv_ref[pl.ds(t, nchunk, stride=CMP_STRIDE), :].astype(BF16)
        for g in range(KV_GROUPS):
            gs = slice(g * HEAD_DIM, (g + 1) * HEAD_DIM)
            ck_ref[0, g, :, dst] = kt[:, gs]
            cv_ref[0, g, :, dst] = vt[:, gs]

    xbc_ref[...] = mm(OFF_XBC, CONV_DIM)
    qx_ref[...] = (mm(OFF_QX, XA_WIDTH) * (XA_HEAD_DIM ** -0.5 * LOG2E)).astype(BF16)
    gx_ref[...] = mm(OFF_GX, XA_WIDTH)
    sm = mm(OFF_SM, LANES)
    sm_ref[...] = sm
    smt_ref[0] = sm.T


def _in_proj(x2, pos3, g_in, w_p, invc, ec, es, *, batch, seq):
    rows = batch * seq
    tm = ROW_TILE
    tps = seq // tm
    row = lambda n: pl.BlockSpec((tm, n), lambda r: (r, 0))
    full = lambda a: pl.BlockSpec(a.shape, lambda r: (0,) * a.ndim)
    grp = lambda n: pl.BlockSpec((1, KV_GROUPS, tm, n), lambda r: (r // tps, 0, r % tps, 0))
    grp_shape = lambda n: jax.ShapeDtypeStruct((batch, KV_GROUPS, seq, n), BF16)
    chunk = pl.BlockSpec((1, KV_GROUPS, tm // CMP_STRIDE, CMP_STRIDE * HEAD_DIM),
                         lambda r: (r // tps, 0, r % tps, 0))
    chunk_shape = jax.ShapeDtypeStruct((batch, KV_GROUPS, seq // CMP_STRIDE, CMP_STRIDE * HEAD_DIM), BF16)
    vtile = pl.BlockSpec((1, KV_GROUPS, tm // NSA_TILE, V_ROWS, NSA_TILE), lambda r: (r // tps, 0, r % tps, 0, 0))
    vtile_shape = jax.ShapeDtypeStruct((batch, KV_GROUPS, seq // NSA_TILE, V_ROWS, NSA_TILE), BF16)
    flat = lambda n, dt: jax.ShapeDtypeStruct((rows, n), dt)
    return pl.pallas_call(
        functools.partial(_inproj_body, tm=tm, tiles_per_seq=tps),
        grid=(rows // tm,),
        in_specs=[row(D_MODEL), pl.BlockSpec((1, 1, tm), lambda r: (r, 0, 0)),
                  full(g_in), full(w_p), full(invc), full(ec), full(es)],
        out_specs=[row(NSA_WIDTH), chunk, chunk, grp(LANES), vtile, grp(HEAD_DIM), vtile,
                   row(NSA_WIDTH), row(SSD_WIDTH), row(CONV_DIM),
                   row(XA_WIDTH), row(XA_WIDTH), row(LANES),
                   pl.BlockSpec((1, LANES, tm), lambda r: (r, 0, 0))],
        out_shape=[flat(NSA_WIDTH, BF16), chunk_shape, chunk_shape, grp_shape(LANES), vtile_shape,
                   grp_shape(HEAD_DIM), vtile_shape,
                   flat(NSA_WIDTH, F32), flat(SSD_WIDTH, F32), flat(CONV_DIM, F32),
                   flat(XA_WIDTH, BF16), flat(XA_WIDTH, F32), flat(LANES, F32),
                   jax.ShapeDtypeStruct((rows // tm, LANES, tm), F32)],
        scratch_shapes=[pltpu.VMEM((tm, KV_WIDTH), F32), pltpu.VMEM((tm, KV_WIDTH), F32)],
        compiler_params=pltpu.CompilerParams(dimension_semantics=("parallel",),
                                             vmem_limit_bytes=VMEM_LIMIT),
        name="in_proj",
    )(x2, pos3, g_in, w_p, invc, ec, es)


def _compress_body(ck_ref, cv_ref, posc_ref, w1k_ref, w2k_ref, pk_ref, w1v_ref, w2v_ref, pv_ref,
                   invl_ref, sgn_ref, kc_ref, vc_ref, *, ncp):
    half = CMP_STRIDE * HEAD_DIM

    def mlp(c_ref, w1_ref, w2_ref, p_ref):
        c = c_ref[0, 0]
        a = _dot(c, w1_ref[0:half, :])
        b = _dot(c, w1_ref[half:2 * half, :])
        bias = _dot(p_ref[...], w1_ref[...])[0:1, :]
        h = a + pltpu.roll(b, ncp - 1, 0) + bias
        return _dot(_silu(h).astype(BF16), w2_ref[...])

    kc = mlp(ck_ref, w1k_ref, w2k_ref, pk_ref)
    vc = mlp(cv_ref, w1v_ref, w2v_ref, pv_ref)
    cs, sn = _rope_tables(posc_ref[0].astype(F32), invl_ref[...], sgn_ref[...])
    lane = lax.broadcasted_iota(I32, (ncp, LANES), 1)
    kc = _rope(kc, cs, sn, (lane % HEAD_DIM) < ROPE_HALF)
    kc_ref[0, 0] = kc[:, 0:HEAD_DIM].astype(BF16)
    vc_ref[0, 0] = vc[:, 0:HEAD_DIM].astype(BF16)


def _compress(ck, cv, posc, w1k, w2k, pk, w1v, w2v, pv, invl, sgn, *, batch, ncp):
    chunk = pl.BlockSpec((1, 1, ncp, CMP_STRIDE * HEAD_DIM), lambda b, g: (b, g, 0, 0))
    full = lambda a: pl.BlockSpec(a.shape, lambda b, g: (0,) * a.ndim)
    out = pl.BlockSpec((1, 1, ncp, HEAD_DIM), lambda b, g: (b, g, 0, 0))
    shp = jax.ShapeDtypeStruct((batch, KV_GROUPS, ncp, HEAD_DIM), BF16)
    return pl.pallas_call(
        functools.partial(_compress_body, ncp=ncp),
        grid=(batch, KV_GROUPS),
        in_specs=[chunk, chunk, pl.BlockSpec((1, ncp, 1), lambda b, g: (b, 0, 0)),
                  full(w1k), full(w2k), full(pk), full(w1v), full(w2v), full(pv),
                  full(invl), full(sgn)],
        out_specs=[out, out],
        out_shape=[shp, shp],
        compiler_params=pltpu.CompilerParams(dimension_semantics=("parallel", "parallel"),
                                             vmem_limit_bytes=VMEM_LIMIT),
        name="compress",
    )(ck, cv, posc, w1k, w2k, pk, w1v, w2v, pv, invl, sgn)


def _nsa_body(q_ref, kc_ref, cat_ref, ksa_ref, vst_ref, kw_ref, vwt_ref, ga_ref, sm_ref,
              o_ref, qa_ref, m_ref, acc_ref, oc_ref, sc_ref, mc_ref, gt_ref, gs_ref, *, tq, ncp, topk):
    g = pl.program_id(1)
    i = pl.program_id(2)
    s0 = i * tq
    nh = HEADS_PER_GROUP
    rows = nh * tq

    def col_max(s):
        n = s.shape[0]
        while n % 16 == 0:
            n //= 2
            s = jnp.maximum(s[0:n], s[n:2 * n])
        return jnp.max(s, axis=0, keepdims=True)

    def reset(br):
        m_ref[br] = jnp.full((1, rows), NEG, F32)
        acc_ref[br] = jnp.zeros((V_ROWS, rows), F32)

    def consume(br, s, vt_tile, mask, s_max=None):
        if mask is not None:
            s = jnp.where(mask, s, NEG)
        if s_max is None:
            s_max = col_max(s)
        m_prev = m_ref[br]
        m_new = jnp.maximum(m_prev, s_max)
        alpha = jnp.exp2(m_prev - m_new)
        p = jnp.exp2(s - m_new).astype(BF16)
        acc_ref[br] = alpha * acc_ref[br] + _dot(vt_tile, p)
        m_ref[br] = m_new

    def sel_scores(j):
        rows_j = pl.ds(pl.multiple_of(j * tq, tq), tq)
        return _nt(ksa_ref[0, 0, rows_j, :], qa_ref[...])

    def win_scores(j):
        rows_j = pl.ds(pl.multiple_of(j * tq, tq), tq)
        return _nt(kw_ref[0, 0, rows_j, :], qa_ref[:, 0:HEAD_DIM])

    q4 = q_ref[...]
    for e in range(nh):
        qa_ref[e * tq:(e + 1) * tq, 0:HEAD_DIM] = q4[:, e * HEAD_DIM:(e + 1) * HEAD_DIM]

    k_io = lax.broadcasted_iota(I32, (tq, rows), 0)
    t_io2 = lax.broadcasted_iota(I32, (tq, rows), 1) % tq
    diag = k_io <= t_io2
    cmp_per_blk = SLC_BLOCK // CMP_STRIDE

    def produce(slot, j):
        s = sel_scores(j)
        sc_ref[slot] = s
        mc_ref[slot] = col_max(s)

    def first_tile_and_gates():
        reset(0)
        produce(0, 0)
        gates_t = jax.nn.sigmoid(sm_ref[...]).T
        gs_ref[...] = _silu(ga_ref[...])

        def gate(e, c):
            c0 = GATE_LANE0 + e * 3 + c
            c1 = c0 + nh * 3
            return jnp.where(g == 0, gates_t[c0:c0 + 1, :], gates_t[c1:c1 + 1, :])

        for e in range(nh):
            cs = slice(e * tq, (e + 1) * tq)
            o_w = acc_ref[1, 0:HEAD_DIM, cs] / acc_ref[1, HEAD_DIM:HEAD_DIM + 1, cs]
            oc_ref[:, cs] = gate(e, 0) * oc_ref[:, cs] + gate(e, 2) * o_w
            gt_ref[e:e + 1, :] = gate(e, 1)

    def prologue(cap, nwin):
        nc = min(ncp, cap * cmp_per_blk)
        n_io = lax.broadcasted_iota(I32, (nc, rows), 0)
        t_io = s0 + lax.broadcasted_iota(I32, (nc, rows), 1) % tq
        s = jnp.where((n_io * CMP_STRIDE + (CMP_LEN - 1)) <= t_io,
                      _nt(kc_ref[0, 0, 0:nc, :], qa_ref[:, 0:HEAD_DIM]), NEG)
        sc_ref[0, 0:nc, :] = s
        mc_ref[0] = col_max(s)
        sc_ref[1] = win_scores(i)
        p = jnp.exp2(sc_ref[0, 0:nc, :] - mc_ref[0]).astype(BF16)
        big = _dot(cat_ref[0, 0, :, 0:nc], p)
        if nwin >= 2:
            sc_ref[0] = win_scores(i - 1)
        t_row = s0 + lax.broadcasted_iota(I32, (1, rows), 1) % tq
        inv = jnp.where(t_row >= CMP_LEN - 1, 1.0 / jnp.maximum(big[HEAD_DIM:HEAD_DIM + 1, :], 1e-30), 0.0)
        oc_ref[...] = big[0:HEAD_DIM, :] * inv
        imp_x = big[CAT_OVL_ROW0:CAT_OVL_ROW0 + cap, :] * inv
        imp = imp_x[:, 0:tq]
        for e in range(1, nh):
            imp = imp + imp_x[:, e * tq:(e + 1) * tq]

        j_io = lax.broadcasted_iota(I32, (cap, tq), 0)
        cur = (s0 + lax.broadcasted_iota(I32, (cap, tq), 1)) // SLC_BLOCK
        valid = j_io <= cur
        if cap > topk:
            TAKEN = -3e38
            TAKEN_BELOW = -1e37
            forced = (j_io == 0) | (j_io == cur) | (j_io == cur - 1)
            j_f = j_io.astype(F32)
            v = jnp.where(forced, TAKEN, jnp.where(valid, imp, -BIG))
            for _ in range(topk - 3):
                mx = jnp.max(v, axis=0, keepdims=True)
                idx = jnp.min(jnp.where(v == mx, j_f, float(SLC_SLOTS)), axis=0, keepdims=True)
                v = jnp.where(j_f == idx, TAKEN, v)
            valid = valid & (v < TAKEN_BELOW)
        selb = jnp.where(valid, 0.0, NEG)
        fill = [jnp.full((SLC_SLOTS - cap, tq), NEG, F32)] if cap < SLC_SLOTS else []
        selb_t = jnp.concatenate([jnp.zeros((SLC_SLOTS, tq), F32), selb] + fill, axis=0).T
        selb_t = selb_t.astype(BF16)

        reset(1)
        consume(1, sc_ref[1], vwt_ref[0, 0, i], diag)
        if nwin >= 3:
            sc_ref[1] = win_scores(i - 2)
        if nwin >= 2:
            consume(1, sc_ref[0], vwt_ref[0, 0, i - 1], None)
        if nwin >= 3:
            consume(1, sc_ref[1], vwt_ref[0, 0, i - 2], k_io > t_io2)
        for e in range(nh):
            qa_ref[e * tq:(e + 1) * tq, HEAD_DIM:LANES] = selb_t[:, HEAD_DIM:LANES]
        first_tile_and_gates()

    blk_per_tile = tq // SLC_BLOCK
    n_tiles = SLC_SLOTS // blk_per_tile
    caps = sorted({min(topk, SLC_SLOTS), SLC_SLOTS // 2, 3 * SLC_SLOTS // 4, SLC_SLOTS})
    cap_of = lambda t: next(c for c in caps if (t + 1) * blk_per_tile <= c)
    starts = sorted({0, 1, 2} | {c // blk_per_tile for c in caps[:-1]})
    for n, first in enumerate(starts):
        last = starts[n + 1] - 1 if n + 1 < len(starts) else n_tiles - 1
        pl.when((i >= first) & (i <= last))(functools.partial(prologue, cap_of(last), min(first, 2) + 1))

    def sel_pair(j):
        produce(1, j + 1)
        consume(0, sc_ref[0], vst_ref[0, 0, j], None, mc_ref[0])
        produce(0, j + 2)
        consume(0, sc_ref[1], vst_ref[0, 0, j + 1], None, mc_ref[1])

    done = 0
    for width in (8, 4, 2):
        start = done

        def block(jj, carry, width=width, start=start):
            for u in range(0, width, 2):
                sel_pair(start + width * jj + u)
            return carry

        trips = (i - start) // width
        lax.fori_loop(0, trips, block, 0)
        done = start + trips * width

    @pl.when(i % 2 == 0)
    def _():
        consume(0, sc_ref[0], vst_ref[0, 0, i], diag)

    @pl.when(i % 2 == 1)
    def _():
        sc_ref[1] = sel_scores(i)
        consume(0, sc_ref[0], vst_ref[0, 0, i - 1], None, mc_ref[0])
        consume(0, sc_ref[1], vst_ref[0, 0, i], diag)

    def head_out(e):
        cs = slice(e * tq, (e + 1) * tq)
        o_s = acc_ref[0, 0:HEAD_DIM, cs] / acc_ref[0, HEAD_DIM:HEAD_DIM + 1, cs]
        return oc_ref[:, cs] + gt_ref[e:e + 1, :] * o_s

    for pr in range(nh // 2):
        ls = slice(pr * LANES, (pr + 1) * LANES)
        pair = jnp.concatenate([head_out(2 * pr), head_out(2 * pr + 1)], axis=0).T
        o_ref[:, ls] = (pair * gs_ref[:, ls]).astype(BF16)


def _nsa(q, kc, cat, ksa, vst, kw, vwt, ga, sm, *, batch, seq, ncp, topk):
    tq = NSA_TILE
    assert WINDOW == 2 * tq
    nq = seq // tq
    gw = HEADS_PER_GROUP * HEAD_DIM
    rows = HEADS_PER_GROUP * tq
    per_bg = lambda *shape: pl.BlockSpec((1, 1) + shape, lambda b, g, i: (b, g) + (0,) * len(shape))
    return pl.pallas_call(
        functools.partial(_nsa_body, tq=tq, ncp=ncp, topk=topk),
        grid=(batch, KV_GROUPS, nq),
        in_specs=[pl.BlockSpec((tq, gw), lambda b, g, i: (b * nq + i, g)),
                  per_bg(ncp, HEAD_DIM), per_bg(CAT_ROWS, ncp),
                  per_bg(seq, LANES), per_bg(nq, V_ROWS, tq), per_bg(seq, HEAD_DIM),
                  per_bg(nq, V_ROWS, tq),
                  pl.BlockSpec((tq, gw), lambda b, g, i: (b * nq + i, g)),
                  pl.BlockSpec((tq, LANES), lambda b, g, i: (b * nq + i, 0))],
        out_specs=pl.BlockSpec((tq, gw), lambda b, g, i: (b * nq + i, g)),
        out_shape=jax.ShapeDtypeStruct((batch * seq, NSA_WIDTH), BF16),
        scratch_shapes=[pltpu.VMEM((rows, LANES), BF16),
                        pltpu.VMEM((2, 1, rows), F32),
                        pltpu.VMEM((2, V_ROWS, rows), F32),
                        pltpu.VMEM((HEAD_DIM, rows), F32),
                        pltpu.VMEM((2, tq, rows), F32), pltpu.VMEM((2, 1, rows), F32),
                        pltpu.VMEM((2 * HEADS_PER_GROUP, tq), F32), pltpu.VMEM((tq, gw), F32)],
        compiler_params=pltpu.CompilerParams(
            dimension_semantics=("parallel", "parallel", "arbitrary"),
            vmem_limit_bytes=VMEM_LIMIT),
        name="nsa",
    )(q, kc, cat, ksa, vst, kw, vwt, ga, sm)


def _ssdout_body(x_ref, oa_ref, qx_ref, gx_ref, k_ref, v_ref, w_ref, g_ref,
                 xbc_ref, z_ref, sm_ref, dtt_ref, cw_ref, cb_ref, dtb_ref, dtbt_ref, al_ref, alt_ref,
                 dsk_ref, gn_ref, eh_ref, ehw_ref, tril_ref, triu_ref,
                 o_ref, acc_ref, mix_ref, ext_ref, xc_ref, y_ref, h_ref, *, ts, tiles_per_seq):
    L = SSD_CHUNK
    N = SSD_STATE
    P = SSD_HEAD_DIM
    E = SSD_HEADS // SSD_GROUPS
    gw = E * P
    pad = 8
    w_b = NSA_WIDTH
    w_c = NSA_WIDTH + SSD_WIDTH

    @pl.when(pl.program_id(0) % tiles_per_seq == 0)
    def _():
        ext_ref[0:pad, :] = jnp.zeros((pad, CONV_DIM), F32)
        h_ref[...] = jnp.zeros(h_ref.shape, F32)

    def attn_head(h):
        hs = slice(h * XA_HEAD_DIM, (h + 1) * XA_HEAD_DIM)
        s = _nt(qx_ref[:, hs], k_ref[0, :, hs])
        p = jnp.exp2(s - jnp.max(s, axis=-1, keepdims=True))
        inv = 1.0 / jnp.sum(p, axis=-1, keepdims=True)
        oc = _dot(p.astype(BF16), v_ref[0, :, hs]) * (inv * _silu(gx_ref[:, hs]))
        mix_ref[:, SSD_WIDTH + h * XA_HEAD_DIM:SSD_WIDTH + (h + 1) * XA_HEAD_DIM] = oc.astype(BF16)

    ext_ref[pad:pad + ts, :] = xbc_ref[...]
    assert SSD_CONV == 4
    ext = ext_ref[...]
    ext1 = pltpu.roll(ext, 1, 0)
    near = ext * cw_ref[3:4, :] + ext1 * cw_ref[2:3, :]
    far = ext * cw_ref[1:2, :] + ext1 * cw_ref[0:1, :]
    taps = near + pltpu.roll(far, 2, 0)
    xc_ref[...] = _silu(cb_ref[...] + taps[pad:pad + ts, :])
    ext_ref[0:pad, :] = ext_ref[ts:ts + pad, :]

    for h in range(XA_HEADS):
        attn_head(h)
    acc_ref[...] = (x_ref[...] + _dot(oa_ref[...], w_ref[0:w_b, :])
                    + _dot(mix_ref[:, SSD_WIDTH:], w_ref[w_c:, :]))

    a_row = -jnp.exp(al_ref[...]) * LOG2E
    a_col = -jnp.exp(alt_ref[...]) * LOG2E
    causal = lax.broadcasted_iota(I32, (L, L), 1) <= lax.broadcasted_iota(I32, (L, L), 0)
    low = lax.broadcasted_iota(I32, (ts, LANES), 1) < P
    nck = ts // L
    chunk = lambda c: slice(c * L, (c + 1) * L)

    xs = xc_ref[:, 0:SSD_WIDTH]
    dt = jax.nn.softplus(sm_ref[...] + dtb_ref[...])
    da = dt * a_row
    a_cs = jnp.concatenate([_dot2_r(tril_ref[...], da[chunk(c), :]) for c in range(nck)], axis=0)
    dt_x = _dot2_l(dt, eh_ref[...])
    acs_w = _dot2_l(a_cs, ehw_ref[...])
    acs_x = jnp.concatenate(
        [jnp.where(low, acs_w[:, (2 * n) * LANES:(2 * n + 1) * LANES],
                   acs_w[:, (2 * n + 1) * LANES:(2 * n + 2) * LANES]) for n in range(SSD_HEADS // 2)],
        axis=1)
    a_last = [acs_x[c * L + L - 1:c * L + L, :] for c in range(nck)]
    a_last_x = jnp.concatenate([jnp.broadcast_to(a, (L, SSD_WIDTH)) for a in a_last], axis=0)
    dat = jax.nn.softplus(dtt_ref[0, DT_LANE0:DT_LANE0 + SSD_HEADS, :] + dtbt_ref[...]) * a_col
    acs_t = [_dot2_l(dat[:, chunk(c)], triu_ref[...]) for c in range(nck)]
    xdt = xs * dt_x
    xdo = (xdt * jnp.exp2(a_last_x - acs_x)).astype(BF16)
    xdt_b = xdt.astype(BF16)
    y_ref[...] = xs * dsk_ref[...]
    pre = jnp.exp2(acs_x)

    for gi in range(SSD_GROUPS):
        gs = slice(gi * gw, (gi + 1) * gw)
        bm = xc_ref[:, SSD_WIDTH + gi * N:SSD_WIDTH + (gi + 1) * N]
        cm_b = xc_ref[:, SSD_WIDTH + SSD_GROUPS * N + gi * N:SSD_WIDTH + SSD_GROUPS * N + (gi + 1) * N].astype(BF16)
        bm_b = bm.astype(BF16)
        h_c = h_ref[gi]
        h_in = []
        for c in range(nck):
            h_in.append(h_c.astype(BF16))
            h_c = h_c * jnp.exp2(a_last[c][:, gs]) + _dot(bm[chunk(c), :].T.astype(BF16), xdo[chunk(c), gs])
        h_ref[gi] = h_c
        for c in range(nck):
            rs = chunk(c)
            cbm = _nt(cm_b[rs, :], bm_b[rs, :])
            y_ref[rs, gs] += _dot(cm_b[rs, :], h_in[c]) * pre[rs, gs]
            for e in range(E):
                h = gi * E + e
                hs = slice(h * P, (h + 1) * P)
                d = acs_w[rs, h * LANES:(h + 1) * LANES] - acs_t[c][h:h + 1, :]
                dec = jnp.exp2(jnp.where(causal, d, NEG))
                y_ref[rs, hs] += _dot((cbm * dec).astype(BF16), xdt_b[rs, hs])

    y = y_ref[...] * _silu(z_ref[...])
    ms = jnp.mean(y * y, axis=-1, keepdims=True)
    mix_ref[:, 0:SSD_WIDTH] = (y * lax.rsqrt(ms + EPS) * gn_ref[...]).astype(BF16)
    acc = acc_ref[...] + _dot(mix_ref[:, 0:SSD_WIDTH], w_ref[w_b:w_c, :])
    ms = jnp.mean(acc * acc, axis=-1, keepdims=True)
    o_ref[...] = acc * lax.rsqrt(ms + EPS) * g_ref[...]


def _ssd_out(x2, oa, qx, gx, km, vm, w_out, g_final, xbc, z, sm, smt, consts, *, batch, seq):
    rows = batch * seq
    ts = ROW_TILE
    tps = seq // ts
    mlen = km.shape[1]
    row = lambda n: pl.BlockSpec((ts, n), lambda r: (r, 0))
    full = lambda a: pl.BlockSpec(a.shape, lambda r: (0,) * a.ndim)
    mem = pl.BlockSpec((1, mlen, XA_WIDTH), lambda r: (r // tps, 0, 0))
    return pl.pallas_call(
        functools.partial(_ssdout_body, ts=ts, tiles_per_seq=tps),
        grid=(rows // ts,),
        in_specs=[row(D_MODEL), row(NSA_WIDTH), row(XA_WIDTH), row(XA_WIDTH), mem, mem,
                  full(w_out), full(g_final),
                  row(CONV_DIM), row(SSD_WIDTH), row(LANES), pl.BlockSpec((1, LANES, ts), lambda r: (r, 0, 0))]
                 + [full(a) for a in consts],
        out_specs=row(D_MODEL),
        out_shape=jax.ShapeDtypeStruct((rows, D_MODEL), F32),
        scratch_shapes=[pltpu.VMEM((ts, D_MODEL), F32), pltpu.VMEM((ts, SSD_WIDTH + XA_WIDTH), BF16),
                        pltpu.VMEM((ts + 8, CONV_DIM), F32), pltpu.VMEM((ts, CONV_DIM), F32),
                        pltpu.VMEM((ts, SSD_WIDTH), F32),
                        pltpu.VMEM((SSD_GROUPS, SSD_STATE, SSD_WIDTH // SSD_GROUPS), F32)],
        compiler_params=pltpu.CompilerParams(dimension_semantics=("arbitrary",),
                                             vmem_limit_bytes=VMEM_LIMIT),
        name="ssd_out",
    )(x2, oa, qx, gx, km, vm, w_out, g_final, xbc, z, sm, smt, *consts)


def _memkv_body(mem_ref, g_ref, w_ref, k_ref, v_ref):
    x = mem_ref[0]
    ms = jnp.mean(x * x, axis=-1, keepdims=True)
    xn = (x * lax.rsqrt(ms + EPS) * g_ref[...]).astype(BF16)
    k_ref[0] = _dot(xn, w_ref[:, 0:XA_WIDTH]).astype(BF16)
    v_ref[0] = _dot(xn, w_ref[:, XA_WIDTH:2 * XA_WIDTH]).astype(BF16)


def _mem_kv(mem, g_mem, w_kv):
    batch, mlen, _ = mem.shape
    full = lambda a: pl.BlockSpec(a.shape, lambda b: (0,) * a.ndim)
    out = pl.BlockSpec((1, mlen, XA_WIDTH), lambda b: (b, 0, 0))
    shp = jax.ShapeDtypeStruct((batch, mlen, XA_WIDTH), BF16)
    return pl.pallas_call(
        _memkv_body,
        grid=(batch,),
        in_specs=[pl.BlockSpec((1, mlen, D_MODEL), lambda b: (b, 0, 0)), full(g_mem), full(w_kv)],
        out_specs=[out, out],
        out_shape=[shp, shp],
        compiler_params=pltpu.CompilerParams(dimension_semantics=("parallel",),
                                             vmem_limit_bytes=VMEM_LIMIT),
        name="mem_kv",
    )(mem, g_mem, w_kv)


def _permute_w_in(w, layer):
    rb = 128
    return pl.pallas_call(
        _wprep_body,
        grid=(w.shape[1] // rb,),
        in_specs=[pl.BlockSpec((1, rb, w.shape[2]), lambda r: (layer, r, 0))],
        out_specs=pl.BlockSpec((rb, N_PROJ), lambda r: (r, 0)),
        out_shape=jax.ShapeDtypeStruct((w.shape[1], N_PROJ), BF16),
        compiler_params=pltpu.CompilerParams(dimension_semantics=("parallel",),
                                             vmem_limit_bytes=VMEM_LIMIT),
        name="w_prep",
    )(w)


def _wprep_body(w_ref, o_ref):
    sizes = [NSA_WIDTH] + [KV_WIDTH] * 6 + [NSA_HEADS * 3, NSA_WIDTH, SSD_WIDTH, CONV_DIM, SSD_HEADS,
                                          XA_WIDTH, XA_WIDTH]
    offs = [0]
    for s in sizes:
        offs.append(offs[-1] + s)
    w = w_ref[0]

    def move(dst, lo, hi):
        o_ref[:, dst:dst + hi - lo] = w[:, lo:hi].astype(BF16)

    move(OFF_Q, offs[0], offs[7])
    move(OFF_GA, offs[8], offs[11])
    move(OFF_QX, offs[12], offs[14])
    small = jnp.concatenate(
        [w[:, offs[7]:offs[8]], w[:, offs[11]:offs[12]],
         jnp.zeros((w.shape[0], LANES - NSA_HEADS * 3 - SSD_HEADS), w.dtype)], axis=1)
    o_ref[:, OFF_SM:OFF_SM + LANES] = small.astype(BF16)


def _lane_row(vals, lane0):
    return jnp.zeros((1, LANES), F32).at[0, lane0:lane0 + vals.shape[0]].set(vals.astype(F32))


def _forward(x, mem, positions, g_in, w_in, cmp_pos_k, w_cmp1_k, w_cmp2_k, cmp_pos_v, w_cmp1_v,
             w_cmp2_v, conv_w, conv_b, dt_bias, a_log, d_skip, g_ssd_norm, g_mem, w_mem_kv, w_out,
             g_final):
    batch, seq, _ = x.shape
    ncp = seq // CMP_STRIDE
    n_slc = seq // SLC_BLOCK
    assert n_slc <= SLC_SLOTS and seq % ROW_TILE == 0 and ncp <= NSA_TILE
    topk = min(SLC_TOPK, n_slc)
    rows = batch * seq

    inv = ROPE_THETA ** (-jnp.arange(0, ROPE_DIM, 2, dtype=F32) / ROPE_DIM)
    head_inv = jnp.concatenate([inv, inv, jnp.zeros((HEAD_DIM - ROPE_DIM,), F32)])
    invl = jnp.tile(head_inv, LANES // HEAD_DIM)[None, :]
    head_sgn = jnp.concatenate([-jnp.ones((ROPE_HALF,), F32), jnp.ones((ROPE_HALF,), F32),
                                jnp.zeros((HEAD_DIM - ROPE_DIM,), F32)])
    sgn = jnp.tile(head_sgn, LANES // HEAD_DIM)[None, :]
    inv_rows = jnp.concatenate([inv, jnp.zeros((ROPE_FREQ_ROWS - ROPE_HALF,), F32)])[:, None]
    d_ix = jnp.arange(LANES) % HEAD_DIM
    freq_of_lane = jnp.where(d_ix < ROPE_DIM, d_ix % ROPE_HALF, ROPE_HALF)
    spread_cos = (freq_of_lane[:, None] == jnp.arange(ROPE_FREQ_ROWS)[None, :]).astype(F32)
    spread_sin = spread_cos * sgn[0][:, None]

    x2 = x.reshape(rows, D_MODEL)
    pos3 = positions.reshape(rows // ROW_TILE, 1, ROW_TILE)
    h = x2
    for l in range(g_in.shape[0]):
        (q, ck, cv, ksa, vst, kw, vwt, ga, z, xbc, qx, gx, sm, smt) = _in_proj(
            h, pos3, g_in[l][None, :], _permute_w_in(w_in, l), inv_rows,
            spread_cos.astype(BF16), spread_sin.astype(BF16), batch=batch, seq=seq)

        cmp_end = jnp.minimum(jnp.arange(ncp) * CMP_STRIDE + CMP_LEN - 1, seq - 1)
        posc = positions[:, cmp_end][:, :, None]
        pad_w2 = lambda w: jnp.pad(w, ((0, 0), (0, LANES - HEAD_DIM))).astype(BF16)
        pos_rows = lambda p: jnp.broadcast_to(p.reshape(1, CMP_LEN * HEAD_DIM), (8, CMP_LEN * HEAD_DIM)).astype(BF16)
        kc, vc = _compress(ck, cv, posc,
                           w_cmp1_k[l].astype(BF16), pad_w2(w_cmp2_k[l]), pos_rows(cmp_pos_k[l]),
                           w_cmp1_v[l].astype(BF16), pad_w2(w_cmp2_v[l]), pos_rows(cmp_pos_v[l]),
                           invl, sgn, batch=batch, ncp=ncp)
        n_ix = jnp.arange(ncp)[None, :]
        j_ix = jnp.arange(SLC_SLOTS)[:, None]
        ovl = ((n_ix * CMP_STRIDE < j_ix * SLC_BLOCK + SLC_BLOCK)
               & (n_ix * CMP_STRIDE + CMP_LEN > j_ix * SLC_BLOCK)
               & (n_ix < ncp - (CMP_LEN // CMP_STRIDE - 1))).astype(BF16)
        bg = (batch, KV_GROUPS)
        cat = jnp.concatenate(
            [jnp.swapaxes(vc, 2, 3), jnp.ones(bg + (1, ncp), BF16),
             jnp.zeros(bg + (CAT_OVL_ROW0 - HEAD_DIM - 1, ncp), BF16),
             jnp.broadcast_to(ovl, bg + ovl.shape),
             jnp.zeros(bg + (CAT_ROWS - CAT_OVL_ROW0 - SLC_SLOTS, ncp), BF16)], axis=2)
        o_a = _nsa(q, kc, cat, ksa, vst, kw, vwt, ga, sm,
                   batch=batch, seq=seq, ncp=ncp, topk=topk)

        head_of_lane = jnp.arange(SSD_WIDTH) // SSD_HEAD_DIM
        k_ix = jnp.arange(LANES)[:, None]
        eh = (k_ix == DT_LANE0 + head_of_lane[None, :]).astype(BF16)
        ehw = (k_ix == DT_LANE0 + (jnp.arange(SSD_HEADS * LANES) // LANES)[None, :]).astype(BF16)
        t_ix = jnp.arange(SSD_CHUNK)
        tril = (t_ix[None, :] <= t_ix[:, None]).astype(BF16)
        ssd_consts = [conv_w[l], conv_b[l][None, :],
                      _lane_row(dt_bias[l], DT_LANE0), dt_bias[l].astype(F32)[:, None],
                      _lane_row(a_log[l], DT_LANE0), a_log[l].astype(F32)[:, None],
                      jnp.repeat(d_skip[l].astype(F32), SSD_HEAD_DIM)[None, :], g_ssd_norm[l][None, :],
                      eh, ehw, tril, tril.T]

        km, vm = _mem_kv(mem, g_mem[l][None, :], w_mem_kv[l].astype(BF16))
        assert g_in.shape[0] == 1, "the final RMSNorm is fused into the (single) layer's last kernel"
        h = _ssd_out(h, o_a, qx, gx, km, vm, w_out[l].astype(BF16), g_final[None, :],
                     xbc, z, sm, smt, ssd_consts, batch=batch, seq=seq)
    return h.reshape(batch, seq, D_MODEL)


def kernel(x, mem, positions, g_in, w_in, cmp_pos_k, w_cmp1_k, w_cmp2_k, cmp_pos_v, w_cmp1_v, w_cmp2_v,
           conv_w, conv_b, dt_bias, a_log, d_skip, g_ssd_norm, g_mem, w_mem_kv, w_out, g_final):
    return _forward(x, mem, positions, g_in, w_in, cmp_pos_k, w_cmp1_k, w_cmp2_k, cmp_pos_v, w_cmp1_v,
                    w_cmp2_v, conv_w, conv_b, dt_bias, a_log, d_skip, g_ssd_norm, g_mem, w_mem_kv,
                    w_out, g_final)
```

```python
import functools

import jax
import jax.numpy as jnp
from jax import lax
from jax.experimental import pallas as pl
from jax.experimental.pallas import tpu as pltpu

F32 = jnp.float32
BF16 = jnp.bfloat16
I32 = jnp.int32

D_MODEL = 1024
NSA_HEADS = 8
HEAD_DIM = 64
KV_GROUPS = 2
HEADS_PER_GROUP = NSA_HEADS // KV_GROUPS
NSA_WIDTH = NSA_HEADS * HEAD_DIM
KV_WIDTH = KV_GROUPS * HEAD_DIM
CMP_LEN = 32
CMP_STRIDE = 16
CMP_HIDDEN = 256
SLC_BLOCK = 64
SLC_TOPK = 16
SLC_SLOTS = 64
WINDOW = 512
BIG = 1e9
NEG = -1e30

SSD_HEADS = 8
SSD_HEAD_DIM = 64
SSD_WIDTH = SSD_HEADS * SSD_HEAD_DIM
SSD_GROUPS = 2
SSD_STATE = 128
SSD_CONV = 4
SSD_CHUNK = 128
CONV_DIM = SSD_WIDTH + 2 * SSD_GROUPS * SSD_STATE

XA_HEADS = 4
XA_HEAD_DIM = 128
XA_WIDTH = XA_HEADS * XA_HEAD_DIM
MIX_WIDTH = NSA_WIDTH + SSD_WIDTH + XA_WIDTH

ROPE_THETA = 500000.0
ROPE_DIM = HEAD_DIM // 4
ROPE_HALF = ROPE_DIM // 2
ROPE_FREQ_ROWS = 16
EPS = 1e-6

LANES = 128
GATE_LANE0 = 0
DT_LANE0 = 24

OFF_Q = 0
OFF_KV = OFF_Q + NSA_WIDTH
OFF_GA = OFF_KV + 6 * KV_WIDTH
OFF_Z = OFF_GA + NSA_WIDTH
OFF_XBC = OFF_Z + SSD_WIDTH
OFF_QX = OFF_XBC + CONV_DIM
OFF_GX = OFF_QX + XA_WIDTH
OFF_SM = OFF_GX + XA_WIDTH
N_PROJ = OFF_SM + LANES

ROW_TILE = 512
NSA_TILE = 256
V_ROWS = 80
LOG2E = 1.4426950408889634
CAT_OVL_ROW0 = 80
CAT_ROWS = 144
VMEM_LIMIT = 56 * 1024 * 1024


def _nt(a, b):
    return lax.dot_general(a, b, (((1,), (1,)), ((), ())), preferred_element_type=F32)


def _dot(a, b):
    return jnp.dot(a, b, preferred_element_type=F32)


def _split2(x):
    hi = x.astype(BF16)
    return hi, (x - hi.astype(F32)).astype(BF16)


def _dot2_l(x, w):
    hi, lo = _split2(x)
    return _dot(hi, w) + _dot(lo, w)


def _dot2_r(w, x):
    hi, lo = _split2(x)
    return _dot(w, hi) + _dot(w, lo)


def _silu(x):
    h = 0.5 * x
    return h + h * jnp.tanh(h)


def _rope(a, cs, sn, first):
    r = jnp.where(first, pltpu.roll(a, LANES - ROPE_HALF, 1), pltpu.roll(a, ROPE_HALF, 1))
    return a * cs + r * sn


def _rope_tables(pos_f32, invl, sgn):
    ang = pos_f32 * invl
    return jnp.cos(ang), jnp.sin(ang) * sgn


def _inproj_body(x_ref, pos_ref, g_ref, w_ref, invc_ref, ec_ref, es_ref,
                 q_ref, ck_ref, cv_ref, ksa_ref, vst_ref, kw_ref, vwt_ref,
                 ga_ref, z_ref, xbc_ref, qx_ref, gx_ref, sm_ref, smt_ref, sk_ref, sv_ref, *, tm, tiles_per_seq):
    x = x_ref[...]
    ms = jnp.mean(x * x, axis=-1, keepdims=True)
    xn = (x * lax.rsqrt(ms + EPS) * g_ref[...]).astype(BF16)

    def mm(lo, n):
        return _dot(xn, w_ref[:, lo:lo + n])

    ga_ref[...] = mm(OFF_GA, NSA_WIDTH)
    z_ref[...] = mm(OFF_Z, SSD_WIDTH)

    ang = invc_ref[...] * pos_ref[0].astype(F32)
    cs = _dot2_r(ec_ref[...], jnp.cos(ang)).T
    sn = _dot2_r(es_ref[...], jnp.sin(ang)).T
    lane = lax.broadcasted_iota(I32, (tm, LANES), 1)
    first = (lane % HEAD_DIM) < ROPE_HALF

    qf = mm(OFF_Q, NSA_WIDTH)
    for c in range(NSA_WIDTH // LANES):
        sl = slice(c * LANES, (c + 1) * LANES)
        q_ref[:, sl] = (_rope(qf[:, sl], cs, sn, first) * (HEAD_DIM ** -0.5 * LOG2E)).astype(BF16)

    kv = mm(OFF_KV, 6 * KV_WIDTH)
    k_c, v_c, k_s, v_s, k_w, v_w = [kv[:, n * KV_WIDTH:(n + 1) * KV_WIDTH] for n in range(6)]
    k_s = _rope(k_s, cs, sn, first)
    k_w = _rope(k_w, cs, sn, first)
    s_base = (pl.program_id(0) % tiles_per_seq) * tm
    blk = (s_base + lax.broadcasted_iota(I32, (tm, LANES), 0)) // SLC_BLOCK
    onehot = jnp.where(lane - HEAD_DIM == blk, 1.0, 0.0)
    low = lane < HEAD_DIM
    vs_t = v_s.T
    vw_t = v_w.T
    tail = jnp.where(lax.broadcasted_iota(I32, (V_ROWS - HEAD_DIM, NSA_TILE), 0) == 0, 1.0, 0.0).astype(BF16)
    for g in range(KV_GROUPS):
        gs = slice(g * HEAD_DIM, (g + 1) * HEAD_DIM)
        kw_ref[0, g] = k_w[:, gs].astype(BF16)
        ks_g = k_s if g == 0 else pltpu.roll(k_s, HEAD_DIM, 1)
        ksa_ref[0, g] = jnp.where(low, ks_g, onehot).astype(BF16)
        for c in range(tm // NSA_TILE):
            cols = slice(c * NSA_TILE, (c + 1) * NSA_TILE)
            vst_ref[0, g, c, 0:HEAD_DIM, :] = vs_t[gs, cols].astype(BF16)
            vst_ref[0, g, c, HEAD_DIM:V_ROWS, :] = tail
            vwt_ref[0, g, c, 0:HEAD_DIM, :] = vw_t[gs, cols].astype(BF16)
            vwt_ref[0, g, c, HEAD_DIM:V_ROWS, :] = tail

    sk_ref[...] = k_c
    sv_ref[...] = v_c
    nchunk = tm // CMP_STRIDE
    for t in range(CMP_STRIDE):
        dst = slice(t * HEAD_DIM, (t + 1) * HEAD_DIM)
        kt = sk_ref[pl.ds(t, nchunk, stride=CMP_STRIDE), :].astype(BF16)
        vt = sv_ref[pl.ds(t, nchunk, stride=CMP_STRIDE), :].astype(BF16)
        for g in range(KV_GROUPS):
            gs = slice(g * HEAD_DIM, (g + 1) * HEAD_DIM)
            ck_ref[0, g, :, dst] = kt[:, gs]
            cv_ref[0, g, :, dst] = vt[:, gs]

    xbc_ref[...] = mm(OFF_XBC, CONV_DIM)
    qx_ref[...] = (mm(OFF_QX, XA_WIDTH) * (XA_HEAD_DIM ** -0.5 * LOG2E)).astype(BF16)
    gx_ref[...] = mm(OFF_GX, XA_WIDTH)
    sm = mm(OFF_SM, LANES)
    sm_ref[...] = sm
    smt_ref[0] = sm.T


def _in_proj(x2, pos3, g_in, w_p, invc, ec, es, *, batch, seq):
    rows = batch * seq
    tm = ROW_TILE
    tps = seq // tm
    row = lambda n: pl.BlockSpec((tm, n), lambda r: (r, 0))
    full = lambda a: pl.BlockSpec(a.shape, lambda r: (0,) * a.ndim, pipeline_mode=pl.Buffered(1))
    grp = lambda n: pl.BlockSpec((1, KV_GROUPS, tm, n), lambda r: (r // tps, 0, r % tps, 0))
    grp_shape = lambda n: jax.ShapeDtypeStruct((batch, KV_GROUPS, seq, n), BF16)
    chunk = pl.BlockSpec((1, KV_GROUPS, tm // CMP_STRIDE, CMP_STRIDE * HEAD_DIM),
                         lambda r: (r // tps, 0, r % tps, 0))
    chunk_shape = jax.ShapeDtypeStruct((batch, KV_GROUPS, seq // CMP_STRIDE, CMP_STRIDE * HEAD_DIM), BF16)
    vtile = pl.BlockSpec((1, KV_GROUPS, tm // NSA_TILE, V_ROWS, NSA_TILE), lambda r: (r // tps, 0, r % tps, 0, 0))
    vtile_shape = jax.ShapeDtypeStruct((batch, KV_GROUPS, seq // NSA_TILE, V_ROWS, NSA_TILE), BF16)
    flat = lambda n, dt: jax.ShapeDtypeStruct((rows, n), dt)
    return pl.pallas_call(
        functools.partial(_inproj_body, tm=tm, tiles_per_seq=tps),
        grid=(rows // tm,),
        in_specs=[row(D_MODEL), pl.BlockSpec((1, 1, tm), lambda r: (r, 0, 0)),
                  full(g_in), full(w_p), full(invc), full(ec), full(es)],
        out_specs=[row(NSA_WIDTH), chunk, chunk, grp(LANES), vtile, grp(HEAD_DIM), vtile,
                   row(NSA_WIDTH), row(SSD_WIDTH), row(CONV_DIM),
                   row(XA_WIDTH), row(XA_WIDTH), row(LANES),
                   pl.BlockSpec((1, LANES, tm), lambda r: (r, 0, 0))],
        out_shape=[flat(NSA_WIDTH, BF16), chunk_shape, chunk_shape, grp_shape(LANES), vtile_shape,
                   grp_shape(HEAD_DIM), vtile_shape,
                   flat(NSA_WIDTH, F32), flat(SSD_WIDTH, F32), flat(CONV_DIM, F32),
                   flat(XA_WIDTH, BF16), flat(XA_WIDTH, F32), flat(LANES, F32),
                   jax.ShapeDtypeStruct((rows // tm, LANES, tm), F32)],
        scratch_shapes=[pltpu.VMEM((tm, KV_WIDTH), F32), pltpu.VMEM((tm, KV_WIDTH), F32)],
        compiler_params=pltpu.CompilerParams(dimension_semantics=("parallel",),
                                             vmem_limit_bytes=VMEM_LIMIT),
        name="in_proj",
    )(x2, pos3, g_in, w_p, invc, ec, es)


def _compress_body(ck_ref, cv_ref, posc_ref, w1k_ref, w2k_ref, pk_ref, w1v_ref, w2v_ref, pv_ref,
                   invl_ref, sgn_ref, kc_ref, vc_ref, *, ncp):
    half = CMP_STRIDE * HEAD_DIM

    nr = KV_GROUPS * ncp

    def mlp(c_ref, w1_ref, w2_ref, p_ref):
        c = c_ref[0].reshape(nr, half)
        a = _dot(c, w1_ref[0:half, :])
        b = _dot(c, w1_ref[half:2 * half, :])
        bias = _dot(p_ref[...], w1_ref[...])[0:1, :]
        h = a + pltpu.roll(b, nr - 1, 0) + bias
        return _dot(_silu(h).astype(BF16), w2_ref[...])

    kc = mlp(ck_ref, w1k_ref, w2k_ref, pk_ref)
    vc = mlp(cv_ref, w1v_ref, w2v_ref, pv_ref)
    pos = posc_ref[0].astype(F32)
    cs, sn = _rope_tables(jnp.concatenate([pos] * KV_GROUPS, axis=0), invl_ref[...], sgn_ref[...])
    lane = lax.broadcasted_iota(I32, (nr, LANES), 1)
    kc = _rope(kc, cs, sn, (lane % HEAD_DIM) < ROPE_HALF)
    for g in range(KV_GROUPS):
        kc_ref[0, g] = kc[g * ncp:(g + 1) * ncp, 0:HEAD_DIM].astype(BF16)
        vc_ref[0, g] = vc[g * ncp:(g + 1) * ncp, 0:HEAD_DIM].astype(BF16)


def _compress(ck, cv, posc, w1k, w2k, pk, w1v, w2v, pv, invl, sgn, *, batch, ncp):
    chunk = pl.BlockSpec((1, KV_GROUPS, ncp, CMP_STRIDE * HEAD_DIM), lambda b: (b, 0, 0, 0))
    full = lambda a: pl.BlockSpec(a.shape, lambda b: (0,) * a.ndim, pipeline_mode=pl.Buffered(1))
    out = pl.BlockSpec((1, KV_GROUPS, ncp, HEAD_DIM), lambda b: (b, 0, 0, 0))
    shp = jax.ShapeDtypeStruct((batch, KV_GROUPS, ncp, HEAD_DIM), BF16)
    return pl.pallas_call(
        functools.partial(_compress_body, ncp=ncp),
        grid=(batch,),
        in_specs=[chunk, chunk, pl.BlockSpec((1, ncp, 1), lambda b: (b, 0, 0)),
                  full(w1k), full(w2k), full(pk), full(w1v), full(w2v), full(pv),
                  full(invl), full(sgn)],
        out_specs=[out, out],
        out_shape=[shp, shp],
        compiler_params=pltpu.CompilerParams(dimension_semantics=("parallel",),
                                             vmem_limit_bytes=VMEM_LIMIT),
        name="compress",
    )(ck, cv, posc, w1k, w2k, pk, w1v, w2v, pv, invl, sgn)


def _nsa_body(q_ref, kc_ref, cat_ref, ksa_ref, vst_ref, kw_ref, vwt_ref, ga_ref, sm_ref,
              o_ref, qa_ref, m_ref, acc_ref, oc_ref, sc_ref, mc_ref, gt_ref, gs_ref, *, tq, ncp, topk):
    g = pl.program_id(1)
    i = pl.program_id(2)
    s0 = i * tq
    nh = HEADS_PER_GROUP
    rows = nh * tq

    def col_max(s):
        n = s.shape[0]
        while n % 16 == 0:
            n //= 2
            s = jnp.maximum(s[0:n], s[n:2 * n])
        return jnp.max(s, axis=0, keepdims=True)

    def reset(br):
        m_ref[br] = jnp.full((1, rows), NEG, F32)
        acc_ref[br] = jnp.zeros((V_ROWS, rows), F32)

    def consume(br, s, vt_tile, mask, s_max=None):
        if mask is not None:
            s = jnp.where(mask, s, NEG)
        if s_max is None:
            s_max = col_max(s)
        m_prev = m_ref[br]
        m_new = jnp.maximum(m_prev, s_max)
        alpha = jnp.exp2(m_prev - m_new)
        p = jnp.exp2(s - m_new).astype(BF16)
        acc_ref[br] = alpha * acc_ref[br] + _dot(vt_tile, p)
        m_ref[br] = m_new

    def sel_scores(j):
        rows_j = pl.ds(pl.multiple_of(j * tq, tq), tq)
        return _nt(ksa_ref[0, 0, rows_j, :], qa_ref[...])

    def win_scores(j):
        rows_j = pl.ds(pl.multiple_of(j * tq, tq), tq)
        return _nt(kw_ref[0, 0, rows_j, :], qa_ref[:, 0:HEAD_DIM])

    q4 = q_ref[...]
    for e in range(nh):
        qa_ref[e * tq:(e + 1) * tq, 0:HEAD_DIM] = q4[:, e * HEAD_DIM:(e + 1) * HEAD_DIM]

    k_io = lax.broadcasted_iota(I32, (tq, rows), 0)
    t_io2 = lax.broadcasted_iota(I32, (tq, rows), 1) % tq
    diag = k_io <= t_io2
    cmp_per_blk = SLC_BLOCK // CMP_STRIDE

    def produce(slot, j):
        s = sel_scores(j)
        sc_ref[slot] = s
        mc_ref[slot] = col_max(s)

    def first_tile_and_gates():
        reset(0)
        produce(0, 0)
        gates_t = jax.nn.sigmoid(sm_ref[...]).T
        gs_ref[...] = _silu(ga_ref[...])

        def gate(e, c):
            c0 = GATE_LANE0 + e * 3 + c
            c1 = c0 + nh * 3
            return jnp.where(g == 0, gates_t[c0:c0 + 1, :], gates_t[c1:c1 + 1, :])

        for e in range(nh):
            cs = slice(e * tq, (e + 1) * tq)
            o_w = acc_ref[1, 0:HEAD_DIM, cs] / acc_ref[1, HEAD_DIM:HEAD_DIM + 1, cs]
            oc_ref[:, cs] = gate(e, 0) * oc_ref[:, cs] + gate(e, 2) * o_w
            gt_ref[e:e + 1, :] = gate(e, 1)

    def prologue(cap, nwin):
        nc = min(ncp, cap * cmp_per_blk)
        n_io = lax.broadcasted_iota(I32, (nc, rows), 0)
        t_io = s0 + lax.broadcasted_iota(I32, (nc, rows), 1) % tq
        s = jnp.where((n_io * CMP_STRIDE + (CMP_LEN - 1)) <= t_io,
                      _nt(kc_ref[0, 0, 0:nc, :], qa_ref[:, 0:HEAD_DIM]), NEG)
        sc_ref[0, 0:nc, :] = s
        mc_ref[0] = col_max(s)
        sc_ref[1] = win_scores(i)
        p = jnp.exp2(sc_ref[0, 0:nc, :] - mc_ref[0]).astype(BF16)
        big = _dot(cat_ref[0, 0, :, 0:nc], p)
        if nwin >= 2:
            sc_ref[0] = win_scores(i - 1)
        t_row = s0 + lax.broadcasted_iota(I32, (1, rows), 1) % tq
        inv = jnp.where(t_row >= CMP_LEN - 1, 1.0 / jnp.maximum(big[HEAD_DIM:HEAD_DIM + 1, :], 1e-30), 0.0)
        oc_ref[...] = big[0:HEAD_DIM, :] * inv
        imp_x = big[CAT_OVL_ROW0:CAT_OVL_ROW0 + cap, :] * inv
        imp = imp_x[:, 0:tq]
        for e in range(1, nh):
            imp = imp + imp_x[:, e * tq:(e + 1) * tq]

        j_io = lax.broadcasted_iota(I32, (cap, tq), 0)
        cur = (s0 + lax.broadcasted_iota(I32, (cap, tq), 1)) // SLC_BLOCK
        valid = j_io <= cur
        if cap > topk:
            TAKEN = -3e38
            TAKEN_BELOW = -1e37
            forced = (j_io == 0) | (j_io == cur) | (j_io == cur - 1)
            j_f = j_io.astype(F32)
            v = jnp.where(forced, TAKEN, jnp.where(valid, imp, -BIG))
            for _ in range(topk - 3):
                mx = jnp.max(v, axis=0, keepdims=True)
                idx = jnp.min(jnp.where(v == mx, j_f, float(SLC_SLOTS)), axis=0, keepdims=True)
                v = jnp.where(j_f == idx, TAKEN, v)
            valid = valid & (v < TAKEN_BELOW)
        selb = jnp.where(valid, 0.0, NEG)
        fill = [jnp.full((SLC_SLOTS - cap, tq), NEG, F32)] if cap < SLC_SLOTS else []
        selb_t = jnp.concatenate([jnp.zeros((SLC_SLOTS, tq), F32), selb] + fill, axis=0).T
        selb_t = selb_t.astype(BF16)

        reset(1)
        consume(1, sc_ref[1], vwt_ref[0, 0, i], diag)
        if nwin >= 3:
            sc_ref[1] = win_scores(i - 2)
        if nwin >= 2:
            consume(1, sc_ref[0], vwt_ref[0, 0, i - 1], None)
        if nwin >= 3:
            consume(1, sc_ref[1], vwt_ref[0, 0, i - 2], k_io > t_io2)
        for e in range(nh):
            qa_ref[e * tq:(e + 1) * tq, HEAD_DIM:LANES] = selb_t[:, HEAD_DIM:LANES]
        first_tile_and_gates()

    blk_per_tile = tq // SLC_BLOCK
    n_tiles = SLC_SLOTS // blk_per_tile
    caps = sorted({min(topk, SLC_SLOTS), SLC_SLOTS // 2, 3 * SLC_SLOTS // 4, SLC_SLOTS})
    cap_of = lambda t: next(c for c in caps if (t + 1) * blk_per_tile <= c)
    starts = sorted({0, 1, 2} | {c // blk_per_tile for c in caps[:-1]})
    for n, first in enumerate(starts):
        last = starts[n + 1] - 1 if n + 1 < len(starts) else n_tiles - 1
        pl.when((i >= first) & (i <= last))(functools.partial(prologue, cap_of(last), min(first, 2) + 1))

    def sel_pair(j):
        produce(1, j + 1)
        consume(0, sc_ref[0], vst_ref[0, 0, j], None, mc_ref[0])
        produce(0, j + 2)
        consume(0, sc_ref[1], vst_ref[0, 0, j + 1], None, mc_ref[1])

    done = 0
    for width in (8, 4, 2):
        start = done

        def block(jj, carry, width=width, start=start):
            for u in range(0, width, 2):
                sel_pair(start + width * jj + u)
            return carry

        trips = (i - start) // width
        lax.fori_loop(0, trips, block, 0)
        done = start + trips * width

    def head_out(e):
        cs = slice(e * tq, (e + 1) * tq)
        o_s = acc_ref[0, 0:HEAD_DIM, cs] / acc_ref[0, HEAD_DIM:HEAD_DIM + 1, cs]
        return oc_ref[:, cs] + gt_ref[e:e + 1, :] * o_s

    def combine():
        for pr in range(nh // 2):
            ls = slice(pr * LANES, (pr + 1) * LANES)
            pair = jnp.concatenate([head_out(2 * pr), head_out(2 * pr + 1)], axis=0).T
            o_ref[:, ls] = (pair * gs_ref[:, ls]).astype(BF16)

    @pl.when(i % 2 == 0)
    def _():
        consume(0, sc_ref[0], vst_ref[0, 0, i], diag)
        combine()

    @pl.when(i % 2 == 1)
    def _():
        sc_ref[1] = sel_scores(i)
        consume(0, sc_ref[0], vst_ref[0, 0, i - 1], None, mc_ref[0])
        consume(0, sc_ref[1], vst_ref[0, 0, i], diag)
        combine()


def _nsa(q, kc, cat, ksa, vst, kw, vwt, ga, sm, *, batch, seq, ncp, topk):
    tq = NSA_TILE
    assert WINDOW == 2 * tq
    nq = seq // tq
    gw = HEADS_PER_GROUP * HEAD_DIM
    rows = HEADS_PER_GROUP * tq
    per_bg = lambda *shape: pl.BlockSpec((1, 1) + shape, lambda b, g, i: (b, g) + (0,) * len(shape))
    return pl.pallas_call(
        functools.partial(_nsa_body, tq=tq, ncp=ncp, topk=topk),
        grid=(batch, KV_GROUPS, nq),
        in_specs=[pl.BlockSpec((tq, gw), lambda b, g, i: (b * nq + i, g)),
                  per_bg(ncp, HEAD_DIM), per_bg(CAT_ROWS, ncp),
                  per_bg(seq, LANES), per_bg(nq, V_ROWS, tq), per_bg(seq, HEAD_DIM),
                  per_bg(nq, V_ROWS, tq),
                  pl.BlockSpec((tq, gw), lambda b, g, i: (b * nq + i, g)),
                  pl.BlockSpec((tq, LANES), lambda b, g, i: (b * nq + i, 0))],
        out_specs=pl.BlockSpec((tq, gw), lambda b, g, i: (b * nq + i, g)),
        out_shape=jax.ShapeDtypeStruct((batch * seq, NSA_WIDTH), BF16),
        scratch_shapes=[pltpu.VMEM((rows, LANES), BF16),
                        pltpu.VMEM((2, 1, rows), F32),
                        pltpu.VMEM((2, V_ROWS, rows), F32),
                        pltpu.VMEM((HEAD_DIM, rows), F32),
                        pltpu.VMEM((2, tq, rows), F32), pltpu.VMEM((2, 1, rows), F32),
                        pltpu.VMEM((2 * HEADS_PER_GROUP, tq), F32), pltpu.VMEM((tq, gw), F32)],
        compiler_params=pltpu.CompilerParams(
            dimension_semantics=("parallel", "parallel", "arbitrary"),
            vmem_limit_bytes=VMEM_LIMIT),
        name="nsa",
    )(q, kc, cat, ksa, vst, kw, vwt, ga, sm)


def _ssdout_body(x_ref, oa_ref, qx_ref, gx_ref, k_ref, v_ref, w_ref, g_ref,
                 xbc_ref, z_ref, sm_ref, dtt_ref, cw_ref, cb_ref, dtb_ref, dtbt_ref, al_ref, alt_ref,
                 dsk_ref, gn_ref, eh_ref, ehw_ref, tril_ref, triu_ref,
                 o_ref, acc_ref, mix_ref, ext_ref, xc_ref, y_ref, h_ref, *, ts, tiles_per_seq):
    L = SSD_CHUNK
    N = SSD_STATE
    P = SSD_HEAD_DIM
    E = SSD_HEADS // SSD_GROUPS
    gw = E * P
    pad = 8
    w_b = NSA_WIDTH
    w_c = NSA_WIDTH + SSD_WIDTH

    @pl.when(pl.program_id(0) % tiles_per_seq == 0)
    def _():
        ext_ref[0:pad, :] = jnp.zeros((pad, CONV_DIM), F32)
        h_ref[...] = jnp.zeros(h_ref.shape, F32)

    def attn_head(h):
        hs = slice(h * XA_HEAD_DIM, (h + 1) * XA_HEAD_DIM)
        s = _nt(qx_ref[:, hs], k_ref[0, :, hs])
        p = jnp.exp2(s - jnp.max(s, axis=-1, keepdims=True))
        inv = 1.0 / jnp.sum(p, axis=-1, keepdims=True)
        oc = _dot(p.astype(BF16), v_ref[0, :, hs]) * (inv * _silu(gx_ref[:, hs]))
        mix_ref[:, SSD_WIDTH + h * XA_HEAD_DIM:SSD_WIDTH + (h + 1) * XA_HEAD_DIM] = oc.astype(BF16)

    ext_ref[pad:pad + ts, :] = xbc_ref[...]
    assert SSD_CONV == 4
    ext = ext_ref[...]
    ext1 = pltpu.roll(ext, 1, 0)
    near = ext * cw_ref[3:4, :] + ext1 * cw_ref[2:3, :]
    far = ext * cw_ref[1:2, :] + ext1 * cw_ref[0:1, :]
    taps = near + pltpu.roll(far, 2, 0)
    xc_ref[...] = _silu(cb_ref[...] + taps[pad:pad + ts, :])
    ext_ref[0:pad, :] = ext_ref[ts:ts + pad, :]

    for h in range(XA_HEADS):
        attn_head(h)
    acc_ref[...] = (x_ref[...] + _dot(oa_ref[...], w_ref[0:w_b, :])
                    + _dot(mix_ref[:, SSD_WIDTH:], w_ref[w_c:, :]))

    a_row = -jnp.exp(al_ref[...]) * LOG2E
    a_col = -jnp.exp(alt_ref[...]) * LOG2E
    causal = lax.broadcasted_iota(I32, (L, L), 1) <= lax.broadcasted_iota(I32, (L, L), 0)
    low = lax.broadcasted_iota(I32, (ts, LANES), 1) < P
    nck = ts // L
    chunk = lambda c: slice(c * L, (c + 1) * L)

    xs = xc_ref[:, 0:SSD_WIDTH]
    dt = jax.nn.softplus(sm_ref[...] + dtb_ref[...])
    da = dt * a_row
    a_cs = jnp.concatenate([_dot2_r(tril_ref[...], da[chunk(c), :]) for c in range(nck)], axis=0)
    dt_x = _dot2_l(dt, eh_ref[...])
    acs_w = _dot2_l(a_cs, ehw_ref[...])
    acs_x = jnp.concatenate(
        [jnp.where(low, acs_w[:, (2 * n) * LANES:(2 * n + 1) * LANES],
                   acs_w[:, (2 * n + 1) * LANES:(2 * n + 2) * LANES]) for n in range(SSD_HEADS // 2)],
        axis=1)
    a_last = [acs_x[c * L + L - 1:c * L + L, :] for c in range(nck)]
    a_last_x = jnp.concatenate([jnp.broadcast_to(a, (L, SSD_WIDTH)) for a in a_last], axis=0)
    dat = jax.nn.softplus(dtt_ref[0, DT_LANE0:DT_LANE0 + SSD_HEADS, :] + dtbt_ref[...]) * a_col
    acs_t = [_dot2_l(dat[:, chunk(c)], triu_ref[...]) for c in range(nck)]
    xdt = xs * dt_x
    xdo = (xdt * jnp.exp2(a_last_x - acs_x)).astype(BF16)
    xdt_b = xdt.astype(BF16)
    y_ref[...] = xs * dsk_ref[...]
    pre = jnp.exp2(acs_x)

    for gi in range(SSD_GROUPS):
        gs = slice(gi * gw, (gi + 1) * gw)
        bm = xc_ref[:, SSD_WIDTH + gi * N:SSD_WIDTH + (gi + 1) * N]
        cm_b = xc_ref[:, SSD_WIDTH + SSD_GROUPS * N + gi * N:SSD_WIDTH + SSD_GROUPS * N + (gi + 1) * N].astype(BF16)
        bm_b = bm.astype(BF16)
        h_c = h_ref[gi]
        h_in = []
        for c in range(nck):
            h_in.append(h_c.astype(BF16))
            h_c = h_c * jnp.exp2(a_last[c][:, gs]) + _dot(bm[chunk(c), :].T.astype(BF16), xdo[chunk(c), gs])
        h_ref[gi] = h_c
        for c in range(nck):
            rs = chunk(c)
            cbm = _nt(cm_b[rs, :], bm_b[rs, :])
            y_ref[rs, gs] += _dot(cm_b[rs, :], h_in[c]) * pre[rs, gs]
            for e in range(E):
                h = gi * E + e
                hs = slice(h * P, (h + 1) * P)
                d = acs_w[rs, h * LANES:(h + 1) * LANES] - acs_t[c][h:h + 1, :]
                dec = jnp.exp2(jnp.where(causal, d, NEG))
                y_ref[rs, hs] += _dot((cbm * dec).astype(BF16), xdt_b[rs, hs])

    y = y_ref[...] * _silu(z_ref[...])
    ms = jnp.mean(y * y, axis=-1, keepdims=True)
    mix_ref[:, 0:SSD_WIDTH] = (y * lax.rsqrt(ms + EPS) * gn_ref[...]).astype(BF16)
    acc = acc_ref[...] + _dot(mix_ref[:, 0:SSD_WIDTH], w_ref[w_b:w_c, :])
    ms = jnp.mean(acc * acc, axis=-1, keepdims=True)
    o_ref[...] = acc * lax.rsqrt(ms + EPS) * g_ref[...]


def _ssd_out(x2, oa, qx, gx, km, vm, w_out, g_final, xbc, z, sm, smt, consts, *, batch, seq):
    rows = batch * seq
    ts = ROW_TILE
    tps = seq // ts
    mlen = km.shape[1]
    row = lambda n: pl.BlockSpec((ts, n), lambda r: (r, 0))
    full = lambda a: pl.BlockSpec(a.shape, lambda r: (0,) * a.ndim, pipeline_mode=pl.Buffered(1))
    mem = pl.BlockSpec((1, mlen, XA_WIDTH), lambda r: (r // tps, 0, 0))
    return pl.pallas_call(
        functools.partial(_ssdout_body, ts=ts, tiles_per_seq=tps),
        grid=(rows // ts,),
        in_specs=[row(D_MODEL), row(NSA_WIDTH), row(XA_WIDTH), row(XA_WIDTH), mem, mem,
                  full(w_out), full(g_final),
                  row(CONV_DIM), row(SSD_WIDTH), row(LANES), pl.BlockSpec((1, LANES, ts), lambda r: (r, 0, 0))]
                 + [full(a) for a in consts],
        out_specs=row(D_MODEL),
        out_shape=jax.ShapeDtypeStruct((rows, D_MODEL), F32),
        scratch_shapes=[pltpu.VMEM((ts, D_MODEL), F32), pltpu.VMEM((ts, SSD_WIDTH + XA_WIDTH), BF16),
                        pltpu.VMEM((ts + 8, CONV_DIM), F32), pltpu.VMEM((ts, CONV_DIM), F32),
                        pltpu.VMEM((ts, SSD_WIDTH), F32),
                        pltpu.VMEM((SSD_GROUPS, SSD_STATE, SSD_WIDTH // SSD_GROUPS), F32)],
        compiler_params=pltpu.CompilerParams(dimension_semantics=("arbitrary",),
                                             vmem_limit_bytes=VMEM_LIMIT),
        name="ssd_out",
    )(x2, oa, qx, gx, km, vm, w_out, g_final, xbc, z, sm, smt, *consts)


def _memkv_body(mem_ref, g_ref, w_ref, k_ref, v_ref):
    x = mem_ref[0]
    ms = jnp.mean(x * x, axis=-1, keepdims=True)
    xn = (x * lax.rsqrt(ms + EPS) * g_ref[...]).astype(BF16)
    k_ref[0] = _dot(xn, w_ref[:, 0:XA_WIDTH]).astype(BF16)
    v_ref[0] = _dot(xn, w_ref[:, XA_WIDTH:2 * XA_WIDTH]).astype(BF16)


def _mem_kv(mem, g_mem, w_kv):
    batch, mlen, _ = mem.shape
    full = lambda a: pl.BlockSpec(a.shape, lambda b: (0,) * a.ndim, pipeline_mode=pl.Buffered(1))
    out = pl.BlockSpec((1, mlen, XA_WIDTH), lambda b: (b, 0, 0))
    shp = jax.ShapeDtypeStruct((batch, mlen, XA_WIDTH), BF16)
    return pl.pallas_call(
        _memkv_body,
        grid=(batch,),
        in_specs=[pl.BlockSpec((1, mlen, D_MODEL), lambda b: (b, 0, 0)), full(g_mem), full(w_kv)],
        out_specs=[out, out],
        out_shape=[shp, shp],
        compiler_params=pltpu.CompilerParams(dimension_semantics=("parallel",),
                                             vmem_limit_bytes=VMEM_LIMIT),
        name="mem_kv",
    )(mem, g_mem, w_kv)


def _permute_w_in(w, layer):
    rb = 128
    return pl.pallas_call(
        _wprep_body,
        grid=(w.shape[1] // rb,),
        in_specs=[pl.BlockSpec((1, rb, w.shape[2]), lambda r: (layer, r, 0))],
        out_specs=pl.BlockSpec((rb, N_PROJ), lambda r: (r, 0)),
        out_shape=jax.ShapeDtypeStruct((w.shape[1], N_PROJ), BF16),
        compiler_params=pltpu.CompilerParams(dimension_semantics=("parallel",),
                                             vmem_limit_bytes=VMEM_LIMIT),
        name="w_prep",
    )(w)


def _wprep_body(w_ref, o_ref):
    sizes = [NSA_WIDTH] + [KV_WIDTH] * 6 + [NSA_HEADS * 3, NSA_WIDTH, SSD_WIDTH, CONV_DIM, SSD_HEADS,
                                          XA_WIDTH, XA_WIDTH]
    offs = [0]
    for s in sizes:
        offs.append(offs[-1] + s)
    w = w_ref[0]

    def move(dst, lo, hi):
        o_ref[:, dst:dst + hi - lo] = w[:, lo:hi].astype(BF16)

    move(OFF_Q, offs[0], offs[7])
    move(OFF_GA, offs[8], offs[11])
    move(OFF_QX, offs[12], offs[14])
    small = jnp.concatenate(
        [w[:, offs[7]:offs[8]], w[:, offs[11]:offs[12]],
         jnp.zeros((w.shape[0], LANES - NSA_HEADS * 3 - SSD_HEADS), w.dtype)], axis=1)
    o_ref[:, OFF_SM:OFF_SM + LANES] = small.astype(BF16)


def _lane_row(vals, lane0):
    return jnp.zeros((1, LANES), F32).at[0, lane0:lane0 + vals.shape[0]].set(vals.astype(F32))


def _forward(x, mem, positions, g_in, w_in, cmp_pos_k, w_cmp1_k, w_cmp2_k, cmp_pos_v, w_cmp1_v,
             w_cmp2_v, conv_w, conv_b, dt_bias, a_log, d_skip, g_ssd_norm, g_mem, w_mem_kv, w_out,
             g_final):
    batch, seq, _ = x.shape
    ncp = seq // CMP_STRIDE
    n_slc = seq // SLC_BLOCK
    assert n_slc <= SLC_SLOTS and seq % ROW_TILE == 0 and ncp <= NSA_TILE
    topk = min(SLC_TOPK, n_slc)
    rows = batch * seq

    inv = ROPE_THETA ** (-jnp.arange(0, ROPE_DIM, 2, dtype=F32) / ROPE_DIM)
    head_inv = jnp.concatenate([inv, inv, jnp.zeros((HEAD_DIM - ROPE_DIM,), F32)])
    invl = jnp.tile(head_inv, LANES // HEAD_DIM)[None, :]
    head_sgn = jnp.concatenate([-jnp.ones((ROPE_HALF,), F32), jnp.ones((ROPE_HALF,), F32),
                                jnp.zeros((HEAD_DIM - ROPE_DIM,), F32)])
    sgn = jnp.tile(head_sgn, LANES // HEAD_DIM)[None, :]
    inv_rows = jnp.concatenate([inv, jnp.zeros((ROPE_FREQ_ROWS - ROPE_HALF,), F32)])[:, None]
    d_ix = jnp.arange(LANES) % HEAD_DIM
    freq_of_lane = jnp.where(d_ix < ROPE_DIM, d_ix % ROPE_HALF, ROPE_HALF)
    spread_cos = (freq_of_lane[:, None] == jnp.arange(ROPE_FREQ_ROWS)[None, :]).astype(F32)
    spread_sin = spread_cos * sgn[0][:, None]

    x2 = x.reshape(rows, D_MODEL)
    pos3 = positions.reshape(rows // ROW_TILE, 1, ROW_TILE)
    h = x2
    for l in range(g_in.shape[0]):
        (q, ck, cv, ksa, vst, kw, vwt, ga, z, xbc, qx, gx, sm, smt) = _in_proj(
            h, pos3, g_in[l][None, :], _permute_w_in(w_in, l), inv_rows,
            spread_cos.astype(BF16), spread_sin.astype(BF16), batch=batch, seq=seq)

        cmp_end = jnp.minimum(jnp.arange(ncp) * CMP_STRIDE + CMP_LEN - 1, seq - 1)
        posc = positions[:, cmp_end][:, :, None]
        pad_w2 = lambda w: jnp.pad(w, ((0, 0), (0, LANES - HEAD_DIM))).astype(BF16)
        pos_rows = lambda p: jnp.broadcast_to(p.reshape(1, CMP_LEN * HEAD_DIM), (8, CMP_LEN * HEAD_DIM)).astype(BF16)
        kc, vc = _compress(ck, cv, posc,
                           w_cmp1_k[l].astype(BF16), pad_w2(w_cmp2_k[l]), pos_rows(cmp_pos_k[l]),
                           w_cmp1_v[l].astype(BF16), pad_w2(w_cmp2_v[l]), pos_rows(cmp_pos_v[l]),
                           invl, sgn, batch=batch, ncp=ncp)
        n_ix = jnp.arange(ncp)[None, :]
        j_ix = jnp.arange(SLC_SLOTS)[:, None]
        ovl = ((n_ix * CMP_STRIDE < j_ix * SLC_BLOCK + SLC_BLOCK)
               & (n_ix * CMP_STRIDE + CMP_LEN > j_ix * SLC_BLOCK)
               & (n_ix < ncp - (CMP_LEN // CMP_STRIDE - 1))).astype(BF16)
        bg = (batch, KV_GROUPS)
        cat = jnp.concatenate(
            [jnp.swapaxes(vc, 2, 3), jnp.ones(bg + (1, ncp), BF16),
             jnp.zeros(bg + (CAT_OVL_ROW0 - HEAD_DIM - 1, ncp), BF16),
             jnp.broadcast_to(ovl, bg + ovl.shape),
             jnp.zeros(bg + (CAT_ROWS - CAT_OVL_ROW0 - SLC_SLOTS, ncp), BF16)], axis=2)
        o_a = _nsa(q, kc, cat, ksa, vst, kw, vwt, ga, sm,
                   batch=batch, seq=seq, ncp=ncp, topk=topk)

        head_of_lane = jnp.arange(SSD_WIDTH) // SSD_HEAD_DIM
        k_ix = jnp.arange(LANES)[:, None]
        eh = (k_ix == DT_LANE0 + head_of_lane[None, :]).astype(BF16)
        ehw = (k_ix == DT_LANE0 + (jnp.arange(SSD_HEADS * LANES) // LANES)[None, :]).astype(BF16)
        t_ix = jnp.arange(SSD_CHUNK)
        tril = (t_ix[None, :] <= t_ix[:, None]).astype(BF16)
        ssd_consts = [conv_w[l], conv_b[l][None, :],
                      _lane_row(dt_bias[l], DT_LANE0), dt_bias[l].astype(F32)[:, None],
                      _lane_row(a_log[l], DT_LANE0), a_log[l].astype(F32)[:, None],
                      jnp.repeat(d_skip[l].astype(F32), SSD_HEAD_DIM)[None, :], g_ssd_norm[l][None, :],
                      eh, ehw, tril, tril.T]

        km, vm = _mem_kv(mem, g_mem[l][None, :], w_mem_kv[l].astype(BF16))
        assert g_in.shape[0] == 1, "the final RMSNorm is fused into the (single) layer's last kernel"
        h = _ssd_out(h, o_a, qx, gx, km, vm, w_out[l].astype(BF16), g_final[None, :],
                     xbc, z, sm, smt, ssd_consts, batch=batch, seq=seq)
    return h.reshape(batch, seq, D_MODEL)


def kernel(x, mem, positions, g_in, w_in, cmp_pos_k, w_cmp1_k, w_cmp2_k, cmp_pos_v, w_cmp1_v, w_cmp2_v,
           conv_w, conv_b, dt_bias, a_log, d_skip, g_ssd_norm, g_mem, w_mem_kv, w_out, g_final):
    return _forward(x, mem, positions, g_in, w_in, cmp_pos_k, w_cmp1_k, w_cmp2_k, cmp_pos_v, w_cmp1_v,
                    w_cmp2_v, conv_w, conv_b, dt_bias, a_log, d_skip, g_ssd_norm, g_mem, w_mem_kv,
                    w_out, g_final)
```
